```python
import math
import jax
import jax.numpy as jnp
from jax import lax
import numpy as np

D_MODEL = 1024
BATCH = 8
SEQ = 4096
DEPTH = 1
DEC_BATCH = 32
DEC_SEQ = 1
PAST_LEN = 16384
PAGE_SIZE = 128

HEAD_DIM = 64
N_HEADS_TOTAL = D_MODEL // HEAD_DIM
A_HEADS = N_HEADS_TOTAL // 2
A_DQK = HEAD_DIM // 2
A_DV = HEAD_DIM
B_HEADS = N_HEADS_TOTAL - A_HEADS
B_KV_HEADS = 2
B_GROUP = B_HEADS // B_KV_HEADS
B_HD = HEAD_DIM
CMP_BLOCK = 64
SLC_BLOCK = CMP_BLOCK
N_SELECT = 16
WINDOW = 512
ROPE_THETA = 500000.0
ROPE_FRACTION = 4
N_EXPERTS = 32
TOP_K = 4
D_FF = D_MODEL
SWIGLU_LIMIT = 7.0
SWIGLU_ALPHA = 1.702
Q_BLOCK = 128
MOE_BLOCK = 128
EPS = 1e-6
NEG_BIG = -1e30
FORCE_SCORE = 1e4
A_Q = A_HEADS * 2 * A_DQK
A_V = A_HEADS * A_DV
B_Q = B_HEADS * B_HD
B_KV = 3 * B_KV_HEADS * B_HD
B_GATES = 3 * B_HEADS
IN_WIDTH = 2 * A_Q + A_V + B_Q + 2 * B_KV + B_GATES
MIX_WIDTH = A_V + B_Q

kernel_name = 'hymba_diff_nsa_moe_decode_step'


def _rms(x, g):
    xf = x.astype(jnp.float32)
    y = xf * lax.rsqrt(jnp.mean(xf * xf, axis=-1, keepdims=True) + EPS)
    return (y * g.astype(jnp.float32)).astype(x.dtype)


def _rope(x, pos):
    d = x.shape[-1]
    rot = d // ROPE_FRACTION
    half = rot // 2
    inv = ROPE_THETA ** (-jnp.arange(half, dtype=jnp.float32) / half)
    ang = pos.astype(jnp.float32)[:, None] * inv[None, :]
    cos = jnp.cos(ang)[:, None, :]
    sin = jnp.sin(ang)[:, None, :]
    xf = x.astype(jnp.float32)
    x1, x2, rest = xf[..., :half], xf[..., half:rot], xf[..., rot:]
    out = jnp.concatenate([x1 * cos - x2 * sin, x2 * cos + x1 * sin, rest], axis=-1)
    return out.astype(x.dtype)


def _masked_softmax(s, mask):
    s = jnp.where(mask, s.astype(jnp.float32), NEG_BIG)
    e = jnp.where(mask, jnp.exp(s - jnp.max(s, axis=-1, keepdims=True)), 0.0)
    return e / jnp.maximum(jnp.sum(e, axis=-1, keepdims=True), 1e-30)


def _sweep(fn, n_q, *qs):
    qb = Q_BLOCK if n_q % Q_BLOCK == 0 else n_q
    nb = n_q // qb

    def body(i):
        start = i * qb
        return fn(start, *[lax.dynamic_slice_in_dim(q, start, qb, axis=1) for q in qs])

    out = jnp.moveaxis(lax.map(body, jnp.arange(nb, dtype=jnp.int32)), 0, 1)
    return out.reshape(out.shape[:1] + (n_q,) + out.shape[3:])


def _gather_pages(cache, page_table):
    g = cache[page_table]
    return g.reshape((g.shape[0], g.shape[1] * g.shape[2]) + g.shape[3:])


def _split_cols(proj):
    widths = (A_Q, A_Q, A_V, B_Q, B_KV, B_KV, B_GATES)
    idx = np.cumsum(widths)[:-1].tolist()
    return jnp.split(proj, idx, axis=-1)


def _diff_attention(q, k, v, offset, layer_idx, lam_params, sub_g):
    B, S = q.shape[:2]
    L = k.shape[1]
    lam_init = 0.8 - 0.6 * math.exp(-0.3 * layer_idx)
    lp = lam_params.astype(jnp.float32)
    lam = jnp.exp(jnp.sum(lp[0] * lp[1])) - jnp.exp(jnp.sum(lp[2] * lp[3])) + lam_init
    kpos = jnp.arange(L, dtype=jnp.int32)
    scale = A_DQK ** -0.5

    def block(start, qblk):
        n = qblk.shape[1]
        qpos = offset + start + jnp.arange(n, dtype=jnp.int32)
        s = jnp.einsum('bqhcd,bkhcd->bhcqk', qblk, k).astype(jnp.float32) * scale
        p = _masked_softmax(s, (kpos[None, :] <= qpos[:, None])[None, None, None])
        pd = p[:, :, 0] - lam * p[:, :, 1]
        return jnp.einsum('bhqk,bkhe->bqhe', pd, v.astype(jnp.float32))

    o = _sweep(block, S, q)
    o = _rms(o, sub_g) * (1.0 - lam_init)
    return o.reshape(B, S, A_V)


def _nsa_cmp_slc(q, k_all, v_all, offset, ck_g, cmp_pe, cmp_w):
    B, S = q.shape[:2]
    L = k_all.shape[1]
    nb = -(-L // CMP_BLOCK)
    pad = ((0, 0), (0, nb * CMP_BLOCK - L), (0, 0), (0, 0), (0, 0))
    kp = jnp.pad(k_all, pad).reshape(B, nb, CMP_BLOCK, 2, B_KV_HEADS, B_HD)
    vp = jnp.pad(v_all, pad).reshape(B, nb, CMP_BLOCK, 2, B_KV_HEADS, B_HD)
    ck = jnp.mean(kp[:, :, :, 0] + cmp_pe[0][None, None, :, None, :], axis=2) @ cmp_w[0]
    ck = _rms(ck, ck_g)
    cv = jnp.mean(vp[:, :, :, 0] + cmp_pe[1][None, None, :, None, :], axis=2) @ cmp_w[1]
    qpos = offset + jnp.arange(S, dtype=jnp.int32)
    blk = jnp.arange(nb, dtype=jnp.int32)
    complete = (blk[None, :] + 1) * CMP_BLOCK - 1 <= qpos[:, None]
    scale = B_HD ** -0.5
    s = jnp.einsum('bqgrd,bngd->bqgrn', q, ck).astype(jnp.float32) * scale
    p = _masked_softmax(s, complete[None, :, None, None, :])
    o_cmp = jnp.einsum('bqgrn,bngd->bqgrd', p, cv.astype(jnp.float32))
    cur = qpos // SLC_BLOCK
    forced = (blk[None, :] == cur[:, None]) | (blk[None, :] == 0)
    imp = jnp.sum(p, axis=3)
    score = jnp.where(forced[None, :, None, :], FORCE_SCORE,
                      jnp.where(complete[None, :, None, :], imp, -1.0))
    n_sel = min(N_SELECT, nb)
    _, sel = lax.top_k(score, n_sel)
    ks = jnp.moveaxis(kp[:, :, :, 1], 3, 1)
    vs = jnp.moveaxis(vp[:, :, :, 1], 3, 1)
    bi = jnp.arange(B)[:, None, None, None]
    gi = jnp.arange(B_KV_HEADS)[None, None, :, None]

    def block(start, qblk, selb):
        n = qblk.shape[1]
        qp = offset + start + jnp.arange(n, dtype=jnp.int32)
        kg = ks[bi, gi, selb]
        vg = vs[bi, gi, selb]
        kpos = selb[..., None] * SLC_BLOCK + jnp.arange(SLC_BLOCK, dtype=jnp.int32)
        mask = (kpos <= qp[None, :, None, None, None]).reshape(B, n, B_KV_HEADS, 1, n_sel * SLC_BLOCK)
        sc = jnp.einsum('bqgrd,bqgkjd->bqgrkj', qblk, kg).astype(jnp.float32) * scale
        pr = _masked_softmax(sc.reshape(B, n, B_KV_HEADS, B_GROUP, n_sel * SLC_BLOCK), mask)
        return jnp.einsum('bqgrm,bqgmd->bqgrd', pr,
                          vg.reshape(B, n, B_KV_HEADS, n_sel * SLC_BLOCK, B_HD).astype(jnp.float32))

    o_slc = _sweep(block, S, q, sel)
    return o_cmp, o_slc


def _nsa_window(q, ext, offset, w_pad):
    S = q.shape[1]
    scale = B_HD ** -0.5

    def block(start, qblk):
        n = qblk.shape[1]
        qp = offset + start + jnp.arange(n, dtype=jnp.int32)
        kv = lax.dynamic_slice_in_dim(ext, start, w_pad + n, axis=1)
        kpos = offset - w_pad + start + jnp.arange(w_pad + n, dtype=jnp.int32)
        dlt = qp[:, None] - kpos[None, :]
        mask = (kpos[None, :] >= 0) & (dlt >= 0) & (dlt < WINDOW)
        sc = jnp.einsum('bqgrd,bkgd->bqgrk', qblk, kv[:, :, 0]).astype(jnp.float32) * scale
        pr = _masked_softmax(sc, mask[None, :, None, None, :])
        return jnp.einsum('bqgrk,bkgd->bqgrd', pr, kv[:, :, 1].astype(jnp.float32))

    return _sweep(block, S, q)


def _moe(h, router_w, router_b, w_gu, b_gu, w_dn, b_dn):
    B, S, D = h.shape
    n = B * S
    na = n * TOP_K
    xt = h.reshape(n, D)
    logits = (xt @ router_w + router_b).astype(jnp.float32)
    top_val, top_idx = lax.top_k(logits, TOP_K)
    gate = jax.nn.softmax(top_val, axis=-1)
    flat_e = top_idx.reshape(-1)
    order = jnp.argsort(flat_e)
    sorted_e = flat_e[order]
    tok = order // TOP_K
    counts = jnp.bincount(flat_e, length=N_EXPERTS)
    padded = (counts + MOE_BLOCK - 1) // MOE_BLOCK * MOE_BLOCK
    ends_p = jnp.cumsum(padded)
    starts_p = ends_p - padded
    starts = jnp.cumsum(counts) - counts
    dest = starts_p[sorted_e] + jnp.arange(na, dtype=jnp.int32) - starts[sorted_e]
    n_blocks = -(-(na + N_EXPERTS * (MOE_BLOCK - 1)) // MOE_BLOCK)
    n_rows = n_blocks * MOE_BLOCK
    row_tok = jnp.full((n_rows,), n, jnp.int32).at[dest].set(tok.astype(jnp.int32))
    blk_e = jnp.minimum(jnp.searchsorted(ends_p, jnp.arange(n_blocks, dtype=jnp.int32) * MOE_BLOCK, side='right'),
                        N_EXPERTS - 1)
    x_pad = jnp.concatenate([xt, jnp.zeros((1, D), xt.dtype)], axis=0)

    def expert_block(args):
        rows, e = args
        gu = x_pad[rows] @ w_gu[e] + b_gu[e]
        g, u = jnp.split(gu, 2, axis=-1)
        g = jnp.minimum(g, SWIGLU_LIMIT)
        u = jnp.clip(u, -SWIGLU_LIMIT, SWIGLU_LIMIT)
        a = g * jax.nn.sigmoid(SWIGLU_ALPHA * g) * (u + 1.0)
        return a @ w_dn[e] + b_dn[e]

    y_rows = lax.map(expert_block, (row_tok.reshape(n_blocks, MOE_BLOCK), blk_e)).reshape(n_rows, D)
    y = y_rows[dest] * gate.reshape(-1)[order][:, None].astype(h.dtype)
    return jax.ops.segment_sum(y, tok, num_segments=n).reshape(B, S, D)


def _layer(x, c, offset, past, w_buf, layer_idx,
           attn_norm_g, ffn_norm_g, ada_w, ada_b, w_in, w_out,
           diff_q_norm_g, diff_k_norm_g, diff_lambda, diff_sub_norm_g,
           nsa_q_norm_g, nsa_k_norm_g, nsa_ck_norm_g, nsa_cmp_pe, nsa_cmp_w,
           router_w, router_b, expert_w_gu, expert_b_gu, expert_w_down, expert_b_down):
    B, S, _ = x.shape
    pos = offset + jnp.arange(S, dtype=jnp.int32)
    mod = jax.nn.silu(c) @ ada_w + ada_b
    sh1, sc1, gt1, sh2, sc2, gt2 = jnp.split(mod[:, None, :], 6, axis=-1)
    h = _rms(x, attn_norm_g) * (1.0 + sc1) + sh1
    qa, ka, va, qb, kb, vb, gb = _split_cols(h @ w_in)
    qa = _rope(_rms(qa.reshape(B, S, 2 * A_HEADS, A_DQK), diff_q_norm_g), pos).reshape(B, S, A_HEADS, 2, A_DQK)
    ka = _rope(_rms(ka.reshape(B, S, 2 * A_HEADS, A_DQK), diff_k_norm_g), pos).reshape(B, S, A_HEADS, 2, A_DQK)
    va = va.reshape(B, S, A_HEADS, A_DV)
    new_diff_k = ka.reshape(B, S, A_HEADS, 2 * A_DQK)
    qn = _rope(_rms(qb.reshape(B, S, B_HEADS, B_HD), nsa_q_norm_g), pos).reshape(B, S, B_KV_HEADS, B_GROUP, B_HD)
    kb = kb.reshape(B, S, 3, B_KV_HEADS, B_HD)
    vb = vb.reshape(B, S, 3, B_KV_HEADS, B_HD)
    k_cmp = _rope(kb[:, :, 0], pos)
    k_slc = _rope(_rms(kb[:, :, 1], nsa_k_norm_g[0]), pos)
    k_win = _rope(_rms(kb[:, :, 2], nsa_k_norm_g[1]), pos)
    new_nsa_k = jnp.stack([k_cmp, k_slc], axis=2)
    new_nsa_v = vb[:, :, :2]
    new_win = jnp.stack([k_win, vb[:, :, 2]], axis=2)
    if past is None:
        ka_all, va_all = ka, va
        nsa_k_all, nsa_v_all = new_nsa_k, new_nsa_v
        win_ext = jnp.pad(new_win, ((0, 0), (WINDOW, 0), (0, 0), (0, 0), (0, 0)))
        w_pad = WINDOW
    else:
        past_dk, past_dv, past_nk, past_nv, past_win = past
        ka_all = jnp.concatenate([past_dk.reshape(B, past_dk.shape[1], A_HEADS, 2, A_DQK), ka], axis=1)
        va_all = jnp.concatenate([past_dv, va], axis=1)
        nsa_k_all = jnp.concatenate([past_nk, new_nsa_k], axis=1)
        nsa_v_all = jnp.concatenate([past_nv, new_nsa_v], axis=1)
        win_ext = jnp.concatenate([past_win, new_win], axis=1)
        w_pad = past_win.shape[1]
    new_win_state = win_ext[:, win_ext.shape[1] - w_buf:]
    o_a = _diff_attention(qa, ka_all, va_all, offset, layer_idx, diff_lambda, diff_sub_norm_g)
    o_cmp, o_slc = _nsa_cmp_slc(qn, nsa_k_all, nsa_v_all, offset, nsa_ck_norm_g, nsa_cmp_pe, nsa_cmp_w)
    o_win = _nsa_window(qn, win_ext, offset, w_pad)
    gates = jax.nn.sigmoid(gb.astype(jnp.float32)).reshape(B, S, 3, B_KV_HEADS, B_GROUP, 1)
    o_b = gates[:, :, 0] * o_cmp + gates[:, :, 1] * o_slc + gates[:, :, 2] * o_win
    mixed = jnp.concatenate([o_a.astype(x.dtype), o_b.reshape(B, S, B_Q).astype(x.dtype)], axis=-1)
    x = x + gt1 * (mixed @ w_out)
    h2 = _rms(x, ffn_norm_g) * (1.0 + sc2) + sh2
    x = x + gt2 * _moe(h2, router_w, router_b, expert_w_gu, expert_b_gu, expert_w_down, expert_b_down)
    return x, (new_diff_k, va, new_nsa_k, new_nsa_v, new_win_state)


def _stack_layers(states):
    return [jnp.stack([s[i] for s in states], axis=0) for i in range(5)]


def setup_inputs(seed: int = 0) -> dict:
    key = jax.random.key(seed)
    ks = list(jax.random.split(key, 40))

    def nrm(i, shape, scale):
        return jax.random.normal(ks[i], shape, jnp.float32) * scale

    n_pages = PAST_LEN // PAGE_SIZE
    n_used = DEC_BATCH * n_pages
    n_pool = n_used + n_used // 4
    w_buf = min(WINDOW, PAST_LEN)
    page_table = jax.random.permutation(ks[39], n_pool)[:n_used].reshape(DEC_BATCH, n_pages).astype(jnp.int32)
    d = D_MODEL
    return {
        'x_prompt': nrm(0, (BATCH, SEQ, d), 1.0),
        'x_sample': nrm(1, (DEC_BATCH, DEC_SEQ, d), 1.0),
        'c_prompt': nrm(2, (BATCH, d), 1.0),
        'c_sample': nrm(3, (DEC_BATCH, d), 1.0),
        'cache_diff_k': nrm(4, (DEPTH, n_pool, PAGE_SIZE, A_HEADS, 2 * A_DQK), 1.0),
        'cache_diff_v': nrm(5, (DEPTH, n_pool, PAGE_SIZE, A_HEADS, A_DV), 1.0),
        'cache_nsa_k': nrm(6, (DEPTH, n_pool, PAGE_SIZE, 2, B_KV_HEADS, B_HD), 1.0),
        'cache_nsa_v': nrm(7, (DEPTH, n_pool, PAGE_SIZE, 2, B_KV_HEADS, B_HD), 1.0),
        'state_win_kv': nrm(8, (DEPTH, DEC_BATCH, w_buf, 2, B_KV_HEADS, B_HD), 1.0),
        'page_table': page_table,
        'attn_norm_g': 1.0 + nrm(9, (DEPTH, d), 0.02),
        'ffn_norm_g': 1.0 + nrm(10, (DEPTH, d), 0.02),
        'ada_w': nrm(11, (DEPTH, d, 6 * d), 0.5 * d ** -0.5),
        'ada_b': nrm(12, (DEPTH, 6 * d), 0.02),
        'w_in': nrm(13, (DEPTH, d, IN_WIDTH), d ** -0.5),
        'w_out': nrm(14, (DEPTH, MIX_WIDTH, d), MIX_WIDTH ** -0.5),
        'diff_q_norm_g': 1.0 + nrm(15, (DEPTH, A_DQK), 0.02),
        'diff_k_norm_g': 1.0 + nrm(16, (DEPTH, A_DQK), 0.02),
        'diff_lambda': nrm(17, (DEPTH, 4, A_DQK), 0.1),
        'diff_sub_norm_g': 1.0 + nrm(18, (DEPTH, A_DV), 0.02),
        'nsa_q_norm_g': 1.0 + nrm(19, (DEPTH, B_HD), 0.02),
        'nsa_k_norm_g': 1.0 + nrm(20, (DEPTH, 2, B_HD), 0.02),
        'nsa_ck_norm_g': 1.0 + nrm(21, (DEPTH, B_HD), 0.02),
        'nsa_cmp_pe': nrm(22, (DEPTH, 2, CMP_BLOCK, B_HD), 0.1),
        'nsa_cmp_w': nrm(23, (DEPTH, 2, B_HD, B_HD), B_HD ** -0.5),
        'router_w': nrm(24, (DEPTH, d, N_EXPERTS), d ** -0.5),
        'router_b': nrm(25, (DEPTH, N_EXPERTS), 0.01),
        'expert_w_gu': nrm(26, (DEPTH, N_EXPERTS, d, 2 * D_FF), d ** -0.5),
        'expert_b_gu': nrm(27, (DEPTH, N_EXPERTS, 2 * D_FF), 0.02),
        'expert_w_down': nrm(28, (DEPTH, N_EXPERTS, D_FF, d), D_FF ** -0.5),
        'expert_b_down': nrm(29, (DEPTH, N_EXPERTS, d), 0.02),
    }


def reference(x_prompt, x_sample, c_prompt, c_sample, cache_diff_k, cache_diff_v, cache_nsa_k, cache_nsa_v,
              state_win_kv, page_table, attn_norm_g, ffn_norm_g, ada_w, ada_b, w_in, w_out,
              diff_q_norm_g, diff_k_norm_g, diff_lambda, diff_sub_norm_g, nsa_q_norm_g, nsa_k_norm_g,
              nsa_ck_norm_g, nsa_cmp_pe, nsa_cmp_w, router_w, router_b, expert_w_gu, expert_b_gu,
              expert_w_down, expert_b_down):
    past_len = page_table.shape[1] * PAGE_SIZE
    w_buf = state_win_kv.shape[2]
    h_p, h_s = x_prompt, x_sample
    st_p, st_s = [], []
    for l in range(DEPTH):
        w = (attn_norm_g[l], ffn_norm_g[l], ada_w[l], ada_b[l], w_in[l], w_out[l],
             diff_q_norm_g[l], diff_k_norm_g[l], diff_lambda[l], diff_sub_norm_g[l],
             nsa_q_norm_g[l], nsa_k_norm_g[l], nsa_ck_norm_g[l], nsa_cmp_pe[l], nsa_cmp_w[l],
             router_w[l], router_b[l], expert_w_gu[l], expert_b_gu[l], expert_w_down[l], expert_b_down[l])
        past = (_gather_pages(cache_diff_k[l], page_table), _gather_pages(cache_diff_v[l], page_table),
                _gather_pages(cache_nsa_k[l], page_table), _gather_pages(cache_nsa_v[l], page_table),
                state_win_kv[l])
        h_p, new_p = _layer(h_p, c_prompt, 0, None, w_buf, l, *w)
        h_s, new_s = _layer(h_s, c_sample, past_len, past, w_buf, l, *w)
        st_p.append(new_p)
        st_s.append(new_s)
    p_diff_k, p_diff_v, p_nsa_k, p_nsa_v, p_win_kv = _stack_layers(st_p)
    s_diff_k, s_diff_v, s_nsa_k, s_nsa_v, s_win_kv = _stack_layers(st_s)
    return (h_p, h_s, p_diff_k, p_diff_v, p_nsa_k, p_nsa_v, p_win_kv,
            s_diff_k, s_diff_v, s_nsa_k, s_nsa_v, s_win_kv)
```

```python
import functools
import math

import jax
import jax.numpy as jnp
from jax import lax
from jax.experimental import pallas as pl
from jax.experimental.pallas import tpu as pltpu

F32 = jnp.float32
BF16 = jnp.bfloat16
I32 = jnp.int32
HI = lax.Precision.HIGHEST

HEAD_DIM = 64
A_HEADS = 8
A_DQK = 32
B_HEADS = 8
B_KV_HEADS = 2
B_GROUP = 4
CMP_BLOCK = 64
N_SELECT = 16
WINDOW = 512
ROPE_THETA = 500000.0
ROPE_FRACTION = 4
N_EXPERTS = 32
TOP_K = 4
SWIGLU_LIMIT = 7.0
SWIGLU_ALPHA = 1.702
EPS = 1e-6
NEG_BIG = -1e30
FORCE_SCORE = 1e4
PAGE = 128

A_Q = 512
IN_PAD = 2944
LANES = 128
VMEM_LIMIT = 56 * 1024 * 1024


def _cparams(sem, vmem=VMEM_LIMIT):
    return pltpu.CompilerParams(dimension_semantics=sem, vmem_limit_bytes=vmem)


def _nt(a, b, precision=None):
    return lax.dot_general(a, b, (((1,), (1,)), ((), ())),
                           preferred_element_type=F32, precision=precision)


def _mm(a, b, precision=None):
    return jnp.dot(a, b, preferred_element_type=F32, precision=precision)


def _mod_kernel(c_ref, w_ref, b_ref, o_ref):
    c = c_ref[...]
    s = c / (1.0 + jnp.exp(-c))
    o_ref[...] = _mm(s, w_ref[...], HI) + b_ref[...]


def _modulation(c_all, ada_w, ada_b):
    n, d = c_all.shape
    width = ada_w.shape[1]
    tn = 1024
    return pl.pallas_call(
        _mod_kernel,
        out_shape=jax.ShapeDtypeStruct((n, width), F32),
        grid=(width // tn,),
        in_specs=[pl.BlockSpec((n, d), lambda j: (0, 0)),
                  pl.BlockSpec((d, tn), lambda j: (0, j)),
                  pl.BlockSpec((1, tn), lambda j: (0, j))],
        out_specs=pl.BlockSpec((n, tn), lambda j: (0, j)),
        compiler_params=_cparams(("arbitrary",)),
        name="adaln_mod",
    )(c_all, ada_w, ada_b.reshape(1, width))


def _group_norm(seg, gmat, gvec):
    out = []
    for j in range(seg.shape[1] // 256):
        c = seg[:, 256 * j:256 * (j + 1)]
        if gmat.dtype == F32:
            ms = _mm(c * c, gmat, HI)
        else:
            ms = _mm((c * c).astype(BF16), gmat)
        out.append(c * lax.rsqrt(ms + EPS) * gvec[:, 256 * j:256 * (j + 1)])
    return out[0] if len(out) == 1 else jnp.concatenate(out, axis=1)


def _rope_lanes(seg, tab_ref, half):
    w = seg.shape[1]
    rep = w // LANES
    cos = jnp.concatenate([tab_ref[0]] * rep, axis=1)
    s_up = jnp.concatenate([tab_ref[1]] * rep, axis=1)
    s_dn = jnp.concatenate([tab_ref[2]] * rep, axis=1)
    return (seg * cos + pltpu.roll(seg, w - half, 1) * s_up
            + pltpu.roll(seg, half, 1) * s_dn)


def _inproj_kernel(x_ref, sh_ref, sc_ref, g_ref, w_ref, gd_ref, gn_ref, gk_ref,
                   m32_ref, m64_ref, td_ref, tn_ref,
                   qa_ref, dk_ref, dkb_ref, dv_ref, dvb_ref, qn_ref, nk_ref, nv_ref,
                   win_ref, kb_ref, vb_ref, gate_ref):
    x = x_ref[0]
    ms = jnp.mean(x * x, axis=-1, keepdims=True)
    h = x * lax.rsqrt(ms + EPS) * g_ref[...]
    h = h * (1.0 + sc_ref[0]) + sh_ref[0]
    if w_ref.dtype == F32:
        proj = _mm(h, w_ref[...], HI)
    else:
        proj = _mm(h.astype(BF16), w_ref[...])
    m32 = m32_ref[...]
    m64 = m64_ref[...]
    gd = gd_ref[...]

    qa = _rope_lanes(_group_norm(proj[:, 0:512], m32, gd[0:1]), td_ref, 4)
    qa_ref[0] = (qa * (A_DQK ** -0.5)).astype(qa_ref.dtype)
    ka = _rope_lanes(_group_norm(proj[:, 512:1024], m32, gd[1:2]), td_ref, 4)
    dk_ref[0] = ka
    dkb_ref[0] = ka.astype(BF16)
    va = proj[:, 1024:1536]
    dv_ref[0] = va
    dvb_ref[0] = va.astype(BF16)
    qn = _rope_lanes(_group_norm(proj[:, 1536:2048], m64, gn_ref[...]), tn_ref, 8)
    qn_ref[0] = qn * (HEAD_DIM ** -0.5)
    k_cmp = _rope_lanes(proj[:, 2048:2176], tn_ref, 8)
    k_sw = _rope_lanes(_group_norm(proj[:, 2176:2432], m64, gk_ref[...]), tn_ref, 8)
    vb = proj[:, 2432:2816]
    nk_ref[0] = jnp.concatenate([k_cmp, k_sw[:, 0:128]], axis=1)
    nv_ref[0] = vb[:, 0:256]
    win_ref[0] = jnp.concatenate([k_sw[:, 128:256], vb[:, 256:384]], axis=1)
    kb_ref[0] = jnp.concatenate([k_cmp, k_sw], axis=1).astype(BF16)
    vb_ref[0] = vb.astype(BF16)
    gl = proj[:, 2816:2944]
    gate_ref[0] = 1.0 / (1.0 + jnp.exp(-gl))


def _rope_tables(pos, group, half):
    inv = ROPE_THETA ** (-jnp.arange(half, dtype=F32) / half)
    ang = pos.astype(F32)[:, None] * inv[None, :]
    cos, sin = jnp.cos(ang), jnp.sin(ang)
    n = pos.shape[0]
    pad = group - 2 * half
    c = jnp.concatenate([cos, cos, jnp.ones((n, pad), F32)], axis=1)
    up = jnp.concatenate([-sin, jnp.zeros((n, half + pad), F32)], axis=1)
    dn = jnp.concatenate([jnp.zeros((n, half), F32), sin, jnp.zeros((n, pad), F32)], axis=1)
    rep = LANES // group
    return jnp.stack([jnp.tile(c, (1, rep)), jnp.tile(up, (1, rep)), jnp.tile(dn, (1, rep))])


def _group_mean_matrix(group, dtype):
    i = jnp.arange(256)
    return jnp.where((i[:, None] // group) == (i[None, :] // group), 1.0 / group, 0.0).astype(dtype)


def _in_projection(x, sh, sc, pos, prm, ts, precise=False):
    bx, sx, d = x.shape
    r = sh.shape[1]
    rb = 1 if r == 1 else ts
    td = _rope_tables(pos, A_DQK, A_DQK // ROPE_FRACTION // 2)
    tn = _rope_tables(pos, HEAD_DIM, HEAD_DIM // ROPE_FRACTION // 2)

    def tok(width, dtype):
        return (jax.ShapeDtypeStruct((bx, sx, width), dtype),
                pl.BlockSpec((1, ts, width), lambda s, b: (b, s, 0)))

    sfx = "_f32" if precise else ""
    outs = [tok(512, F32 if precise else BF16), tok(512, F32), tok(512, BF16), tok(512, F32), tok(512, BF16),
            tok(512, F32), tok(256, F32), tok(256, F32), tok(256, F32), tok(384, BF16),
            tok(384, BF16), tok(128, F32)]
    const = lambda shape: pl.BlockSpec(shape, lambda s, b: (0,) * len(shape))
    mod_spec = pl.BlockSpec((1, rb, d), (lambda s, b: (b, 0, 0)) if r == 1 else (lambda s, b: (b, s, 0)))
    return pl.pallas_call(
        _inproj_kernel,
        out_shape=[o[0] for o in outs],
        grid=(sx // ts, bx),
        in_specs=[pl.BlockSpec((1, ts, d), lambda s, b: (b, s, 0)), mod_spec, mod_spec,
                  const((1, d)), const((d, IN_PAD)), const((2, 512)), const((1, 512)),
                  const((1, 256)), const((256, 256)), const((256, 256)),
                  pl.BlockSpec((3, ts, LANES), lambda s, b: (0, s, 0)),
                  pl.BlockSpec((3, ts, LANES), lambda s, b: (0, s, 0))],
        out_specs=[o[1] for o in outs],
        compiler_params=_cparams(("arbitrary", "arbitrary")),
        name="in_projection",
    )(x, sh, sc, prm["attn_g"], prm["w_in" + sfx], prm["gd"], prm["gn"], prm["gk"],
      prm["m32" + sfx], prm["m64" + sfx], td, tn)


def _diff_lambda(lam_ref, lam_init):
    lp = lam_ref[...]
    a = jnp.sum(lp[0:1] * lp[1:2], axis=1, keepdims=True)
    b = jnp.sum(lp[2:3] * lp[3:4], axis=1, keepdims=True)
    return jnp.exp(a) - jnp.exp(b) + lam_init


def _head_rms(o, sg):
    lane = lax.broadcasted_iota(I32, (1, LANES), 1)
    sq = o * o
    s0 = jnp.sum(jnp.where(lane < HEAD_DIM, sq, 0.0), axis=1, keepdims=True)
    s1 = jnp.sum(jnp.where(lane >= HEAD_DIM, sq, 0.0), axis=1, keepdims=True)
    ms = jnp.where(lane < HEAD_DIM, s0, s1) * (1.0 / HEAD_DIM)
    return o * lax.rsqrt(ms + EPS) * sg


def _diff_attn_kernel(lam_ref, q_ref, k_ref, v_ref, sg_ref, o_ref, *, tq, lam_init):
    qi = pl.program_id(2)
    q = q_ref[0]
    lane = lax.broadcasted_iota(I32, (1, LANES), 1)
    lam = _diff_lambda(lam_ref, lam_init)
    zero = jnp.zeros_like(q)
    qm = [jnp.where((lane >= A_DQK * i) & (lane < A_DQK * (i + 1)), q, zero) for i in range(4)]
    first_head = lane < HEAD_DIM

    def chunk(j, carry, masked):
        ms, ls, accs = carry
        start = pl.multiple_of(j * tq, tq)
        kc = k_ref[0, pl.ds(start, tq), :]
        vc = v_ref[0, pl.ds(start, tq), :]
        if masked:
            row = lax.broadcasted_iota(I32, (tq, tq), 0)
            col = lax.broadcasted_iota(I32, (tq, tq), 1)
            causal = col <= row
        new_ms, new_ls, alphas, pvs = [], [], [], []
        for i in range(4):
            s = _nt(qm[i], kc)
            if masked:
                s = jnp.where(causal, s, NEG_BIG)
            m_new = jnp.maximum(ms[i], jnp.max(s, axis=1, keepdims=True))
            p = jnp.exp(s - m_new)
            alpha = jnp.exp(ms[i] - m_new)
            new_ms.append(m_new)
            new_ls.append(alpha * ls[i] + jnp.sum(p, axis=1, keepdims=True))
            alphas.append(alpha)
            pvs.append(_mm(p.astype(BF16), vc))
        new_accs = []
        for c in range(2):
            a = jnp.where(first_head, alphas[c], alphas[2 + c])
            new_accs.append(a * accs[c] + jnp.where(first_head, pvs[c], pvs[2 + c]))
        return tuple(new_ms), tuple(new_ls), tuple(new_accs)

    init = (tuple(jnp.full((tq, 1), NEG_BIG, F32) for _ in range(4)),
            tuple(jnp.zeros((tq, 1), F32) for _ in range(4)),
            tuple(jnp.zeros((tq, LANES), F32) for _ in range(2)))
    carry = lax.fori_loop(0, qi, lambda j, c: chunk(j, c, False), init)
    ms, ls, accs = chunk(qi, carry, True)
    o0 = accs[0] * jnp.where(first_head, 1.0 / ls[0], 1.0 / ls[2])
    o1 = accs[1] * jnp.where(first_head, 1.0 / ls[1], 1.0 / ls[3])
    o = o0 - lam * o1
    o_ref[0] = (_head_rms(o, sg_ref[...]) * (1.0 - lam_init)).astype(BF16)


def _diff_attention_prompt(qa, dkb, dvb, lam_p, sub_g2, lam_init, tq):
    b, s, _ = qa.shape
    kern = functools.partial(_diff_attn_kernel, tq=tq, lam_init=lam_init)
    return pl.pallas_call(
        kern,
        out_shape=jax.ShapeDtypeStruct((b, s, 512), BF16),
        grid=(b, A_HEADS // 2, s // tq),
        in_specs=[pl.BlockSpec((4, A_DQK), lambda b_, p, i: (0, 0)),
                  pl.BlockSpec((1, tq, LANES), lambda b_, p, i: (b_, i, p)),
                  pl.BlockSpec((1, s, LANES), lambda b_, p, i: (b_, 0, p)),
                  pl.BlockSpec((1, s, LANES), lambda b_, p, i: (b_, 0, p)),
                  pl.BlockSpec((1, LANES), lambda b_, p, i: (0, 0))],
        out_specs=pl.BlockSpec((1, tq, LANES), lambda b_, p, i: (b_, i, p)),
        compiler_params=_cparams(("arbitrary", "arbitrary", "arbitrary")),
        name="diff_attention_prompt",
    )(lam_p, qa, dkb, dvb, sub_g2)


def _softmax_rows(s):
    m = jnp.max(s, axis=1, keepdims=True)
    e = jnp.exp(s - m)
    return e / jnp.sum(e, axis=1, keepdims=True)


def _top_select(score, blk, n_sel):
    nb = score.shape[1]
    cnt = jnp.zeros(score.shape, F32)
    for i in range(nb):
        col = score[:, i:i + 1]
        gt = jnp.where(col > score, 1.0, 0.0)
        eq = jnp.where(col == score, 1.0, 0.0)
        cnt = cnt + gt + jnp.where(blk > i, eq, 0.0)
    return cnt < n_sel


def _compress(kc, pe_sum, w, nb):
    mean = (jnp.sum(kc.reshape(nb, CMP_BLOCK, HEAD_DIM), axis=1) + pe_sum) * (1.0 / CMP_BLOCK)
    return _mm(mean, w, HI)


def _nsa_prompt_kernel(q_ref, gate_ref, kcmp_ref, vcmp_ref, kslc_ref, vslc_ref, kwin_ref,
                       vwin_ref, pe_ref, cw_ref, ckg_ref, o_ref,
                       kaug, vs, kw, vw, ck, cv, *, tq, tk, seq):
    qi = pl.program_id(1)
    nb = seq // CMP_BLOCK
    n_sel = min(N_SELECT, nb)
    wk = WINDOW + tq

    @pl.when(qi == 0)
    def _():
        rowblk = lax.broadcasted_iota(I32, (seq, HEAD_DIM), 0) // CMP_BLOCK
        colblk = lax.broadcasted_iota(I32, (seq, HEAD_DIM), 1)
        onehot = jnp.where(rowblk == colblk, 1.0, 0.0).astype(BF16)
        pe_k = jnp.sum(pe_ref[0], axis=0, keepdims=True)
        pe_v = jnp.sum(pe_ref[1], axis=0, keepdims=True)
        for g in range(B_KV_HEADS):
            lo, hi = HEAD_DIM * g, HEAD_DIM * (g + 1)
            kaug[g] = jnp.concatenate([kslc_ref[0][:, lo:hi], onehot], axis=1)
            vs[g] = vslc_ref[0][:, lo:hi]
            kw[g] = kwin_ref[0][:, lo:hi]
            vw[g] = vwin_ref[0][:, lo:hi]
            c = _compress(kcmp_ref[0][:, lo:hi], pe_k, cw_ref[0], nb)
            ms = jnp.mean(c * c, axis=-1, keepdims=True)
            ck[g] = c * lax.rsqrt(ms + EPS) * ckg_ref[...]
            cv[g] = _compress(vcmp_ref[0][:, lo:hi], pe_v, cw_ref[1], nb)

    q = q_ref[0]
    gates = gate_ref[0]
    q0 = qi * tq
    qpos = q0 + lax.broadcasted_iota(I32, (tq, 1), 0)
    blk = lax.broadcasted_iota(I32, (1, nb), 1)
    complete = (blk + 1) * CMP_BLOCK - 1 <= qpos
    cur = qpos // CMP_BLOCK
    forced = blk * (blk - cur) == 0
    qpos4 = jnp.concatenate([qpos] * B_GROUP, axis=0)
    n_full = q0 // tk
    wstart = pl.multiple_of(jnp.maximum(q0 - WINDOW, 0), tq)
    kpos_w = wstart + lax.broadcasted_iota(I32, (1, wk), 1)
    dlt = qpos4 - kpos_w
    heads_out = []

    for g in range(B_KV_HEADS):
        qh = [q[:, HEAD_DIM * (B_GROUP * g + r):HEAD_DIM * (B_GROUP * g + r + 1)]
              for r in range(B_GROUP)]
        ckg = ck[g]
        cvg = cv[g]
        o_cmp = []
        imp = jnp.zeros((tq, nb), F32)
        for r in range(B_GROUP):
            s = jnp.where(complete, _nt(qh[r], ckg, HI), NEG_BIG)
            e = jnp.where(complete, jnp.exp(s - jnp.max(s, axis=1, keepdims=True)), 0.0)
            p = e / jnp.maximum(jnp.sum(e, axis=1, keepdims=True), 1e-30)
            o_cmp.append(_mm(p, cvg, HI))
            imp = imp + p
        score = jnp.where(forced, FORCE_SCORE, jnp.where(complete, imp, -1.0))
        sel = _top_select(score, blk, n_sel)
        bias = jnp.where(blk <= cur, jnp.where(sel, 0.0, NEG_BIG), NEG_BIG).astype(BF16)
        if nb < HEAD_DIM:
            bias = jnp.concatenate([bias, jnp.zeros((tq, HEAD_DIM - nb), BF16)], axis=1)

        q4 = jnp.concatenate([qh[r].astype(BF16) for r in range(B_GROUP)], axis=0)
        qaug = jnp.concatenate([q4, jnp.concatenate([bias] * B_GROUP, axis=0)], axis=1)

        def chunk(j, carry, masked, g=g, qaug=qaug):
            m, l, acc = carry
            start = pl.multiple_of(j * tk, tk)
            s = _nt(qaug, kaug[g, pl.ds(start, tk), :])
            if masked:
                kpos = start + lax.broadcasted_iota(I32, (1, tk), 1)
                s = jnp.where(kpos <= qpos4, s, NEG_BIG)
            m_new = jnp.maximum(m, jnp.max(s, axis=1, keepdims=True))
            p = jnp.exp(s - m_new)
            alpha = jnp.exp(m - m_new)
            l_new = alpha * l + jnp.sum(p, axis=1, keepdims=True)
            acc_new = alpha * acc + _mm(p.astype(BF16), vs[g, pl.ds(start, tk), :])
            return m_new, l_new, acc_new

        init = (jnp.full((B_GROUP * tq, 1), NEG_BIG, F32), jnp.zeros((B_GROUP * tq, 1), F32),
                jnp.zeros((B_GROUP * tq, HEAD_DIM), F32))
        carry = lax.fori_loop(0, n_full, lambda j, c: chunk(j, c, False), init)
        _, l, acc = chunk(n_full, carry, True)
        o_slc = acc / l

        sw = _nt(q4, kw[g, pl.ds(wstart, wk), :])
        sw = jnp.where(dlt >= 0, jnp.where(dlt < WINDOW, sw, NEG_BIG), NEG_BIG)
        o_win = _mm(_softmax_rows(sw).astype(BF16), vw[g, pl.ds(wstart, wk), :])

        for r in range(B_GROUP):
            h = B_GROUP * g + r
            rows = slice(r * tq, (r + 1) * tq)
            heads_out.append(gates[:, h:h + 1] * o_cmp[r]
                             + gates[:, 8 + h:9 + h] * o_slc[rows]
                             + gates[:, 16 + h:17 + h] * o_win[rows])
    o_ref[0] = jnp.concatenate(heads_out, axis=1).astype(BF16)


def _nsa_prompt(qn, gates, nk, nv, kb, vb, pe, cw, ckg, tq, tk):
    b, s, _ = qn.shape
    nb = s // CMP_BLOCK
    kern = functools.partial(_nsa_prompt_kernel, tq=tq, tk=tk, seq=s)
    full = lambda lane_blk: pl.BlockSpec((1, s, LANES), lambda b_, i: (b_, 0, lane_blk))
    const = lambda shape: pl.BlockSpec(shape, lambda b_, i: (0,) * len(shape))
    return pl.pallas_call(
        kern,
        out_shape=jax.ShapeDtypeStruct((b, s, 512), BF16),
        grid=(b, s // tq),
        in_specs=[pl.BlockSpec((1, tq, 512), lambda b_, i: (b_, i, 0)),
                  pl.BlockSpec((1, tq, LANES), lambda b_, i: (b_, i, 0)),
                  full(0), full(0), full(1), full(1), full(2), full(2),
                  const((2, CMP_BLOCK, HEAD_DIM)), const((2, HEAD_DIM, HEAD_DIM)),
                  const((1, HEAD_DIM))],
        out_specs=pl.BlockSpec((1, tq, 512), lambda b_, i: (b_, i, 0)),
        scratch_shapes=[pltpu.VMEM((2, s, LANES), BF16), pltpu.VMEM((2, s, HEAD_DIM), BF16),
                        pltpu.VMEM((2, s, HEAD_DIM), BF16), pltpu.VMEM((2, s, HEAD_DIM), BF16),
                        pltpu.VMEM((2, nb, HEAD_DIM), F32), pltpu.VMEM((2, nb, HEAD_DIM), F32)],
        compiler_params=_cparams(("arbitrary", "arbitrary")),
        name="nsa_prompt",
    )(qn, gates, nk, nv, kb, vb, kb, vb, pe, cw, ckg)


def _outproj_kernel(oa_ref, ob_ref, x_ref, gt_ref, sh_ref, sc_ref, g_ref, w_ref, rw_ref, rb_ref,
                    x1_ref, h2_ref, ids_ref, gates_ref):
    prec = HI if w_ref.dtype == F32 else None
    y = _mm(oa_ref[0], w_ref[0:512, :], prec) + _mm(ob_ref[0], w_ref[512:1024, :], prec)
    x1 = x_ref[0] + gt_ref[0] * y
    x1_ref[0] = x1
    ms = jnp.mean(x1 * x1, axis=-1, keepdims=True)
    h2 = x1 * lax.rsqrt(ms + EPS) * g_ref[...]
    h2 = h2 * (1.0 + sc_ref[0]) + sh_ref[0]
    h2_ref[0] = h2
    logits = _nt(rw_ref[...], h2, HI) + rb_ref[...]
    eidx = lax.broadcasted_iota(I32, logits.shape, 0)
    work = logits
    vals, ids = [], []
    for _ in range(TOP_K):
        m = jnp.max(work, axis=0, keepdims=True)
        idx = jnp.min(jnp.where(work == m, eidx, N_EXPERTS), axis=0, keepdims=True)
        vals.append(m)
        ids.append(idx)
        work = jnp.where(eidx == idx, -3e38, work)
    es = [jnp.exp(v - vals[0]) for v in vals]
    tot = es[0] + es[1] + es[2] + es[3]
    ids_ref[0] = jnp.concatenate(ids + ids, axis=0)
    gates_ref[0] = jnp.concatenate([e / tot for e in es] * 2, axis=0)


def _out_projection(oa, ob, x, gt, sh, sc, prm, ts, precise=False):
    bx, sx, d = x.shape
    r = gt.shape[1]
    rb = 1 if r == 1 else ts
    mod_spec = pl.BlockSpec((1, rb, d), (lambda b, s: (b, 0, 0)) if r == 1 else (lambda b, s: (b, s, 0)))
    const = lambda shape: pl.BlockSpec(shape, lambda b, s: (0,) * len(shape))
    tokspec = lambda w: pl.BlockSpec((1, ts, w), lambda b, s: (b, s, 0))
    nt = sx // ts
    return pl.pallas_call(
        _outproj_kernel,
        out_shape=[jax.ShapeDtypeStruct((bx, sx, d), F32), jax.ShapeDtypeStruct((bx, sx, d), F32),
                   jax.ShapeDtypeStruct((bx * nt, 8, ts), I32),
                   jax.ShapeDtypeStruct((bx * nt, 8, ts), F32)],
        grid=(bx, nt),
        in_specs=[tokspec(512), tokspec(512), tokspec(d), mod_spec, mod_spec, mod_spec,
                  const((1, d)), const((d, d)), const((N_EXPERTS, d)), const((N_EXPERTS, 1))],
        out_specs=[tokspec(d), tokspec(d),
                   pl.BlockSpec((1, 8, ts), lambda b, s: (b * nt + s, 0, 0)),
                   pl.BlockSpec((1, 8, ts), lambda b, s: (b * nt + s, 0, 0))],
        compiler_params=_cparams(("arbitrary", "arbitrary")),
        name="out_projection",
    )(oa, ob, x, gt, sh, sc, prm["ffn_g"], prm["w_out_f32" if precise else "w_out"],
      prm["router_wt"], prm["router_b"])


def _route_kernel(ids_ref, pos_ref, cnt_ref, carry, *, tt):
    @pl.when(pl.program_id(0) == 0)
    def _():
        carry[...] = jnp.zeros_like(carry)

    ids = ids_ref[0]
    e_iota = lax.broadcasted_iota(I32, (N_EXPERTS, tt), 0)
    hits = [ids[k:k + 1, :] == e_iota for k in range(TOP_K)]
    oh = jnp.zeros((N_EXPERTS, tt), F32)
    for k in range(TOP_K):
        oh = oh + jnp.where(hits[k], 1.0, 0.0)
    r = lax.broadcasted_iota(I32, (tt, tt), 0)
    c = lax.broadcasted_iota(I32, (tt, tt), 1)
    upper = jnp.where(r < c, 1.0, 0.0).astype(BF16)
    before = _mm(oh.astype(BF16), upper) + carry[:, 0:1]
    rows = [jnp.sum(jnp.where(hits[k], before, 0.0), axis=0, keepdims=True) for k in range(TOP_K)]
    pos_ref[0] = jnp.concatenate(rows + rows, axis=0).astype(I32)
    carry[...] = carry[...] + jnp.sum(oh, axis=1, keepdims=True)
    cnt_ref[...] = carry[...]


def _route_positions(ids3, tt):
    nt = ids3.shape[0]
    return pl.pallas_call(
        functools.partial(_route_kernel, tt=tt),
        out_shape=[jax.ShapeDtypeStruct((nt, 8, tt), I32),
                   jax.ShapeDtypeStruct((N_EXPERTS, LANES), F32)],
        grid=(nt,),
        in_specs=[pl.BlockSpec((1, 8, tt), lambda i: (i, 0, 0))],
        out_specs=[pl.BlockSpec((1, 8, tt), lambda i: (i, 0, 0)),
                   pl.BlockSpec((N_EXPERTS, LANES), lambda i: (0, 0))],
        scratch_shapes=[pltpu.VMEM((N_EXPERTS, LANES), F32)],
        compiler_params=_cparams(("arbitrary",)),
        name="moe_route_positions",
    )(ids3)


def _scatter_kernel(dest_hbm, h_ref, xs_in_hbm, xs_hbm, idx, isem, sem, *, tt):
    del xs_in_hbm
    i = pl.program_id(0)
    cp = pltpu.make_async_copy(dest_hbm.at[i], idx, isem)
    cp.start()
    cp.wait()

    def issue(t, _):
        for k in range(TOP_K):
            pltpu.make_async_copy(h_ref.at[pl.ds(t, 1)], xs_hbm.at[pl.ds(idx[k * tt + t], 1)],
                                  sem).start()
        return 0

    lax.fori_loop(0, tt, issue, 0)
    for k in range(TOP_K):
        pltpu.make_async_copy(h_ref, xs_hbm.at[pl.ds(0, tt)], sem).wait()


def _scatter_rows(dest2, h_all, n_rows, tt):
    nt = dest2.shape[0]
    d = h_all.shape[1]
    return pl.pallas_call(
        functools.partial(_scatter_kernel, tt=tt),
        out_shape=jax.ShapeDtypeStruct((n_rows, d), F32),
        grid=(nt,),
        in_specs=[pl.BlockSpec(memory_space=pl.ANY),
                  pl.BlockSpec((tt, d), lambda i: (i, 0)),
                  pl.BlockSpec(memory_space=pl.ANY)],
        out_specs=pl.BlockSpec(memory_space=pl.ANY),
        scratch_shapes=[pltpu.SMEM((TOP_K * tt,), I32), pltpu.SemaphoreType.DMA,
                        pltpu.SemaphoreType.DMA],
        input_output_aliases={2: 0},
        compiler_params=_cparams(("arbitrary",)),
        name="moe_scatter_rows",
    )(dest2, h_all, jnp.zeros((n_rows, d), F32))


def _expert_kernel(be_ref, nu_ref, x_ref, wgu_ref, bgu_ref, wdn_ref, bdn_ref, y_ref, *, d_ff):
    @pl.when(pl.program_id(0) < nu_ref[0])
    def _():
        gu = _mm(x_ref[...].astype(BF16), wgu_ref[0]) + bgu_ref[0]
        g = jnp.minimum(gu[:, :d_ff], SWIGLU_LIMIT)
        u = jnp.clip(gu[:, d_ff:], -SWIGLU_LIMIT, SWIGLU_LIMIT)
        a = g * (1.0 / (1.0 + jnp.exp(-SWIGLU_ALPHA * g))) * (u + 1.0)
        y_ref[...] = _mm(a.astype(BF16), wdn_ref[0]) + bdn_ref[0]

    @pl.when(pl.program_id(0) >= nu_ref[0])
    def _():
        y_ref[...] = jnp.zeros_like(y_ref)


def _expert_matmul(blk_e, n_used, xs, wgu, bgu, wdn, bdn, n_rows, tm):
    d = xs.shape[1]
    d_ff = wdn.shape[1]
    nblk = n_rows // tm
    row = lambda i, be, nu: (jnp.minimum(i, nu[0] - 1), 0)
    wsel = lambda i, be, nu: (be[jnp.minimum(i, nu[0] - 1)], 0, 0)
    return pl.pallas_call(
        functools.partial(_expert_kernel, d_ff=d_ff),
        out_shape=jax.ShapeDtypeStruct((n_rows, d), F32),
        grid_spec=pltpu.PrefetchScalarGridSpec(
            num_scalar_prefetch=2,
            grid=(nblk,),
            in_specs=[pl.BlockSpec((tm, d), row),
                      pl.BlockSpec((1, d, 2 * d_ff), wsel),
                      pl.BlockSpec((1, 1, 2 * d_ff), wsel),
                      pl.BlockSpec((1, d_ff, d), wsel),
                      pl.BlockSpec((1, 1, d), wsel)],
            out_specs=pl.BlockSpec((tm, d), lambda i, be, nu: (i, 0))),
        compiler_params=_cparams(("arbitrary",)),
        name="moe_expert_matmul",
    )(blk_e, n_used, xs, wgu, bgu, wdn, bdn)


def _combine_kernel(dest_hbm, y_hbm, x1_ref, gt_ref, gate_ref, o_ref, idx, buf, isem, sem, *, tt, tile0):
    b = pl.program_id(0)
    s = pl.program_id(1)
    i = tile0 + b * pl.num_programs(1) + s
    cp = pltpu.make_async_copy(dest_hbm.at[i], idx, isem)
    cp.start()
    cp.wait()

    def issue(t, _):
        for k in range(TOP_K):
            pltpu.make_async_copy(y_hbm.at[pl.ds(idx[k * tt + t], 1)], buf.at[k, pl.ds(t, 1)],
                                  sem).start()
        return 0

    lax.fori_loop(0, tt, issue, 0)
    for k in range(TOP_K):
        pltpu.make_async_copy(y_hbm.at[pl.ds(0, tt)], buf.at[k], sem).wait()
    gate = gate_ref[0]
    moe = gate[:, 0:1] * buf[0]
    for k in range(1, TOP_K):
        moe = moe + gate[:, k:k + 1] * buf[k]
    o_ref[0] = x1_ref[0] + gt_ref[0] * moe


def _combine(dest2, y_rows, x1, gt, gate_t, tile0, tt):
    bx, sx, d = x1.shape
    r = gt.shape[1]
    rb = 1 if r == 1 else tt
    nt = sx // tt
    mod_spec = pl.BlockSpec((1, rb, d), (lambda b, s: (b, 0, 0)) if r == 1 else (lambda b, s: (b, s, 0)))
    return pl.pallas_call(
        functools.partial(_combine_kernel, tt=tt, tile0=tile0),
        out_shape=jax.ShapeDtypeStruct((bx, sx, d), F32),
        grid=(bx, nt),
        in_specs=[pl.BlockSpec(memory_space=pl.ANY), pl.BlockSpec(memory_space=pl.ANY),
                  pl.BlockSpec((1, tt, d), lambda b, s: (b, s, 0)), mod_spec,
                  pl.BlockSpec((1, tt, TOP_K), lambda b, s: (b * nt + s, 0, 0))],
        out_specs=pl.BlockSpec((1, tt, d), lambda b, s: (b, s, 0)),
        scratch_shapes=[pltpu.SMEM((TOP_K * tt,), I32), pltpu.VMEM((TOP_K, tt, d), F32),
                        pltpu.SemaphoreType.DMA, pltpu.SemaphoreType.DMA],
        compiler_params=_cparams(("arbitrary", "arbitrary")),
        name="moe_combine",
    )(dest2, y_rows, x1, gt, gate_t)


def _expand_rows(v, emat, terms=3):
    out = None
    for _ in range(terms):
        part = v.astype(BF16)
        v = v - part.astype(F32)
        out = _mm(part, emat) if out is None else out + _mm(part, emat)
    return out


def _diff_decode_kernel(pt_ref, lam_ref, q_ref, kn_ref, vn_ref, sg_ref, seg_ref, e0_ref, e1_ref,
                        *rest, pps, lam_init):
    k_refs = rest[:pps]
    v_refs = rest[pps:2 * pps]
    o_ref = rest[2 * pps]
    m_s, l_s, a0_s, a1_s = rest[2 * pps + 1:]
    j = pl.program_id(1)
    q = q_ref[0]
    seg = seg_ref[...]
    e0 = e0_ref[...]
    e1 = e1_ref[...]

    @pl.when(j == 0)
    def _():
        kn = jnp.broadcast_to(kn_ref[0], (8, 512))
        m_s[...] = _expand_rows(kn * q, seg)
        l_s[...] = jnp.ones_like(l_s)
        row0 = lax.broadcasted_iota(I32, (8, 512), 0) == 0
        vn = jnp.where(row0, jnp.broadcast_to(vn_ref[0], (8, 512)), 0.0)
        a0_s[...] = vn
        a1_s[...] = vn

    for p in range(pps):
        kp = k_refs[p][0]
        vp = v_refs[p][0]
        s = _expand_rows(kp * q, seg)
        m_old = m_s[...]
        m_new = jnp.maximum(m_old, jnp.max(s, axis=0, keepdims=True))
        pr = jnp.exp(s - m_new[0:1])
        alpha = jnp.exp(m_old - m_new)
        m_s[...] = m_new
        l_s[...] = alpha * l_s[...] + jnp.sum(pr, axis=0, keepdims=True)
        for e_mat, acc in ((e0, a0_s), (e1, a1_s)):
            pe = _expand_rows(pr, e_mat, terms=2)
            contrib = jnp.sum((pe * vp).reshape(16, 8, 512), axis=0)
            acc[...] = _expand_rows(alpha, e_mat) * acc[...] + contrib

    @pl.when(j == pl.num_programs(1) - 1)
    def _():
        lam = _diff_lambda(lam_ref, lam_init)
        inv = 1.0 / l_s[...]
        o0 = jnp.sum(a0_s[...], axis=0, keepdims=True) * _expand_rows(inv, e0)[0:1]
        o1 = jnp.sum(a1_s[...], axis=0, keepdims=True) * _expand_rows(inv, e1)[0:1]
        o = o0 - lam * o1
        hm = jnp.where(lax.broadcasted_iota(I32, (512, 512), 0) // HEAD_DIM
                       == lax.broadcasted_iota(I32, (512, 512), 1) // HEAD_DIM, 1.0 / HEAD_DIM, 0.0)
        ms = _mm(jnp.broadcast_to(o * o, (8, 512)), hm, HI)
        y = o * lax.rsqrt(ms[0:1] + EPS) * sg_ref[...] * (1.0 - lam_init)
        o_ref[0] = jnp.broadcast_to(y, (8, 512))


def _diff_attention_sample(page_table, lam_p, q, k_new, v_new, sub_g8, cache_k, cache_v, lam_init, pps):
    nb, n_pages = page_table.shape
    steps = n_pages // pps
    lane = jnp.arange(512)
    col = jnp.arange(LANES)
    seg = jnp.where((lane[:, None] // A_DQK) == col[None, :], 1.0, 0.0).astype(BF16)
    e0 = jnp.where(col[:, None] == 2 * (lane[None, :] // HEAD_DIM), 1.0, 0.0).astype(BF16)
    e1 = jnp.where(col[:, None] == 2 * (lane[None, :] // HEAD_DIM) + 1, 1.0, 0.0).astype(BF16)
    const = lambda shape: pl.BlockSpec(shape, lambda b, j, pt: (0,) * len(shape))
    per_b = pl.BlockSpec((1, 1, 512), lambda b, j, pt: (b, 0, 0))

    def page_spec(p):
        return pl.BlockSpec((1, PAGE, 512), lambda b, j, pt: (pt[b, j * pps + p], 0, 0))

    kern = functools.partial(_diff_decode_kernel, pps=pps, lam_init=lam_init)
    out = pl.pallas_call(
        kern,
        out_shape=jax.ShapeDtypeStruct((nb, 8, 512), F32),
        grid_spec=pltpu.PrefetchScalarGridSpec(
            num_scalar_prefetch=1,
            grid=(nb, steps),
            in_specs=[const((4, A_DQK)), per_b, per_b, per_b, const((1, 512)), const((512, LANES)),
                      const((LANES, 512)), const((LANES, 512))]
                     + [page_spec(p) for p in range(pps)] * 2,
            out_specs=pl.BlockSpec((1, 8, 512), lambda b, j, pt: (b, 0, 0)),
            scratch_shapes=[pltpu.VMEM((8, LANES), F32), pltpu.VMEM((8, LANES), F32),
                            pltpu.VMEM((8, 512), F32), pltpu.VMEM((8, 512), F32)]),
        compiler_params=_cparams(("arbitrary", "arbitrary")),
        name="diff_attention_sample",
    )(page_table, lam_p, q, k_new, v_new, sub_g8, seg, e0, e1,
      *([cache_k] * pps), *([cache_v] * pps))
    return out[:, 0, :]


def _nsa_decode_cmp_kernel(pt_ref, q_ref, kn_ref, vn_ref, pe_ref, cw_ref, ckg_ref, *rest,
                           pps, nbp, past_len):
    k_refs = rest[:pps]
    v_refs = rest[pps:2 * pps]
    ocmp_ref, sel_ref = rest[2 * pps:2 * pps + 2]
    ksum, vsum = rest[2 * pps + 2:]
    j = pl.program_id(1)
    nb_past = past_len // CMP_BLOCK
    per_page = PAGE // CMP_BLOCK

    @pl.when(j == 0)
    def _():
        ksum[...] = jnp.zeros_like(ksum)
        vsum[...] = jnp.zeros_like(vsum)

    for p in range(pps):
        base = (j * pps + p) * per_page
        for ref, acc in ((k_refs[p], ksum), (v_refs[p], vsum)):
            pg = ref[0]
            sums = jnp.sum(pg.reshape(per_page, CMP_BLOCK, LANES), axis=1)
            for t in range(per_page):
                acc[pl.ds(base + t, 1), :] = sums[t:t + 1]

    @pl.when(j == pl.num_programs(1) - 1)
    def _():
        ksum[pl.ds(nb_past, 1), :] = kn_ref[0]
        vsum[pl.ds(nb_past, 1), :] = vn_ref[0]
        q = q_ref[0]
        blk = lax.broadcasted_iota(I32, (1, nbp), 1)
        qpos = past_len
        complete = (blk + 1) * CMP_BLOCK - 1 <= qpos
        cur = qpos // CMP_BLOCK
        forced = blk * (blk - cur) == 0
        pe_k = jnp.sum(pe_ref[0], axis=0, keepdims=True)
        pe_v = jnp.sum(pe_ref[1], axis=0, keepdims=True)
        lane16 = lax.broadcasted_iota(I32, (1, LANES), 1)
        outs = []
        sel_rows = []
        for g in range(B_KV_HEADS):
            lo, hi = HEAD_DIM * g, HEAD_DIM * (g + 1)
            c = _mm((ksum[...][:, lo:hi] + pe_k) * (1.0 / CMP_BLOCK), cw_ref[0], HI)
            ck = c * lax.rsqrt(jnp.mean(c * c, axis=-1, keepdims=True) + EPS) * ckg_ref[...]
            cv = _mm((vsum[...][:, lo:hi] + pe_v) * (1.0 / CMP_BLOCK), cw_ref[1], HI)
            qg = jnp.concatenate(
                [q[:, HEAD_DIM * (B_GROUP * g + r):HEAD_DIM * (B_GROUP * g + r + 1)]
                 for r in range(B_GROUP)] * 2, axis=0)
            s = jnp.where(complete, _nt(qg, ck, HI), NEG_BIG)
            e = jnp.where(complete, jnp.exp(s - jnp.max(s, axis=1, keepdims=True)), 0.0)
            p = e / jnp.maximum(jnp.sum(e, axis=1, keepdims=True), 1e-30)
            outs.append(_mm(p, cv, HI)[0:B_GROUP])
            imp = jnp.sum(p[0:B_GROUP], axis=0, keepdims=True)
            work = jnp.where(forced, FORCE_SCORE, jnp.where(complete, imp, -1.0))
            work = jnp.where(blk <= cur, work, -2.0)
            picked = jnp.zeros((1, LANES), I32)
            for t in range(N_SELECT):
                mx = jnp.max(work, axis=1, keepdims=True)
                first = jnp.min(jnp.where(work == mx, blk, nbp), axis=1, keepdims=True)
                picked = jnp.where(lane16 == t, first, picked)
                work = jnp.where(blk == first, -3.0, work)
            sel_rows.append(picked)
        ocmp_ref[0] = jnp.concatenate(outs + outs, axis=0)[0:8]
        sel_ref[0] = jnp.concatenate(sel_rows * 4, axis=0)


def _nsa_decode_cmp(page_table, q, k_new, v_new, pe, cw, ckg, cache_k, cache_v, past_len, pps):
    nb, n_pages = page_table.shape
    steps = n_pages // pps
    n_blocks = past_len // CMP_BLOCK + 1
    nbp = -(-n_blocks // LANES) * LANES
    const = lambda shape: pl.BlockSpec(shape, lambda b, j, pt: (0,) * len(shape))
    per_b = lambda w: pl.BlockSpec((1, 1, w), lambda b, j, pt: (b, 0, 0))

    def page_spec(p):
        return pl.BlockSpec((1, PAGE, LANES), lambda b, j, pt: (pt[b, j * pps + p], 0, 0))

    kern = functools.partial(_nsa_decode_cmp_kernel, pps=pps, nbp=nbp, past_len=past_len)
    return pl.pallas_call(
        kern,
        out_shape=[jax.ShapeDtypeStruct((nb, 8, HEAD_DIM), F32),
                   jax.ShapeDtypeStruct((nb, 8, LANES), I32)],
        grid_spec=pltpu.PrefetchScalarGridSpec(
            num_scalar_prefetch=1,
            grid=(nb, steps),
            in_specs=[per_b(512), per_b(LANES), per_b(LANES), const((2, CMP_BLOCK, HEAD_DIM)),
                      const((2, HEAD_DIM, HEAD_DIM)), const((1, HEAD_DIM))]
                     + [page_spec(p) for p in range(pps)] * 2,
            out_specs=[pl.BlockSpec((1, 8, HEAD_DIM), lambda b, j, pt: (b, 0, 0)),
                       pl.BlockSpec((1, 8, LANES), lambda b, j, pt: (b, 0, 0))],
            scratch_shapes=[pltpu.VMEM((nbp, LANES), F32), pltpu.VMEM((nbp, LANES), F32)]),
        compiler_params=_cparams(("arbitrary", "arbitrary")),
        name="nsa_sample_compressed",
    )(page_table, q, k_new, v_new, pe, cw, ckg, *([cache_k] * pps), *([cache_v] * pps))


def _pick_head(x, g):
    return jnp.where(g == 0, x[:, 0:HEAD_DIM], x[:, HEAD_DIM:LANES])


def _nsa_decode_mix_kernel(pt_ref, sel_ref, q_ref, gate_ref, ocmp_ref, kn_ref, vn_ref, wn_ref,
                           win_ref, *rest, past_len, n_sel):
    k_refs = rest[:n_sel]
    v_refs = rest[n_sel:2 * n_sel]
    o_ref = rest[2 * n_sel]
    b = pl.program_id(0)
    g = pl.program_id(1)
    q = q_ref[0]
    gates = gate_ref[0]
    ocmp = ocmp_ref[0, 0]
    nb_past = past_len // CMP_BLOCK
    win = win_ref[0]
    w_buf = win.shape[0]
    wrow = lax.broadcasted_iota(I32, (1, w_buf), 1)
    wlo = max(w_buf - WINDOW + 1, w_buf - past_len, 0)
    wmask = wrow >= wlo
    qg = jnp.concatenate([q[:, HEAD_DIM * r:HEAD_DIM * (r + 1)] for r in range(B_GROUP)] * 2,
                         axis=0)
    kn = _pick_head(kn_ref[0][:, LANES:2 * LANES], g)
    vn = _pick_head(vn_ref[0][:, LANES:2 * LANES], g)
    s_new = jnp.sum(qg * kn, axis=1, keepdims=True)
    ss = []
    for t in range(n_sel):
        valid = sel_ref[b, g * n_sel + t] < nb_past
        s = _nt(qg, _pick_head(k_refs[t][0], g), HI)
        ss.append(jnp.where(valid, s, NEG_BIG))
    m = s_new
    for s in ss:
        m = jnp.maximum(m, jnp.max(s, axis=1, keepdims=True))
    l = jnp.exp(s_new - m)
    acc = l * vn
    for t in range(n_sel):
        p = jnp.exp(ss[t] - m)
        l = l + jnp.sum(p, axis=1, keepdims=True)
        acc = acc + _mm(p, _pick_head(v_refs[t][0], g), HI)
    o_slc = acc / l
    kwn = _pick_head(wn_ref[0][:, 0:LANES], g)
    vwn = _pick_head(wn_ref[0][:, LANES:2 * LANES], g)
    sw_new = jnp.sum(qg * kwn, axis=1, keepdims=True)
    sw = jnp.where(wmask, _nt(qg, _pick_head(win[:, 0:LANES], g), HI), NEG_BIG)
    mw = jnp.maximum(sw_new, jnp.max(sw, axis=1, keepdims=True))
    pw = jnp.exp(sw - mw)
    pn = jnp.exp(sw_new - mw)
    lw = pn + jnp.sum(pw, axis=1, keepdims=True)
    o_win = (pn * vwn + _mm(pw, _pick_head(win[:, LANES:2 * LANES], g), HI)) / lw
    heads = []
    for r in range(B_GROUP):
        def gate(c, r=r):
            return jnp.where(g == 0, gates[:, 8 * c + r:8 * c + r + 1],
                             gates[:, 8 * c + B_GROUP + r:8 * c + B_GROUP + r + 1])
        heads.append(gate(0) * ocmp[r:r + 1] + gate(1) * o_slc[r:r + 1] + gate(2) * o_win[r:r + 1])
    o_ref[0] = jnp.broadcast_to(jnp.concatenate(heads, axis=1), (8, 256))


def _nsa_decode_mix(page_table, sel2, q, gates, ocmp, nk_new, nv_new, win_new, state_win,
                    cache_k, cache_v, past_len, n_sel):
    nb = page_table.shape[0]
    w_buf = state_win.shape[1]
    nb_past = past_len // CMP_BLOCK
    per_page = PAGE // CMP_BLOCK
    per_b = lambda r, w: pl.BlockSpec((1, r, w), lambda b, g, pt, sl: (b, 0, 0))

    def blk_spec(t):
        def imap(b, g, pt, sl):
            blk = jnp.minimum(sl[b, g * n_sel + t], nb_past - 1)
            return (pt[b, blk // per_page], blk % per_page, 1)
        return pl.BlockSpec((1, CMP_BLOCK, LANES), imap)

    kern = functools.partial(_nsa_decode_mix_kernel, past_len=past_len, n_sel=n_sel)
    out = pl.pallas_call(
        kern,
        out_shape=jax.ShapeDtypeStruct((nb, 8, 512), F32),
        grid_spec=pltpu.PrefetchScalarGridSpec(
            num_scalar_prefetch=2,
            grid=(nb, B_KV_HEADS),
            in_specs=[pl.BlockSpec((1, 1, 256), lambda b, g, pt, sl: (b, 0, g)),
                      per_b(1, LANES),
                      pl.BlockSpec((1, 1, B_GROUP, HEAD_DIM), lambda b, g, pt, sl: (b, g, 0, 0)),
                      per_b(1, 256), per_b(1, 256), per_b(1, 256), per_b(w_buf, 256)]
                     + [blk_spec(t) for t in range(n_sel)] * 2,
            out_specs=pl.BlockSpec((1, 8, 256), lambda b, g, pt, sl: (b, 0, g))),
        compiler_params=_cparams(("arbitrary", "arbitrary")),
        name="nsa_sample_mix",
    )(page_table, sel2, q, gates, ocmp, nk_new, nv_new, win_new, state_win,
      *([cache_k] * n_sel), *([cache_v] * n_sel))
    return out[:, 0, :]


def _moe(h2_list, ids_list, gates_list, x1_list, gt_list, prm, tt, tm):
    d = h2_list[0].shape[-1]
    n_group = [h.shape[0] * h.shape[1] for h in h2_list]
    n_tok = sum(n_group)
    nt = -(-n_tok // tt)
    ntp = nt * tt
    ids = jnp.concatenate(ids_list, axis=1)[:TOP_K]
    ids = jnp.pad(ids, ((0, 8 - TOP_K), (0, ntp - n_tok)), constant_values=N_EXPERTS)
    ids3 = ids.reshape(8, nt, tt).transpose(1, 0, 2)
    pos3, cnt = _route_positions(ids3, tt)
    counts = cnt[:, 0].astype(I32)
    padded = (counts + tm - 1) // tm * tm
    ends_p = jnp.cumsum(padded)
    starts_p = ends_p - padded
    na = n_tok * TOP_K
    nblk = -(-(na + N_EXPERTS * (tm - 1)) // tm)
    n_rows = nblk * tm
    ids4 = ids3[:, :TOP_K, :]
    valid = ids4 < N_EXPERTS
    dest = jnp.take(starts_p, jnp.minimum(ids4, N_EXPERTS - 1)) + pos3[:, :TOP_K, :]
    pad_rank = jnp.cumsum(jnp.where(valid, 0, 1).reshape(-1)).reshape(valid.shape) - 1
    dest_scatter = jnp.where(valid, dest, n_rows + pad_rank).reshape(nt, TOP_K * tt)
    dest_gather = jnp.where(valid, dest, 0).reshape(nt, TOP_K * tt)
    n_trash = (ntp - n_tok) * TOP_K
    blk_e = jnp.minimum(jnp.searchsorted(ends_p, jnp.arange(nblk, dtype=I32) * tm, side="right"),
                        N_EXPERTS - 1).astype(I32)
    n_used = (ends_p[-1:] // tm).astype(I32)

    h_all = jnp.concatenate([h.reshape(-1, d) for h in h2_list], axis=0)
    h_all = jnp.pad(h_all, ((0, ntp - n_tok), (0, 0)))
    xs = _scatter_rows(dest_scatter, h_all, n_rows + max(n_trash, 8), tt)
    y_rows = _expert_matmul(blk_e, n_used, xs, prm["w_gu"], prm["b_gu"], prm["w_dn"], prm["b_dn"],
                            n_rows, tm)

    gates = jnp.concatenate(gates_list, axis=1)[:TOP_K]
    gates = jnp.pad(gates, ((0, 0), (0, ntp - n_tok)))
    gate_t = gates.reshape(TOP_K, nt, tt).transpose(1, 2, 0)
    outs = []
    tok0 = 0
    for x1, gt, n in zip(x1_list, gt_list, n_group):
        bx, sx, _ = x1.shape
        tile0 = tok0 // tt
        if sx % tt:
            padn = tt - sx
            x1p = jnp.pad(x1, ((0, 0), (0, padn), (0, 0)))
            gtp = jnp.pad(gt, ((0, 0), (0, padn), (0, 0)))
            o = _combine(dest_gather, y_rows, x1p, gtp, gate_t[tile0:tile0 + 1], tile0, tt)[:, :sx]
        else:
            o = _combine(dest_gather, y_rows, x1, gt, gate_t[tile0:tile0 + bx * (sx // tt)], tile0, tt)
        outs.append(o)
        tok0 += n
    return outs


def _flatten_rows(a):
    return a.transpose(1, 0, 2).reshape(8, -1)


def kernel(x_prompt, x_sample, c_prompt, c_sample, cache_diff_k, cache_diff_v, cache_nsa_k, cache_nsa_v, state_win_kv, page_table, attn_norm_g, ffn_norm_g, ada_w, ada_b, w_in, w_out, diff_q_norm_g, diff_k_norm_g, diff_lambda, diff_sub_norm_g, nsa_q_norm_g, nsa_k_norm_g, nsa_ck_norm_g, nsa_cmp_pe, nsa_cmp_w, router_w, router_b, expert_w_gu, expert_b_gu, expert_w_down, expert_b_down):
    depth = w_in.shape[0]
    assert depth == 1, "single-layer trunk"
    bp, sp, d = x_prompt.shape
    bs, ss, _ = x_sample.shape
    assert ss == 1
    n_pages = page_table.shape[1]
    past_len = n_pages * PAGE
    w_buf = state_win_kv.shape[2]
    n_pool = cache_diff_k.shape[1]
    l = 0
    lam_init = 0.8 - 0.6 * math.exp(-0.3 * l)
    d_ff = expert_w_down.shape[2]

    prm = {
        "attn_g": attn_norm_g[l].reshape(1, d),
        "ffn_g": ffn_norm_g[l].reshape(1, d),
        "w_in": jnp.pad(w_in[l], ((0, 0), (0, IN_PAD - w_in.shape[2]))).astype(BF16),
        "gd": jnp.stack([jnp.tile(diff_q_norm_g[l], 16), jnp.tile(diff_k_norm_g[l], 16)]),
        "gn": jnp.tile(nsa_q_norm_g[l], 8).reshape(1, 512),
        "gk": jnp.concatenate([jnp.tile(nsa_k_norm_g[l, 0], 2), jnp.tile(nsa_k_norm_g[l, 1], 2)]).reshape(1, 256),
        "m32": _group_mean_matrix(A_DQK, BF16),
        "m64": _group_mean_matrix(HEAD_DIM, BF16),
        "m32_f32": _group_mean_matrix(A_DQK, F32),
        "m64_f32": _group_mean_matrix(HEAD_DIM, F32),
        "w_in_f32": jnp.pad(w_in[l], ((0, 0), (0, IN_PAD - w_in.shape[2]))),
        "w_out": w_out[l].astype(BF16),
        "w_out_f32": w_out[l],
        "router_wt": router_w[l].T,
        "router_b": router_b[l].reshape(N_EXPERTS, 1),
        "w_gu": expert_w_gu[l].astype(BF16),
        "b_gu": expert_b_gu[l].reshape(N_EXPERTS, 1, 2 * d_ff),
        "w_dn": expert_w_down[l].astype(BF16),
        "b_dn": expert_b_down[l].reshape(N_EXPERTS, 1, d),
    }
    lam_p = diff_lambda[l]
    sub_g2 = jnp.tile(diff_sub_norm_g[l], 2).reshape(1, LANES)
    sub_g8 = jnp.tile(diff_sub_norm_g[l], 8).reshape(1, 512)
    pe = nsa_cmp_pe[l]
    cw = nsa_cmp_w[l]
    ckg = nsa_ck_norm_g[l].reshape(1, HEAD_DIM)

    n_c = bp + bs
    n_cp = -(-n_c // 8) * 8
    c_all = jnp.pad(jnp.concatenate([c_prompt, c_sample], axis=0), ((0, n_cp - n_c), (0, 0)))
    mod = _modulation(c_all, ada_w[l], ada_b[l])
    mod_p = mod[:bp].reshape(bp, 1, 6, d)
    mod_s = mod[bp:n_c].reshape(1, bs, 6, d)
    sh1p, sc1p, gt1p, sh2p, sc2p, gt2p = [mod_p[:, :, i] for i in range(6)]
    sh1s, sc1s, gt1s, sh2s, sc2s, gt2s = [mod_s[:, :, i] for i in range(6)]

    ts = min(512, sp)
    pos_p = jnp.arange(sp, dtype=I32)
    (qa, dk, dkb, dv, dvb, qn, nk, nv, win, kb, vb, gate) = _in_projection(
        x_prompt, sh1p, sc1p, pos_p, prm, ts)
    tq = min(256, sp)
    o_a = _diff_attention_prompt(qa, dkb, dvb, lam_p, sub_g2, lam_init, tq)
    tqn = min(128, sp)
    o_b = _nsa_prompt(qn, gate, nk, nv, kb, vb, pe, cw, ckg, tqn, min(512, sp))
    x1p, h2p, idsp, gatesp = _out_projection(o_a, o_b, x_prompt, gt1p, sh2p, sc2p, prm, ts)

    xs_ = x_sample.reshape(1, bs, d)
    pos_s = jnp.full((bs,), past_len, I32)
    (qa_s, dk_s, _, dv_s, _, qn_s, nk_s, nv_s, win_s, _, _, gate_s) = _in_projection(
        xs_, sh1s, sc1s, pos_s, prm, bs, precise=True)
    ck2 = cache_diff_k[l].reshape(n_pool, PAGE, 512)
    cv2 = cache_diff_v[l].reshape(n_pool, PAGE, 512)
    nk2 = cache_nsa_k[l].reshape(n_pool, PAGE, 256)
    nv2 = cache_nsa_v[l].reshape(n_pool, PAGE, 256)
    pps = 8 if n_pages % 8 == 0 else 1
    as3 = lambda a: a.reshape(bs, 1, a.shape[-1])
    o_a_s = _diff_attention_sample(page_table, lam_p, as3(qa_s[0]), as3(dk_s[0]), as3(dv_s[0]),
                                   sub_g8, ck2, cv2, lam_init, pps)
    ocmp_s, sel_s = _nsa_decode_cmp(page_table, as3(qn_s[0]), as3(nk_s[0][:, :LANES]),
                                    as3(nv_s[0][:, :LANES]), pe, cw, ckg, nk2, nv2, past_len, pps)
    n_sel = min(N_SELECT, past_len // CMP_BLOCK + 1)
    sel2 = jnp.concatenate([sel_s[:, 0, :n_sel], sel_s[:, 1, :n_sel]], axis=1)
    state = state_win_kv[l].reshape(bs, w_buf, 256)
    o_b_s = _nsa_decode_mix(page_table, sel2, as3(qn_s[0]), as3(gate_s[0]),
                            ocmp_s.reshape(bs, B_KV_HEADS, B_GROUP, HEAD_DIM), as3(nk_s[0]), as3(nv_s[0]),
                            as3(win_s[0]), state, nk2, nv2, past_len, n_sel)
    x1s, h2s, idss, gatess = _out_projection(o_a_s.reshape(1, bs, 512), o_b_s.reshape(1, bs, 512),
                                             xs_, gt1s, sh2s, sc2s, prm, bs, precise=True)

    y_p, y_s = _moe([h2p, h2s], [_flatten_rows(idsp), _flatten_rows(idss)],
                    [_flatten_rows(gatesp), _flatten_rows(gatess)], [x1p, x1s], [gt2p, gt2s],
                    prm, 256, 256)

    p_win = win[:, sp - w_buf:] if sp >= w_buf else jnp.pad(win, ((0, 0), (w_buf - sp, 0), (0, 0)))
    s_win = jnp.concatenate([state, win_s[0][:, None, :]], axis=1)[:, 1:]
    return (y_p, y_s.reshape(bs, 1, d),
            dk.reshape(1, bp, sp, A_HEADS, 2 * A_DQK), dv.reshape(1, bp, sp, A_HEADS, HEAD_DIM),
            nk.reshape(1, bp, sp, 2, B_KV_HEADS, HEAD_DIM), nv.reshape(1, bp, sp, 2, B_KV_HEADS, HEAD_DIM),
            p_win.reshape(1, bp, w_buf, 2, B_KV_HEADS, HEAD_DIM),
            dk_s.reshape(1, bs, 1, A_HEADS, 2 * A_DQK), dv_s.reshape(1, bs, 1, A_HEADS, HEAD_DIM),
            nk_s.reshape(1, bs, 1, 2, B_KV_HEADS, HEAD_DIM), nv_s.reshape(1, bs, 1, 2, B_KV_HEADS, HEAD_DIM),
            s_win.reshape(1, bs, w_buf, 2, B_KV_HEADS, HEAD_DIM))
```

```python
import functools
import math

import jax
import jax.numpy as jnp
from jax import lax
from jax.experimental import pallas as pl
from jax.experimental.pallas import tpu as pltpu

F32 = jnp.float32
BF16 = jnp.bfloat16
I32 = jnp.int32
HI = lax.Precision.HIGHEST

HEAD_DIM = 64
A_HEADS = 8
A_DQK = 32
B_HEADS = 8
B_KV_HEADS = 2
B_GROUP = 4
CMP_BLOCK = 64
N_SELECT = 16
WINDOW = 512
ROPE_THETA = 500000.0
ROPE_FRACTION = 4
N_EXPERTS = 32
TOP_K = 4
SWIGLU_LIMIT = 7.0
SWIGLU_ALPHA = 1.702
EPS = 1e-6
NEG_BIG = -1e30
FORCE_SCORE = 1e4
PAGE = 128

A_Q = 512
IN_PAD = 2944
LANES = 128
VMEM_LIMIT = 56 * 1024 * 1024


def _cparams(sem, vmem=VMEM_LIMIT):
    return pltpu.CompilerParams(dimension_semantics=sem, vmem_limit_bytes=vmem)


def _nt(a, b, precision=None):
    return lax.dot_general(a, b, (((1,), (1,)), ((), ())),
                           preferred_element_type=F32, precision=precision)


def _mm(a, b, precision=None):
    return jnp.dot(a, b, preferred_element_type=F32, precision=precision)


def _mod_kernel(c_ref, w_ref, b_ref, o_ref):
    c = c_ref[...]
    s = c / (1.0 + jnp.exp(-c))
    o_ref[...] = _mm(s, w_ref[...], HI) + b_ref[...]


def _modulation(c_all, ada_w, ada_b):
    n, d = c_all.shape
    width = ada_w.shape[1]
    tn = 1024
    return pl.pallas_call(
        _mod_kernel,
        out_shape=jax.ShapeDtypeStruct((n, width), F32),
        grid=(width // tn,),
        in_specs=[pl.BlockSpec((n, d), lambda j: (0, 0)),
                  pl.BlockSpec((d, tn), lambda j: (0, j)),
                  pl.BlockSpec((1, tn), lambda j: (0, j))],
        out_specs=pl.BlockSpec((n, tn), lambda j: (0, j)),
        compiler_params=_cparams(("arbitrary",)),
        name="adaln_mod",
    )(c_all, ada_w, ada_b.reshape(1, width))


def _group_norm(seg, gmat, gvec):
    out = []
    for j in range(seg.shape[1] // 256):
        c = seg[:, 256 * j:256 * (j + 1)]
        if gmat.dtype == F32:
            ms = _mm(c * c, gmat, HI)
        else:
            ms = _mm((c * c).astype(BF16), gmat)
        out.append(c * lax.rsqrt(ms + EPS) * gvec[:, 256 * j:256 * (j + 1)])
    return out[0] if len(out) == 1 else jnp.concatenate(out, axis=1)


def _rope_lanes(seg, tab_ref, half):
    w = seg.shape[1]
    rep = w // LANES
    cos = jnp.concatenate([tab_ref[0]] * rep, axis=1)
    s_up = jnp.concatenate([tab_ref[1]] * rep, axis=1)
    s_dn = jnp.concatenate([tab_ref[2]] * rep, axis=1)
    return (seg * cos + pltpu.roll(seg, w - half, 1) * s_up
            + pltpu.roll(seg, half, 1) * s_dn)


def _inproj_kernel(x_ref, sh_ref, sc_ref, g_ref, w_ref, gd_ref, gn_ref, gk_ref,
                   m32_ref, m64_ref, td_ref, tn_ref,
                   qa_ref, dk_ref, dkb_ref, dv_ref, dvb_ref, qn_ref, nk_ref, nv_ref,
                   win_ref, kb_ref, vb_ref, gate_ref):
    x = x_ref[0]
    ms = jnp.mean(x * x, axis=-1, keepdims=True)
    h = x * lax.rsqrt(ms + EPS) * g_ref[...]
    h = h * (1.0 + sc_ref[0]) + sh_ref[0]
    if w_ref.dtype == F32:
        proj = _mm(h, w_ref[...], HI)
    else:
        proj = _mm(h.astype(BF16), w_ref[...])
    m32 = m32_ref[...]
    m64 = m64_ref[...]
    gd = gd_ref[...]

    qa = _rope_lanes(_group_norm(proj[:, 0:512], m32, gd[0:1]), td_ref, 4)
    qa_ref[0] = (qa * (A_DQK ** -0.5)).astype(qa_ref.dtype)
    ka = _rope_lanes(_group_norm(proj[:, 512:1024], m32, gd[1:2]), td_ref, 4)
    dk_ref[0] = ka
    dkb_ref[0] = ka.astype(BF16)
    va = proj[:, 1024:1536]
    dv_ref[0] = va
    dvb_ref[0] = va.astype(BF16)
    qn = _rope_lanes(_group_norm(proj[:, 1536:2048], m64, gn_ref[...]), tn_ref, 8)
    qn_ref[0] = qn * (HEAD_DIM ** -0.5)
    k_cmp = _rope_lanes(proj[:, 2048:2176], tn_ref, 8)
    k_sw = _rope_lanes(_group_norm(proj[:, 2176:2432], m64, gk_ref[...]), tn_ref, 8)
    vb = proj[:, 2432:2816]
    nk_ref[0] = jnp.concatenate([k_cmp, k_sw[:, 0:128]], axis=1)
    nv_ref[0] = vb[:, 0:256]
    win_ref[0] = jnp.concatenate([k_sw[:, 128:256], vb[:, 256:384]], axis=1)
    kb_ref[0] = jnp.concatenate([k_cmp, k_sw], axis=1).astype(BF16)
    vb_ref[0] = vb.astype(BF16)
    gl = proj[:, 2816:2944]
    gate_ref[0] = 1.0 / (1.0 + jnp.exp(-gl))


def _rope_tables(pos, group, half):
    inv = ROPE_THETA ** (-jnp.arange(half, dtype=F32) / half)
    ang = pos.astype(F32)[:, None] * inv[None, :]
    cos, sin = jnp.cos(ang), jnp.sin(ang)
    n = pos.shape[0]
    pad = group - 2 * half
    c = jnp.concatenate([cos, cos, jnp.ones((n, pad), F32)], axis=1)
    up = jnp.concatenate([-sin, jnp.zeros((n, half + pad), F32)], axis=1)
    dn = jnp.concatenate([jnp.zeros((n, half), F32), sin, jnp.zeros((n, pad), F32)], axis=1)
    rep = LANES // group
    return jnp.stack([jnp.tile(c, (1, rep)), jnp.tile(up, (1, rep)), jnp.tile(dn, (1, rep))])


def _group_mean_matrix(group, dtype):
    i = jnp.arange(256)
    return jnp.where((i[:, None] // group) == (i[None, :] // group), 1.0 / group, 0.0).astype(dtype)


def _in_projection(x, sh, sc, pos, prm, ts, precise=False):
    bx, sx, d = x.shape
    r = sh.shape[1]
    rb = 1 if r == 1 else ts
    td = _rope_tables(pos, A_DQK, A_DQK // ROPE_FRACTION // 2)
    tn = _rope_tables(pos, HEAD_DIM, HEAD_DIM // ROPE_FRACTION // 2)

    def tok(width, dtype):
        return (jax.ShapeDtypeStruct((bx, sx, width), dtype),
                pl.BlockSpec((1, ts, width), lambda s, b: (b, s, 0)))

    sfx = "_f32" if precise else ""
    outs = [tok(512, F32 if precise else BF16), tok(512, F32), tok(512, BF16), tok(512, F32), tok(512, BF16),
            tok(512, F32), tok(256, F32), tok(256, F32), tok(256, F32), tok(384, BF16),
            tok(384, BF16), tok(128, F32)]
    const = lambda shape: pl.BlockSpec(shape, lambda s, b: (0,) * len(shape))
    mod_spec = pl.BlockSpec((1, rb, d), (lambda s, b: (b, 0, 0)) if r == 1 else (lambda s, b: (b, s, 0)))
    return pl.pallas_call(
        _inproj_kernel,
        out_shape=[o[0] for o in outs],
        grid=(sx // ts, bx),
        in_specs=[pl.BlockSpec((1, ts, d), lambda s, b: (b, s, 0)), mod_spec, mod_spec,
                  const((1, d)), const((d, IN_PAD)), const((2, 512)), const((1, 512)),
                  const((1, 256)), const((256, 256)), const((256, 256)),
                  pl.BlockSpec((3, ts, LANES), lambda s, b: (0, s, 0)),
                  pl.BlockSpec((3, ts, LANES), lambda s, b: (0, s, 0))],
        out_specs=[o[1] for o in outs],
        compiler_params=_cparams(("arbitrary", "arbitrary")),
        name="in_projection",
    )(x, sh, sc, prm["attn_g"], prm["w_in" + sfx], prm["gd"], prm["gn"], prm["gk"],
      prm["m32" + sfx], prm["m64" + sfx], td, tn)


def _diff_lambda(lam_ref, lam_init):
    lp = lam_ref[...]
    a = jnp.sum(lp[0:1] * lp[1:2], axis=1, keepdims=True)
    b = jnp.sum(lp[2:3] * lp[3:4], axis=1, keepdims=True)
    return jnp.exp(a) - jnp.exp(b) + lam_init


def _head_rms(o, sg):
    lane = lax.broadcasted_iota(I32, (1, LANES), 1)
    sq = o * o
    s0 = jnp.sum(jnp.where(lane < HEAD_DIM, sq, 0.0), axis=1, keepdims=True)
    s1 = jnp.sum(jnp.where(lane >= HEAD_DIM, sq, 0.0), axis=1, keepdims=True)
    ms = jnp.where(lane < HEAD_DIM, s0, s1) * (1.0 / HEAD_DIM)
    return o * lax.rsqrt(ms + EPS) * sg


def _diff_attn_kernel(lam_ref, q_ref, k_ref, v_ref, sg_ref, o_ref, *, tq, lam_init):
    qi = pl.program_id(2)
    q = q_ref[0]
    lane = lax.broadcasted_iota(I32, (1, LANES), 1)
    lam = _diff_lambda(lam_ref, lam_init)
    zero = jnp.zeros_like(q)
    qm = [jnp.where((lane >= A_DQK * i) & (lane < A_DQK * (i + 1)), q, zero) for i in range(4)]
    first_head = lane < HEAD_DIM

    def chunk(j, carry, masked):
        ms, ls, accs = carry
        start = pl.multiple_of(j * tq, tq)
        kc = k_ref[0, pl.ds(start, tq), :]
        vc = v_ref[0, pl.ds(start, tq), :]
        if masked:
            row = lax.broadcasted_iota(I32, (tq, tq), 0)
            col = lax.broadcasted_iota(I32, (tq, tq), 1)
            causal = col <= row
        new_ms, new_ls, alphas, pvs = [], [], [], []
        for i in range(4):
            s = _nt(qm[i], kc)
            if masked:
                s = jnp.where(causal, s, NEG_BIG)
            m_new = jnp.maximum(ms[i], jnp.max(s, axis=1, keepdims=True))
            p = jnp.exp(s - m_new)
            alpha = jnp.exp(ms[i] - m_new)
            new_ms.append(m_new)
            new_ls.append(alpha * ls[i] + jnp.sum(p, axis=1, keepdims=True))
            alphas.append(alpha)
            pvs.append(_mm(p.astype(BF16), vc))
        new_accs = []
        for c in range(2):
            a = jnp.where(first_head, alphas[c], alphas[2 + c])
            new_accs.append(a * accs[c] + jnp.where(first_head, pvs[c], pvs[2 + c]))
        return tuple(new_ms), tuple(new_ls), tuple(new_accs)

    init = (tuple(jnp.full((tq, 1), NEG_BIG, F32) for _ in range(4)),
            tuple(jnp.zeros((tq, 1), F32) for _ in range(4)),
            tuple(jnp.zeros((tq, LANES), F32) for _ in range(2)))
    carry = lax.fori_loop(0, qi, lambda j, c: chunk(j, c, False), init)
    ms, ls, accs = chunk(qi, carry, True)
    o0 = accs[0] * jnp.where(first_head, 1.0 / ls[0], 1.0 / ls[2])
    o1 = accs[1] * jnp.where(first_head, 1.0 / ls[1], 1.0 / ls[3])
    o = o0 - lam * o1
    o_ref[0] = (_head_rms(o, sg_ref[...]) * (1.0 - lam_init)).astype(BF16)


def _diff_attention_prompt(qa, dkb, dvb, lam_p, sub_g2, lam_init, tq):
    b, s, _ = qa.shape
    kern = functools.partial(_diff_attn_kernel, tq=tq, lam_init=lam_init)
    return pl.pallas_call(
        kern,
        out_shape=jax.ShapeDtypeStruct((b, s, 512), BF16),
        grid=(b, A_HEADS // 2, s // tq),
        in_specs=[pl.BlockSpec((4, A_DQK), lambda b_, p, i: (0, 0)),
                  pl.BlockSpec((1, tq, LANES), lambda b_, p, i: (b_, i, p)),
                  pl.BlockSpec((1, s, LANES), lambda b_, p, i: (b_, 0, p)),
                  pl.BlockSpec((1, s, LANES), lambda b_, p, i: (b_, 0, p)),
                  pl.BlockSpec((1, LANES), lambda b_, p, i: (0, 0))],
        out_specs=pl.BlockSpec((1, tq, LANES), lambda b_, p, i: (b_, i, p)),
        compiler_params=_cparams(("arbitrary", "arbitrary", "arbitrary")),
        name="diff_attention_prompt",
    )(lam_p, qa, dkb, dvb, sub_g2)


def _softmax_rows(s):
    m = jnp.max(s, axis=1, keepdims=True)
    e = jnp.exp(s - m)
    return e / jnp.sum(e, axis=1, keepdims=True)


def _top_select(score, blk, n_sel):
    nb = score.shape[1]
    cnt = jnp.zeros(score.shape, F32)
    for i in range(nb):
        col = score[:, i:i + 1]
        gt = jnp.where(col > score, 1.0, 0.0)
        eq = jnp.where(col == score, 1.0, 0.0)
        cnt = cnt + gt + jnp.where(blk > i, eq, 0.0)
    return cnt < n_sel


def _compress(kc, pe_sum, w, nb):
    mean = (jnp.sum(kc.reshape(nb, CMP_BLOCK, HEAD_DIM), axis=1) + pe_sum) * (1.0 / CMP_BLOCK)
    return _mm(mean, w, HI)


def _nsa_prompt_kernel(q_ref, gate_ref, kcmp_ref, vcmp_ref, kslc_ref, vslc_ref, kwin_ref,
                       vwin_ref, pe_ref, cw_ref, ckg_ref, o_ref,
                       kaug, vs, kw, vw, ck, cv, *, tq, tk, seq):
    qi = pl.program_id(1)
    nb = seq // CMP_BLOCK
    n_sel = min(N_SELECT, nb)
    wk = WINDOW + tq

    @pl.when(qi == 0)
    def _():
        rowblk = lax.broadcasted_iota(I32, (seq, HEAD_DIM), 0) // CMP_BLOCK
        colblk = lax.broadcasted_iota(I32, (seq, HEAD_DIM), 1)
        onehot = jnp.where(rowblk == colblk, 1.0, 0.0).astype(BF16)
        pe_k = jnp.sum(pe_ref[0], axis=0, keepdims=True)
        pe_v = jnp.sum(pe_ref[1], axis=0, keepdims=True)
        for g in range(B_KV_HEADS):
            lo, hi = HEAD_DIM * g, HEAD_DIM * (g + 1)
            kaug[g] = jnp.concatenate([kslc_ref[0][:, lo:hi], onehot], axis=1)
            vs[g] = vslc_ref[0][:, lo:hi]
            kw[g] = kwin_ref[0][:, lo:hi]
            vw[g] = vwin_ref[0][:, lo:hi]
            c = _compress(kcmp_ref[0][:, lo:hi], pe_k, cw_ref[0], nb)
            ms = jnp.mean(c * c, axis=-1, keepdims=True)
            ck[g] = c * lax.rsqrt(ms + EPS) * ckg_ref[...]
            cv[g] = _compress(vcmp_ref[0][:, lo:hi], pe_v, cw_ref[1], nb)

    q = q_ref[0]
    gates = gate_ref[0]
    q0 = qi * tq
    qpos = q0 + lax.broadcasted_iota(I32, (tq, 1), 0)
    blk = lax.broadcasted_iota(I32, (1, nb), 1)
    complete = (blk + 1) * CMP_BLOCK - 1 <= qpos
    cur = qpos // CMP_BLOCK
    forced = blk * (blk - cur) == 0
    qpos4 = jnp.concatenate([qpos] * B_GROUP, axis=0)
    n_full = q0 // tk
    wstart = pl.multiple_of(jnp.maximum(q0 - WINDOW, 0), tq)
    kpos_w = wstart + lax.broadcasted_iota(I32, (1, wk), 1)
    dlt = qpos4 - kpos_w
    heads_out = []

    for g in range(B_KV_HEADS):
        qh = [q[:, HEAD_DIM * (B_GROUP * g + r):HEAD_DIM * (B_GROUP * g + r + 1)]
              for r in range(B_GROUP)]
        ckg = ck[g]
        cvg = cv[g]
        o_cmp = []
        imp = jnp.zeros((tq, nb), F32)
        for r in range(B_GROUP):
            s = jnp.where(complete, _nt(qh[r], ckg, HI), NEG_BIG)
            e = jnp.where(complete, jnp.exp(s - jnp.max(s, axis=1, keepdims=True)), 0.0)
            p = e / jnp.maximum(jnp.sum(e, axis=1, keepdims=True), 1e-30)
            o_cmp.append(_mm(p, cvg, HI))
            imp = imp + p
        score = jnp.where(forced, FORCE_SCORE, jnp.where(complete, imp, -1.0))
        sel = _top_select(score, blk, n_sel)
        bias = jnp.where(blk <= cur, jnp.where(sel, 0.0, NEG_BIG), NEG_BIG).astype(BF16)
        if nb < HEAD_DIM:
            bias = jnp.concatenate([bias, jnp.zeros((tq, HEAD_DIM - nb), BF16)], axis=1)

        q4 = jnp.concatenate([qh[r].astype(BF16) for r in range(B_GROUP)], axis=0)
        qaug = jnp.concatenate([q4, jnp.concatenate([bias] * B_GROUP, axis=0)], axis=1)

        def chunk(j, carry, masked, g=g, qaug=qaug):
            m, l, acc = carry
            start = pl.multiple_of(j * tk, tk)
            s = _nt(qaug, kaug[g, pl.ds(start, tk), :])
            if masked:
                kpos = start + lax.broadcasted_iota(I32, (1, tk), 1)
                s = jnp.where(kpos <= qpos4, s, NEG_BIG)
            m_new = jnp.maximum(m, jnp.max(s, axis=1, keepdims=True))
            p = jnp.exp(s - m_new)
            alpha = jnp.exp(m - m_new)
            l_new = alpha * l + jnp.sum(p, axis=1, keepdims=True)
            acc_new = alpha * acc + _mm(p.astype(BF16), vs[g, pl.ds(start, tk), :])
            return m_new, l_new, acc_new

        init = (jnp.full((B_GROUP * tq, 1), NEG_BIG, F32), jnp.zeros((B_GROUP * tq, 1), F32),
                jnp.zeros((B_GROUP * tq, HEAD_DIM), F32))
        carry = lax.fori_loop(0, n_full, lambda j, c: chunk(j, c, False), init)
        _, l, acc = chunk(n_full, carry, True)
        o_slc = acc / l

        sw = _nt(q4, kw[g, pl.ds(wstart, wk), :])
        sw = jnp.where(dlt >= 0, jnp.where(dlt < WINDOW, sw, NEG_BIG), NEG_BIG)
        o_win = _mm(_softmax_rows(sw).astype(BF16), vw[g, pl.ds(wstart, wk), :])

        for r in range(B_GROUP):
            h = B_GROUP * g + r
            rows = slice(r * tq, (r + 1) * tq)
            heads_out.append(gates[:, h:h + 1] * o_cmp[r]
                             + gates[:, 8 + h:9 + h] * o_slc[rows]
                             + gates[:, 16 + h:17 + h] * o_win[rows])
    o_ref[0] = jnp.concatenate(heads_out, axis=1).astype(BF16)


def _nsa_prompt(qn, gates, nk, nv, kb, vb, pe, cw, ckg, tq, tk):
    b, s, _ = qn.shape
    nb = s // CMP_BLOCK
    kern = functools.partial(_nsa_prompt_kernel, tq=tq, tk=tk, seq=s)
    full = lambda lane_blk: pl.BlockSpec((1, s, LANES), lambda b_, i: (b_, 0, lane_blk))
    const = lambda shape: pl.BlockSpec(shape, lambda b_, i: (0,) * len(shape))
    return pl.pallas_call(
        kern,
        out_shape=jax.ShapeDtypeStruct((b, s, 512), BF16),
        grid=(b, s // tq),
        in_specs=[pl.BlockSpec((1, tq, 512), lambda b_, i: (b_, i, 0)),
                  pl.BlockSpec((1, tq, LANES), lambda b_, i: (b_, i, 0)),
                  full(0), full(0), full(1), full(1), full(2), full(2),
                  const((2, CMP_BLOCK, HEAD_DIM)), const((2, HEAD_DIM, HEAD_DIM)),
                  const((1, HEAD_DIM))],
        out_specs=pl.BlockSpec((1, tq, 512), lambda b_, i: (b_, i, 0)),
        scratch_shapes=[pltpu.VMEM((2, s, LANES), BF16), pltpu.VMEM((2, s, HEAD_DIM), BF16),
                        pltpu.VMEM((2, s, HEAD_DIM), BF16), pltpu.VMEM((2, s, HEAD_DIM), BF16),
                        pltpu.VMEM((2, nb, HEAD_DIM), F32), pltpu.VMEM((2, nb, HEAD_DIM), F32)],
        compiler_params=_cparams(("arbitrary", "arbitrary")),
        name="nsa_prompt",
    )(qn, gates, nk, nv, kb, vb, kb, vb, pe, cw, ckg)


def _outproj_kernel(oa_ref, ob_ref, x_ref, gt_ref, sh_ref, sc_ref, g_ref, w_ref, rw_ref, rb_ref,
                    x1_ref, h2_ref, ids_ref, gates_ref):
    prec = HI if w_ref.dtype == F32 else None
    y = _mm(oa_ref[0], w_ref[0:512, :], prec) + _mm(ob_ref[0], w_ref[512:1024, :], prec)
    x1 = x_ref[0] + gt_ref[0] * y
    x1_ref[0] = x1
    ms = jnp.mean(x1 * x1, axis=-1, keepdims=True)
    h2 = x1 * lax.rsqrt(ms + EPS) * g_ref[...]
    h2 = h2 * (1.0 + sc_ref[0]) + sh_ref[0]
    h2_ref[0] = h2
    logits = _nt(rw_ref[...], h2, HI) + rb_ref[...]
    eidx = lax.broadcasted_iota(I32, logits.shape, 0)
    work = logits
    vals, ids = [], []
    for _ in range(TOP_K):
        m = jnp.max(work, axis=0, keepdims=True)
        idx = jnp.min(jnp.where(work == m, eidx, N_EXPERTS), axis=0, keepdims=True)
        vals.append(m)
        ids.append(idx)
        work = jnp.where(eidx == idx, -3e38, work)
    es = [jnp.exp(v - vals[0]) for v in vals]
    tot = es[0] + es[1] + es[2] + es[3]
    ids_ref[0] = jnp.concatenate(ids + ids, axis=0)
    gates_ref[0] = jnp.concatenate([e / tot for e in es] * 2, axis=0)


def _out_projection(oa, ob, x, gt, sh, sc, prm, ts, precise=False):
    bx, sx, d = x.shape
    r = gt.shape[1]
    rb = 1 if r == 1 else ts
    mod_spec = pl.BlockSpec((1, rb, d), (lambda b, s: (b, 0, 0)) if r == 1 else (lambda b, s: (b, s, 0)))
    const = lambda shape: pl.BlockSpec(shape, lambda b, s: (0,) * len(shape))
    tokspec = lambda w: pl.BlockSpec((1, ts, w), lambda b, s: (b, s, 0))
    nt = sx // ts
    return pl.pallas_call(
        _outproj_kernel,
        out_shape=[jax.ShapeDtypeStruct((bx, sx, d), F32), jax.ShapeDtypeStruct((bx, sx, d), F32),
                   jax.ShapeDtypeStruct((bx * nt, 8, ts), I32),
                   jax.ShapeDtypeStruct((bx * nt, 8, ts), F32)],
        grid=(bx, nt),
        in_specs=[tokspec(512), tokspec(512), tokspec(d), mod_spec, mod_spec, mod_spec,
                  const((1, d)), const((d, d)), const((N_EXPERTS, d)), const((N_EXPERTS, 1))],
        out_specs=[tokspec(d), tokspec(d),
                   pl.BlockSpec((1, 8, ts), lambda b, s: (b * nt + s, 0, 0)),
                   pl.BlockSpec((1, 8, ts), lambda b, s: (b * nt + s, 0, 0))],
        compiler_params=_cparams(("arbitrary", "arbitrary")),
        name="out_projection",
    )(oa, ob, x, gt, sh, sc, prm["ffn_g"], prm["w_out_f32" if precise else "w_out"],
      prm["router_wt"], prm["router_b"])


def _route_kernel(ids_ref, pos_ref, cnt_ref, carry, *, tt):
    @pl.when(pl.program_id(0) == 0)
    def _():
        carry[...] = jnp.zeros_like(carry)

    ids = ids_ref[0]
    e_iota = lax.broadcasted_iota(I32, (N_EXPERTS, tt), 0)
    hits = [ids[k:k + 1, :] == e_iota for k in range(TOP_K)]
    oh = jnp.zeros((N_EXPERTS, tt), F32)
    for k in range(TOP_K):
        oh = oh + jnp.where(hits[k], 1.0, 0.0)
    r = lax.broadcasted_iota(I32, (tt, tt), 0)
    c = lax.broadcasted_iota(I32, (tt, tt), 1)
    upper = jnp.where(r < c, 1.0, 0.0).astype(BF16)
    before = _mm(oh.astype(BF16), upper) + carry[:, 0:1]
    rows = [jnp.sum(jnp.where(hits[k], before, 0.0), axis=0, keepdims=True) for k in range(TOP_K)]
    pos_ref[0] = jnp.concatenate(rows + rows, axis=0).astype(I32)
    carry[...] = carry[...] + jnp.sum(oh, axis=1, keepdims=True)
    cnt_ref[...] = carry[...]


def _route_positions(ids3, tt):
    nt = ids3.shape[0]
    return pl.pallas_call(
        functools.partial(_route_kernel, tt=tt),
        out_shape=[jax.ShapeDtypeStruct((nt, 8, tt), I32),
                   jax.ShapeDtypeStruct((N_EXPERTS, LANES), F32)],
        grid=(nt,),
        in_specs=[pl.BlockSpec((1, 8, tt), lambda i: (i, 0, 0))],
        out_specs=[pl.BlockSpec((1, 8, tt), lambda i: (i, 0, 0)),
                   pl.BlockSpec((N_EXPERTS, LANES), lambda i: (0, 0))],
        scratch_shapes=[pltpu.VMEM((N_EXPERTS, LANES), F32)],
        compiler_params=_cparams(("arbitrary",)),
        name="moe_route_positions",
    )(ids3)


def _scatter_kernel(dest_hbm, h_ref, xs_in_hbm, xs_hbm, idx, isem, sem, *, tt):
    del xs_in_hbm
    i = pl.program_id(0)
    cp = pltpu.make_async_copy(dest_hbm.at[i], idx, isem)
    cp.start()
    cp.wait()

    def issue(t, _):
        for k in range(TOP_K):
            pltpu.make_async_copy(h_ref.at[pl.ds(t, 1)], xs_hbm.at[pl.ds(idx[k * tt + t], 1)],
                                  sem).start()
        return 0

    lax.fori_loop(0, tt, issue, 0)
    for k in range(TOP_K):
        pltpu.make_async_copy(h_ref, xs_hbm.at[pl.ds(0, tt)], sem).wait()


def _scatter_rows(dest2, h_all, n_rows, tt):
    nt = dest2.shape[0]
    d = h_all.shape[1]
    return pl.pallas_call(
        functools.partial(_scatter_kernel, tt=tt),
        out_shape=jax.ShapeDtypeStruct((n_rows, d), F32),
        grid=(nt,),
        in_specs=[pl.BlockSpec(memory_space=pl.ANY),
                  pl.BlockSpec((tt, d), lambda i: (i, 0)),
                  pl.BlockSpec(memory_space=pl.ANY)],
        out_specs=pl.BlockSpec(memory_space=pl.ANY),
        scratch_shapes=[pltpu.SMEM((TOP_K * tt,), I32), pltpu.SemaphoreType.DMA,
                        pltpu.SemaphoreType.DMA],
        input_output_aliases={2: 0},
        compiler_params=_cparams(("arbitrary",)),
        name="moe_scatter_rows",
    )(dest2, h_all, jnp.zeros((n_rows, d), F32))


def _expert_kernel(be_ref, nu_ref, x_ref, wgu_ref, bgu_ref, wdn_ref, bdn_ref, y_ref, *, d_ff):
    @pl.when(pl.program_id(0) < nu_ref[0])
    def _():
        gu = _mm(x_ref[...].astype(BF16), wgu_ref[0]) + bgu_ref[0]
        g = jnp.minimum(gu[:, :d_ff], SWIGLU_LIMIT)
        u = jnp.clip(gu[:, d_ff:], -SWIGLU_LIMIT, SWIGLU_LIMIT)
        a = g * (1.0 / (1.0 + jnp.exp(-SWIGLU_ALPHA * g))) * (u + 1.0)
        y_ref[...] = _mm(a.astype(BF16), wdn_ref[0]) + bdn_ref[0]

    @pl.when(pl.program_id(0) >= nu_ref[0])
    def _():
        y_ref[...] = jnp.zeros_like(y_ref)


def _expert_matmul(blk_e, n_used, xs, wgu, bgu, wdn, bdn, n_rows, tm):
    d = xs.shape[1]
    d_ff = wdn.shape[1]
    nblk = n_rows // tm
    row = lambda i, be, nu: (jnp.minimum(i, nu[0] - 1), 0)
    wsel = lambda i, be, nu: (be[jnp.minimum(i, nu[0] - 1)], 0, 0)
    return pl.pallas_call(
        functools.partial(_expert_kernel, d_ff=d_ff),
        out_shape=jax.ShapeDtypeStruct((n_rows, d), F32),
        grid_spec=pltpu.PrefetchScalarGridSpec(
            num_scalar_prefetch=2,
            grid=(nblk,),
            in_specs=[pl.BlockSpec((tm, d), row),
                      pl.BlockSpec((1, d, 2 * d_ff), wsel),
                      pl.BlockSpec((1, 1, 2 * d_ff), wsel),
                      pl.BlockSpec((1, d_ff, d), wsel),
                      pl.BlockSpec((1, 1, d), wsel)],
            out_specs=pl.BlockSpec((tm, d), lambda i, be, nu: (i, 0))),
        compiler_params=_cparams(("arbitrary",)),
        name="moe_expert_matmul",
    )(blk_e, n_used, xs, wgu, bgu, wdn, bdn)


def _combine_kernel(dest_hbm, y_hbm, x1_ref, gt_ref, gate_ref, o_ref, idx, buf, isem, sem, *, tt, tile0):
    b = pl.program_id(0)
    s = pl.program_id(1)
    i = tile0 + b * pl.num_programs(1) + s
    cp = pltpu.make_async_copy(dest_hbm.at[i], idx, isem)
    cp.start()
    cp.wait()

    def issue(t, _):
        for k in range(TOP_K):
            pltpu.make_async_copy(y_hbm.at[pl.ds(idx[k * tt + t], 1)], buf.at[k, pl.ds(t, 1)],
                                  sem).start()
        return 0

    lax.fori_loop(0, tt, issue, 0)
    for k in range(TOP_K):
        pltpu.make_async_copy(y_hbm.at[pl.ds(0, tt)], buf.at[k], sem).wait()
    gate = gate_ref[0]
    moe = gate[:, 0:1] * buf[0]
    for k in range(1, TOP_K):
        moe = moe + gate[:, k:k + 1] * buf[k]
    o_ref[0] = x1_ref[0] + gt_ref[0] * moe


def _combine(dest2, y_rows, x1, gt, gate_t, tile0, tt):
    bx, sx, d = x1.shape
    r = gt.shape[1]
    rb = 1 if r == 1 else tt
    nt = sx // tt
    mod_spec = pl.BlockSpec((1, rb, d), (lambda b, s: (b, 0, 0)) if r == 1 else (lambda b, s: (b, s, 0)))
    return pl.pallas_call(
        functools.partial(_combine_kernel, tt=tt, tile0=tile0),
        out_shape=jax.ShapeDtypeStruct((bx, sx, d), F32),
        grid=(bx, nt),
        in_specs=[pl.BlockSpec(memory_space=pl.ANY), pl.BlockSpec(memory_space=pl.ANY),
                  pl.BlockSpec((1, tt, d), lambda b, s: (b, s, 0)), mod_spec,
                  pl.BlockSpec((1, tt, TOP_K), lambda b, s: (b * nt + s, 0, 0))],
        out_specs=pl.BlockSpec((1, tt, d), lambda b, s: (b, s, 0)),
        scratch_shapes=[pltpu.SMEM((TOP_K * tt,), I32), pltpu.VMEM((TOP_K, tt, d), F32),
                        pltpu.SemaphoreType.DMA, pltpu.SemaphoreType.DMA],
        compiler_params=_cparams(("arbitrary", "arbitrary")),
        name="moe_combine",
    )(dest2, y_rows, x1, gt, gate_t)


def _diff_decode_kernel(pt_ref, lam_ref, q_ref, kn_ref, vn_ref, sg_ref, *rest, pps, lam_init):
    k_refs = rest[:pps]
    v_refs = rest[pps:2 * pps]
    o_ref = rest[2 * pps]
    m_s, l_s, a0_s, a1_s = rest[2 * pps + 1:]
    j = pl.program_id(1)
    q = q_ref[0]
    n_hc = 2 * A_HEADS

    @pl.when(j == 0)
    def _():
        m_s[...] = jnp.full_like(m_s, NEG_BIG)
        l_s[...] = jnp.zeros_like(l_s)
        a0_s[...] = jnp.zeros_like(a0_s)
        a1_s[...] = jnp.zeros_like(a1_s)

    for p in range(pps):
        s = jnp.sum((k_refs[p][0] * q).reshape(n_hc, A_DQK, PAGE), axis=1)
        m_old = m_s[...]
        m_new = jnp.maximum(m_old, s)
        pr = jnp.exp(s - m_new)
        alpha = jnp.exp(m_old - m_new)
        m_s[...] = m_new
        l_s[...] = alpha * l_s[...] + pr
        for h in range(A_HEADS):
            rows = pl.ds(HEAD_DIM * h, HEAD_DIM)
            vt = v_refs[p][0, rows, :]
            for c, acc in ((0, a0_s), (1, a1_s)):
                i = 2 * h + c
                acc[rows, :] = alpha[i:i + 1] * acc[rows, :] + pr[i:i + 1] * vt

    @pl.when(j == pl.num_programs(1) - 1)
    def _():
        lam = _diff_lambda(lam_ref, lam_init)
        m = m_s[...]
        s_new = jnp.sum((kn_ref[0] * q).reshape(n_hc, A_DQK, 1), axis=1)
        big = jnp.maximum(jnp.max(m, axis=1, keepdims=True), s_new)
        w = jnp.exp(m - big)
        wn = jnp.exp(s_new - big)
        inv = 1.0 / (jnp.sum(l_s[...] * w, axis=1, keepdims=True) + wn)
        sg = sg_ref[...]
        for h in range(A_HEADS):
            rows = pl.ds(HEAD_DIM * h, HEAD_DIM)
            vn = vn_ref[0, rows, :]
            oc = []
            for c, acc in ((0, a0_s), (1, a1_s)):
                i = 2 * h + c
                num = jnp.sum(acc[rows, :] * w[i:i + 1], axis=1, keepdims=True) + wn[i:i + 1] * vn
                oc.append(num * inv[i:i + 1])
            o = oc[0] - lam * oc[1]
            ms = jnp.mean(o * o, axis=0, keepdims=True)
            o_ref[0, rows, :] = o * lax.rsqrt(ms + EPS) * sg * (1.0 - lam_init)


def _diff_attention_sample(page_table, lam_p, q_col, k_new_col, v_new_col, sub_g_col, cache_kt, cache_vt,
                           lam_init, pps):
    nb, n_pages = page_table.shape
    steps = n_pages // pps
    const = lambda shape: pl.BlockSpec(shape, lambda b, j, pt: (0,) * len(shape))
    per_b = pl.BlockSpec((1, 512, 1), lambda b, j, pt: (b, 0, 0))

    def page_spec(p):
        return pl.BlockSpec((1, 512, PAGE), lambda b, j, pt: (pt[b, j * pps + p], 0, 0))

    kern = functools.partial(_diff_decode_kernel, pps=pps, lam_init=lam_init)
    out = pl.pallas_call(
        kern,
        out_shape=jax.ShapeDtypeStruct((nb, 512, 1), F32),
        grid_spec=pltpu.PrefetchScalarGridSpec(
            num_scalar_prefetch=1,
            grid=(nb, steps),
            in_specs=[const((4, A_DQK)), per_b, per_b, per_b, const((HEAD_DIM, 1))]
                     + [page_spec(p) for p in range(pps)] * 2,
            out_specs=pl.BlockSpec((1, 512, 1), lambda b, j, pt: (b, 0, 0)),
            scratch_shapes=[pltpu.VMEM((2 * A_HEADS, PAGE), F32), pltpu.VMEM((2 * A_HEADS, PAGE), F32),
                            pltpu.VMEM((512, PAGE), F32), pltpu.VMEM((512, PAGE), F32)]),
        compiler_params=_cparams(("arbitrary", "arbitrary")),
        name="diff_attention_sample",
    )(page_table, lam_p, q_col, k_new_col, v_new_col, sub_g_col,
      *([cache_kt] * pps), *([cache_vt] * pps))
    return out[:, :, 0]


def _nsa_decode_cmp_kernel(pt_ref, q_ref, kn_ref, vn_ref, pet_ref, cwt_ref, ckg_ref, *rest,
                           pps, n_chunks, past_len):
    k_refs = rest[:pps]
    v_refs = rest[pps:2 * pps]
    ocmp_ref, sel_ref = rest[2 * pps:2 * pps + 2]
    ksum, vsum = rest[2 * pps + 2:]
    j = pl.program_id(1)
    nb_past = past_len // CMP_BLOCK
    per_page = PAGE // CMP_BLOCK
    nbp = n_chunks * LANES
    lane = lax.broadcasted_iota(I32, (1, LANES), 1)

    @pl.when(j == 0)
    def _():
        ksum[...] = jnp.zeros_like(ksum)
        vsum[...] = jnp.zeros_like(vsum)

    base = j * (pps * per_page)
    chunk = base // LANES
    lane0 = base % LANES
    for refs, acc in ((k_refs, ksum), (v_refs, vsum)):
        cur = acc[chunk]
        for p in range(pps):
            pg = refs[p][0]
            for t in range(per_page):
                in_blk = lane // CMP_BLOCK == t
                col = jnp.sum(jnp.where(in_blk, pg, 0.0), axis=1, keepdims=True)
                cur = jnp.where(lane == lane0 + per_page * p + t, col, cur)
        acc[chunk] = cur

    @pl.when(j == pl.num_programs(1) - 1)
    def _():
        q = q_ref[0]
        blk = lax.broadcasted_iota(I32, (1, nbp), 1)
        qpos = past_len
        complete = (blk + 1) * CMP_BLOCK - 1 <= qpos
        cur_blk = qpos // CMP_BLOCK
        forced = blk * (blk - cur_blk) == 0
        is_new = blk == nb_past
        kall = jnp.where(is_new, kn_ref[0], jnp.concatenate([ksum[c] for c in range(n_chunks)], axis=1))
        vall = jnp.where(is_new, vn_ref[0], jnp.concatenate([vsum[c] for c in range(n_chunks)], axis=1))
        pe_k = jnp.sum(pet_ref[0], axis=1, keepdims=True)
        pe_v = jnp.sum(pet_ref[1], axis=1, keepdims=True)
        outs = []
        sel_rows = []
        for g in range(B_KV_HEADS):
            rows = slice(HEAD_DIM * g, HEAD_DIM * (g + 1))
            c = _mm(cwt_ref[0], (kall[rows] + pe_k) * (1.0 / CMP_BLOCK), HI)
            ck = c * lax.rsqrt(jnp.mean(c * c, axis=0, keepdims=True) + EPS) * ckg_ref[...]
            cv = _mm(cwt_ref[1], (vall[rows] + pe_v) * (1.0 / CMP_BLOCK), HI)
            qg = jnp.concatenate(
                [q[:, HEAD_DIM * (B_GROUP * g + r):HEAD_DIM * (B_GROUP * g + r + 1)]
                 for r in range(B_GROUP)] * 2, axis=0)
            s = jnp.where(complete, _mm(qg, ck, HI), NEG_BIG)
            e = jnp.where(complete, jnp.exp(s - jnp.max(s, axis=1, keepdims=True)), 0.0)
            p = e / jnp.maximum(jnp.sum(e, axis=1, keepdims=True), 1e-30)
            outs.append(_nt(p, cv, HI)[0:B_GROUP])
            imp = jnp.sum(p[0:B_GROUP], axis=0, keepdims=True)
            work = jnp.where(forced, FORCE_SCORE, jnp.where(complete, imp, -1.0))
            work = jnp.where(blk <= cur_blk, work, -2.0)
            picked = jnp.zeros((1, LANES), I32)
            for t in range(N_SELECT):
                mx = jnp.max(work, axis=1, keepdims=True)
                first = jnp.min(jnp.where(work == mx, blk, nbp), axis=1, keepdims=True)
                picked = jnp.where(lane == t, first, picked)
                work = jnp.where(blk == first, -3.0, work)
            sel_rows.append(picked)
        ocmp_ref[0] = jnp.concatenate(outs, axis=0)
        sel_ref[0] = jnp.concatenate(sel_rows * 4, axis=0)


def _nsa_decode_cmp(page_table, q, k_new_col, v_new_col, pet, cwt, ckg_col, cache_kt, cache_vt,
                    past_len, pps):
    nb, n_pages = page_table.shape
    steps = n_pages // pps
    n_blocks = past_len // CMP_BLOCK + 1
    n_chunks = -(-n_blocks // LANES)
    assert LANES % (pps * (PAGE // CMP_BLOCK)) == 0 and n_blocks >= N_SELECT
    const = lambda shape: pl.BlockSpec(shape, lambda b, j, pt: (0,) * len(shape))

    def page_spec(p):
        return pl.BlockSpec((1, LANES, PAGE), lambda b, j, pt: (pt[b, j * pps + p], 0, 0))

    kern = functools.partial(_nsa_decode_cmp_kernel, pps=pps, n_chunks=n_chunks, past_len=past_len)
    return pl.pallas_call(
        kern,
        out_shape=[jax.ShapeDtypeStruct((nb, 8, HEAD_DIM), F32),
                   jax.ShapeDtypeStruct((nb, 8, LANES), I32)],
        grid_spec=pltpu.PrefetchScalarGridSpec(
            num_scalar_prefetch=1,
            grid=(nb, steps),
            in_specs=[pl.BlockSpec((1, 1, 512), lambda b, j, pt: (b, 0, 0)),
                      pl.BlockSpec((1, LANES, 1), lambda b, j, pt: (b, 0, 0)),
                      pl.BlockSpec((1, LANES, 1), lambda b, j, pt: (b, 0, 0)),
                      const((2, HEAD_DIM, CMP_BLOCK)), const((2, HEAD_DIM, HEAD_DIM)),
                      const((HEAD_DIM, 1))]
                     + [page_spec(p) for p in range(pps)] * 2,
            out_specs=[pl.BlockSpec((1, 8, HEAD_DIM), lambda b, j, pt: (b, 0, 0)),
                       pl.BlockSpec((1, 8, LANES), lambda b, j, pt: (b, 0, 0))],
            scratch_shapes=[pltpu.VMEM((n_chunks, LANES, LANES), F32),
                            pltpu.VMEM((n_chunks, LANES, LANES), F32)]),
        compiler_params=_cparams(("arbitrary", "arbitrary")),
        name="nsa_sample_compressed",
    )(page_table, q, k_new_col, v_new_col, pet, cwt, ckg_col,
      *([cache_kt] * pps), *([cache_vt] * pps))


def _pick_head(x, g):
    return jnp.where(g == 0, x[:, 0:HEAD_DIM], x[:, HEAD_DIM:LANES])


def _nsa_decode_mix_kernel(pt_ref, sel_ref, q_ref, gate_ref, ocmp_ref, kn_ref, vn_ref, wn_ref,
                           kwin_ref, vwin_ref, *rest, past_len, n_sel):
    k_refs = rest[:n_sel]
    v_refs = rest[n_sel:2 * n_sel]
    o_ref = rest[2 * n_sel]
    b = pl.program_id(0)
    g = pl.program_id(1)
    q = q_ref[0]
    gates = gate_ref[0]
    ocmp = ocmp_ref[0, 0]
    nb_past = past_len // CMP_BLOCK
    kwin = kwin_ref[0]
    vwin = vwin_ref[0]
    w_buf = kwin.shape[1]
    wrow = lax.broadcasted_iota(I32, (1, w_buf), 1)
    wlo = max(w_buf - WINDOW + 1, w_buf - past_len, 0)
    wmask = wrow >= wlo
    qg = jnp.concatenate([q[:, HEAD_DIM * r:HEAD_DIM * (r + 1)] for r in range(B_GROUP)] * 2,
                         axis=0)
    kn = _pick_head(kn_ref[0][:, LANES:2 * LANES], g)
    vn = _pick_head(vn_ref[0][:, LANES:2 * LANES], g)
    s_new = jnp.sum(qg * kn, axis=1, keepdims=True)
    ss = []
    halves = []
    for t in range(n_sel):
        blk = sel_ref[b, g * n_sel + t]
        valid = blk < nb_past
        halves.append(blk % (PAGE // CMP_BLOCK))
        s = _mm(qg, _pick_head(k_refs[t][0], halves[t]), HI)
        ss.append(jnp.where(valid, s, NEG_BIG))
    m = s_new
    for s in ss:
        m = jnp.maximum(m, jnp.max(s, axis=1, keepdims=True))
    l = jnp.exp(s_new - m)
    acc = l * vn
    for t in range(n_sel):
        p = jnp.exp(ss[t] - m)
        l = l + jnp.sum(p, axis=1, keepdims=True)
        acc = acc + _nt(p, _pick_head(v_refs[t][0], halves[t]), HI)
    o_slc = acc / l
    kwn = _pick_head(wn_ref[0][:, 0:LANES], g)
    vwn = _pick_head(wn_ref[0][:, LANES:2 * LANES], g)
    sw_new = jnp.sum(qg * kwn, axis=1, keepdims=True)
    sw = jnp.where(wmask, _mm(qg, kwin, HI), NEG_BIG)
    mw = jnp.maximum(sw_new, jnp.max(sw, axis=1, keepdims=True))
    pw = jnp.exp(sw - mw)
    pn = jnp.exp(sw_new - mw)
    lw = pn + jnp.sum(pw, axis=1, keepdims=True)
    o_win = (pn * vwn + _nt(pw, vwin, HI)) / lw
    heads = []
    for r in range(B_GROUP):
        def gate(c, r=r):
            return jnp.where(g == 0, gates[:, 8 * c + r:8 * c + r + 1],
                             gates[:, 8 * c + B_GROUP + r:8 * c + B_GROUP + r + 1])
        heads.append(gate(0) * ocmp[r:r + 1] + gate(1) * o_slc[r:r + 1] + gate(2) * o_win[r:r + 1])
    o_ref[0] = jnp.broadcast_to(jnp.concatenate(heads, axis=1), (8, 256))


def _nsa_decode_mix(page_table, sel2, q, gates, ocmp, nk_new, nv_new, win_new, state_t,
                    cache_kt, cache_vt, past_len, n_sel):
    nb = page_table.shape[0]
    w_buf = state_t.shape[2]
    nb_past = past_len // CMP_BLOCK
    per_page = PAGE // CMP_BLOCK
    per_b = lambda r, w: pl.BlockSpec((1, r, w), lambda b, g, pt, sl: (b, 0, 0))

    def blk_spec(t):
        def imap(b, g, pt, sl):
            blk = jnp.minimum(sl[b, g * n_sel + t], nb_past - 1)
            return (pt[b, blk // per_page], B_KV_HEADS + g, 0)
        return pl.BlockSpec((1, HEAD_DIM, PAGE), imap)

    kern = functools.partial(_nsa_decode_mix_kernel, past_len=past_len, n_sel=n_sel)
    out = pl.pallas_call(
        kern,
        out_shape=jax.ShapeDtypeStruct((nb, 8, 512), F32),
        grid_spec=pltpu.PrefetchScalarGridSpec(
            num_scalar_prefetch=2,
            grid=(nb, B_KV_HEADS),
            in_specs=[pl.BlockSpec((1, 1, 256), lambda b, g, pt, sl: (b, 0, g)),
                      per_b(1, LANES),
                      pl.BlockSpec((1, 1, B_GROUP, HEAD_DIM), lambda b, g, pt, sl: (b, g, 0, 0)),
                      per_b(1, 256), per_b(1, 256), per_b(1, 256),
                      pl.BlockSpec((1, HEAD_DIM, w_buf), lambda b, g, pt, sl: (b, g, 0)),
                      pl.BlockSpec((1, HEAD_DIM, w_buf), lambda b, g, pt, sl: (b, B_KV_HEADS + g, 0))]
                     + [blk_spec(t) for t in range(n_sel)] * 2,
            out_specs=pl.BlockSpec((1, 8, 256), lambda b, g, pt, sl: (b, 0, g))),
        compiler_params=_cparams(("arbitrary", "arbitrary")),
        name="nsa_sample_mix",
    )(page_table, sel2, q, gates, ocmp, nk_new, nv_new, win_new, state_t, state_t,
      *([cache_kt] * n_sel), *([cache_vt] * n_sel))
    return out[:, 0, :]


def _moe(h2_list, ids_list, gates_list, x1_list, gt_list, prm, tt, tm):
    d = h2_list[0].shape[-1]
    n_group = [h.shape[0] * h.shape[1] for h in h2_list]
    n_tok = sum(n_group)
    nt = -(-n_tok // tt)
    ntp = nt * tt
    ids = jnp.concatenate(ids_list, axis=1)[:TOP_K]
    ids = jnp.pad(ids, ((0, 8 - TOP_K), (0, ntp - n_tok)), constant_values=N_EXPERTS)
    ids3 = ids.reshape(8, nt, tt).transpose(1, 0, 2)
    pos3, cnt = _route_positions(ids3, tt)
    counts = cnt[:, 0].astype(I32)
    padded = (counts + tm - 1) // tm * tm
    ends_p = jnp.cumsum(padded)
    starts_p = ends_p - padded
    na = n_tok * TOP_K
    nblk = -(-(na + N_EXPERTS * (tm - 1)) // tm)
    n_rows = nblk * tm
    ids4 = ids3[:, :TOP_K, :]
    valid = ids4 < N_EXPERTS
    e_ar = jnp.arange(N_EXPERTS, dtype=I32)
    start_of = jnp.sum(jnp.where(ids4[..., None] == e_ar, starts_p, 0), axis=-1)
    dest = start_of + pos3[:, :TOP_K, :]
    tok_id = (jnp.arange(nt, dtype=I32)[:, None, None] * tt + jnp.arange(tt, dtype=I32)[None, None, :])
    pad_rank = (tok_id - n_tok) * TOP_K + jnp.arange(TOP_K, dtype=I32)[None, :, None]
    dest_scatter = jnp.where(valid, dest, n_rows + pad_rank).reshape(nt, TOP_K * tt)
    dest_gather = jnp.where(valid, dest, 0).reshape(nt, TOP_K * tt)
    n_trash = (ntp - n_tok) * TOP_K
    blk_start = jnp.arange(nblk, dtype=I32) * tm
    blk_e = jnp.minimum(jnp.sum(jnp.where(ends_p[None, :] <= blk_start[:, None], 1, 0), axis=1),
                        N_EXPERTS - 1).astype(I32)
    n_used = (ends_p[-1:] // tm).astype(I32)

    h_all = jnp.concatenate([h.reshape(-1, d) for h in h2_list], axis=0)
    h_all = jnp.pad(h_all, ((0, ntp - n_tok), (0, 0)))
    xs = _scatter_rows(dest_scatter, h_all, n_rows + max(n_trash, 8), tt)
    y_rows = _expert_matmul(blk_e, n_used, xs, prm["w_gu"], prm["b_gu"], prm["w_dn"], prm["b_dn"],
                            n_rows, tm)

    gates = jnp.concatenate(gates_list, axis=1)[:TOP_K]
    gates = jnp.pad(gates, ((0, 0), (0, ntp - n_tok)))
    gate_t = gates.reshape(TOP_K, nt, tt).transpose(1, 2, 0)
    outs = []
    tok0 = 0
    for x1, gt, n in zip(x1_list, gt_list, n_group):
        bx, sx, _ = x1.shape
        tile0 = tok0 // tt
        if sx % tt:
            padn = tt - sx
            x1p = jnp.pad(x1, ((0, 0), (0, padn), (0, 0)))
            gtp = jnp.pad(gt, ((0, 0), (0, padn), (0, 0)))
            o = _combine(dest_gather, y_rows, x1p, gtp, gate_t[tile0:tile0 + 1], tile0, tt)[:, :sx]
        else:
            o = _combine(dest_gather, y_rows, x1, gt, gate_t[tile0:tile0 + bx * (sx // tt)], tile0, tt)
        outs.append(o)
        tok0 += n
    return outs


def _flatten_rows(a):
    return a.transpose(1, 0, 2).reshape(8, -1)


def kernel(x_prompt, x_sample, c_prompt, c_sample, cache_diff_k, cache_diff_v, cache_nsa_k, cache_nsa_v, state_win_kv, page_table, attn_norm_g, ffn_norm_g, ada_w, ada_b, w_in, w_out, diff_q_norm_g, diff_k_norm_g, diff_lambda, diff_sub_norm_g, nsa_q_norm_g, nsa_k_norm_g, nsa_ck_norm_g, nsa_cmp_pe, nsa_cmp_w, router_w, router_b, expert_w_gu, expert_b_gu, expert_w_down, expert_b_down):
    depth = w_in.shape[0]
    assert depth == 1, "single-layer trunk"
    bp, sp, d = x_prompt.shape
    bs, ss, _ = x_sample.shape
    assert ss == 1
    n_pages = page_table.shape[1]
    past_len = n_pages * PAGE
    w_buf = state_win_kv.shape[2]
    n_pool = cache_diff_k.shape[1]
    l = 0
    lam_init = 0.8 - 0.6 * math.exp(-0.3 * l)
    d_ff = expert_w_down.shape[2]

    prm = {
        "attn_g": attn_norm_g[l].reshape(1, d),
        "ffn_g": ffn_norm_g[l].reshape(1, d),
        "w_in": jnp.pad(w_in[l], ((0, 0), (0, IN_PAD - w_in.shape[2]))).astype(BF16),
        "gd": jnp.stack([jnp.tile(diff_q_norm_g[l], 16), jnp.tile(diff_k_norm_g[l], 16)]),
        "gn": jnp.tile(nsa_q_norm_g[l], 8).reshape(1, 512),
        "gk": jnp.concatenate([jnp.tile(nsa_k_norm_g[l, 0], 2), jnp.tile(nsa_k_norm_g[l, 1], 2)]).reshape(1, 256),
        "m32": _group_mean_matrix(A_DQK, BF16),
        "m64": _group_mean_matrix(HEAD_DIM, BF16),
        "m32_f32": _group_mean_matrix(A_DQK, F32),
        "m64_f32": _group_mean_matrix(HEAD_DIM, F32),
        "w_in_f32": jnp.pad(w_in[l], ((0, 0), (0, IN_PAD - w_in.shape[2]))),
        "w_out": w_out[l].astype(BF16),
        "w_out_f32": w_out[l],
        "router_wt": router_w[l].T,
        "router_b": router_b[l].reshape(N_EXPERTS, 1),
        "w_gu": expert_w_gu[l].astype(BF16),
        "b_gu": expert_b_gu[l].reshape(N_EXPERTS, 1, 2 * d_ff),
        "w_dn": expert_w_down[l].astype(BF16),
        "b_dn": expert_b_down[l].reshape(N_EXPERTS, 1, d),
    }
    lam_p = diff_lambda[l]
    sub_g2 = jnp.tile(diff_sub_norm_g[l], 2).reshape(1, LANES)
    sub_g8 = jnp.tile(diff_sub_norm_g[l], 8).reshape(1, 512)
    pe = nsa_cmp_pe[l]
    cw = nsa_cmp_w[l]
    ckg = nsa_ck_norm_g[l].reshape(1, HEAD_DIM)

    n_c = bp + bs
    n_cp = -(-n_c // 8) * 8
    c_all = jnp.pad(jnp.concatenate([c_prompt, c_sample], axis=0), ((0, n_cp - n_c), (0, 0)))
    mod = _modulation(c_all, ada_w[l], ada_b[l])
    mod_p = mod[:bp].reshape(bp, 1, 6, d)
    mod_s = mod[bp:n_c].reshape(1, bs, 6, d)
    sh1p, sc1p, gt1p, sh2p, sc2p, gt2p = [mod_p[:, :, i] for i in range(6)]
    sh1s, sc1s, gt1s, sh2s, sc2s, gt2s = [mod_s[:, :, i] for i in range(6)]

    ts = min(512, sp)
    pos_p = jnp.arange(sp, dtype=I32)
    (qa, dk, dkb, dv, dvb, qn, nk, nv, win, kb, vb, gate) = _in_projection(
        x_prompt, sh1p, sc1p, pos_p, prm, ts)
    tq = min(256, sp)
    o_a = _diff_attention_prompt(qa, dkb, dvb, lam_p, sub_g2, lam_init, tq)
    tqn = min(128, sp)
    o_b = _nsa_prompt(qn, gate, nk, nv, kb, vb, pe, cw, ckg, tqn, min(512, sp))
    x1p, h2p, idsp, gatesp = _out_projection(o_a, o_b, x_prompt, gt1p, sh2p, sc2p, prm, ts)

    xs_ = x_sample.reshape(1, bs, d)
    pos_s = jnp.full((bs,), past_len, I32)
    (qa_s, dk_s, _, dv_s, _, qn_s, nk_s, nv_s, win_s, _, _, gate_s) = _in_projection(
        xs_, sh1s, sc1s, pos_s, prm, bs, precise=True)
    ckt = jnp.transpose(cache_diff_k[l], (0, 2, 3, 1)).reshape(n_pool, 512, PAGE)
    cvt = jnp.transpose(cache_diff_v[l], (0, 2, 3, 1)).reshape(n_pool, 512, PAGE)
    nkt = jnp.transpose(cache_nsa_k[l], (0, 2, 3, 4, 1)).reshape(n_pool, 256, PAGE)
    nvt = jnp.transpose(cache_nsa_v[l], (0, 2, 3, 4, 1)).reshape(n_pool, 256, PAGE)
    state = state_win_kv[l].reshape(bs, w_buf, 256)
    state_t = jnp.transpose(state_win_kv[l], (0, 2, 3, 4, 1)).reshape(bs, 256, w_buf)
    pps = 8 if n_pages % 8 == 0 else 1
    as3 = lambda a: a.reshape(bs, 1, a.shape[-1])
    col = lambda a: a.reshape(bs, a.shape[-1], 1)
    o_a_s = _diff_attention_sample(page_table, lam_p, col(qa_s[0]), col(dk_s[0]), col(dv_s[0]),
                                   diff_sub_norm_g[l].reshape(HEAD_DIM, 1), ckt, cvt, lam_init, pps)
    ocmp_s, sel_s = _nsa_decode_cmp(page_table, as3(qn_s[0]), col(nk_s[0][:, :LANES]),
                                    col(nv_s[0][:, :LANES]), jnp.transpose(pe, (0, 2, 1)),
                                    jnp.transpose(cw, (0, 2, 1)), ckg.reshape(HEAD_DIM, 1),
                                    nkt, nvt, past_len, pps)
    n_sel = min(N_SELECT, past_len // CMP_BLOCK + 1)
    sel2 = jnp.concatenate([sel_s[:, 0, :n_sel], sel_s[:, 1, :n_sel]], axis=1)
    o_b_s = _nsa_decode_mix(page_table, sel2, as3(qn_s[0]), as3(gate_s[0]),
                            ocmp_s.reshape(bs, B_KV_HEADS, B_GROUP, HEAD_DIM), as3(nk_s[0]), as3(nv_s[0]),
                            as3(win_s[0]), state_t, nkt, nvt, past_len, n_sel)
    x1s, h2s, idss, gatess = _out_projection(o_a_s.reshape(1, bs, 512), o_b_s.reshape(1, bs, 512),
                                             xs_, gt1s, sh2s, sc2s, prm, bs, precise=True)

    y_p, y_s = _moe([h2p, h2s], [_flatten_rows(idsp), _flatten_rows(idss)],
                    [_flatten_rows(gatesp), _flatten_rows(gatess)], [x1p, x1s], [gt2p, gt2s],
                    prm, 256, 256)

    p_win = win[:, sp - w_buf:] if sp >= w_buf else jnp.pad(win, ((0, 0), (w_buf - sp, 0), (0, 0)))
    s_win = jnp.concatenate([state, win_s[0][:, None, :]], axis=1)[:, 1:]
    return (y_p, y_s.reshape(bs, 1, d),
            dk.reshape(1, bp, sp, A_HEADS, 2 * A_DQK), dv.reshape(1, bp, sp, A_HEADS, HEAD_DIM),
            nk.reshape(1, bp, sp, 2, B_KV_HEADS, HEAD_DIM), nv.reshape(1, bp, sp, 2, B_KV_HEADS, HEAD_DIM),
            p_win.reshape(1, bp, w_buf, 2, B_KV_HEADS, HEAD_DIM),
            dk_s.reshape(1, bs, 1, A_HEADS, 2 * A_DQK), dv_s.reshape(1, bs, 1, A_HEADS, HEAD_DIM),
            nk_s.reshape(1, bs, 1, 2, B_KV_HEADS, HEAD_DIM), nv_s.reshape(1, bs, 1, 2, B_KV_HEADS, HEAD_DIM),
            s_win.reshape(1, bs, w_buf, 2, B_KV_HEADS, HEAD_DIM))
```

```python
import functools
import math

import jax
import jax.numpy as jnp
from jax import lax
from jax.experimental import pallas as pl
from jax.experimental.pallas import tpu as pltpu

F32 = jnp.float32
BF16 = jnp.bfloat16
I32 = jnp.int32
HI = lax.Precision.HIGHEST

HEAD_DIM = 64
A_HEADS = 8
A_DQK = 32
B_HEADS = 8
B_KV_HEADS = 2
B_GROUP = 4
CMP_BLOCK = 64
N_SELECT = 16
WINDOW = 512
ROPE_THETA = 500000.0
ROPE_FRACTION = 4
N_EXPERTS = 32
TOP_K = 4
SWIGLU_LIMIT = 7.0
SWIGLU_ALPHA = 1.702
EPS = 1e-6
NEG_BIG = -1e30
FORCE_SCORE = 1e4
PAGE = 128

A_Q = 512
IN_PAD = 2944
LANES = 128
STRIP = 32
LOG2E = 1.4426950408889634
VMEM_LIMIT = 56 * 1024 * 1024


def _cparams(sem, vmem=VMEM_LIMIT):
    return pltpu.CompilerParams(dimension_semantics=sem, vmem_limit_bytes=vmem)


def _nt(a, b, precision=None):
    return lax.dot_general(a, b, (((1,), (1,)), ((), ())),
                           preferred_element_type=F32, precision=precision)


def _mm(a, b, precision=None):
    return jnp.dot(a, b, preferred_element_type=F32, precision=precision)


def _mod_kernel(c_ref, w_ref, b_ref, o_ref):
    c = c_ref[...]
    s = c / (1.0 + jnp.exp(-c))
    o_ref[...] = _mm(s.astype(BF16), w_ref[...].astype(BF16)) + b_ref[...]


def _modulation(c_all, ada_w, ada_b):
    n, d = c_all.shape
    width = ada_w.shape[1]
    tn = 1024
    return pl.pallas_call(
        _mod_kernel,
        out_shape=jax.ShapeDtypeStruct((n, width), F32),
        grid=(width // tn,),
        in_specs=[pl.BlockSpec((n, d), lambda j: (0, 0)),
                  pl.BlockSpec((d, tn), lambda j: (0, j)),
                  pl.BlockSpec((1, tn), lambda j: (0, j))],
        out_specs=pl.BlockSpec((n, tn), lambda j: (0, j)),
        compiler_params=_cparams(("arbitrary",)),
        name="adaln_mod",
    )(c_all, ada_w, ada_b.reshape(1, width))


def _group_norm(seg, gmat, gvec):
    out = []
    for j in range(seg.shape[1] // 256):
        c = seg[:, 256 * j:256 * (j + 1)]
        if gmat.dtype == F32:
            ms = _mm(c * c, gmat, HI)
        else:
            ms = _mm((c * c).astype(BF16), gmat)
        out.append(c * lax.rsqrt(ms + EPS) * gvec[:, 256 * j:256 * (j + 1)])
    return out[0] if len(out) == 1 else jnp.concatenate(out, axis=1)


def _rope_lanes(seg, tab_ref, half):
    w = seg.shape[1]
    rep = w // LANES
    cos = jnp.concatenate([tab_ref[0]] * rep, axis=1)
    s_up = jnp.concatenate([tab_ref[1]] * rep, axis=1)
    s_dn = jnp.concatenate([tab_ref[2]] * rep, axis=1)
    return (seg * cos + pltpu.roll(seg, w - half, 1) * s_up
            + pltpu.roll(seg, half, 1) * s_dn)


def _inproj_kernel(x_ref, sh_ref, sc_ref, g_ref, w_ref, gd_ref, gn_ref, gk_ref,
                   m32_ref, m64_ref, td_ref, tn_ref,
                   qa_ref, dk_ref, dkb_ref, dv_ref, dvb_ref, qn_ref, nk_ref, nv_ref,
                   win_ref, kb_ref, vb_ref, gate_ref):
    x = x_ref[0]
    ms = jnp.mean(x * x, axis=-1, keepdims=True)
    h = x * lax.rsqrt(ms + EPS) * g_ref[...]
    h = h * (1.0 + sc_ref[0]) + sh_ref[0]
    proj = _mm(h.astype(BF16), w_ref[...])
    m32 = m32_ref[...]
    m64 = m64_ref[...]
    gd = gd_ref[...]

    qa = _rope_lanes(_group_norm(proj[:, 0:512], m32, gd[0:1]), td_ref, 4)
    q_scale = A_DQK ** -0.5 * LOG2E if qa_ref.dtype == BF16 else 1.0
    qa_ref[0] = (qa * q_scale).astype(qa_ref.dtype)
    ka = _rope_lanes(_group_norm(proj[:, 512:1024], m32, gd[1:2]), td_ref, 4)
    dk_ref[0] = ka
    dkb_ref[0] = ka.astype(BF16)
    va = proj[:, 1024:1536]
    dv_ref[0] = va
    dvb_ref[0] = va.astype(BF16)
    qn = _rope_lanes(_group_norm(proj[:, 1536:2048], m64, gn_ref[...]), tn_ref, 8)
    qn_ref[0] = qn * (HEAD_DIM ** -0.5)
    k_cmp = _rope_lanes(proj[:, 2048:2176], tn_ref, 8)
    k_sw = _rope_lanes(_group_norm(proj[:, 2176:2432], m64, gk_ref[...]), tn_ref, 8)
    vb = proj[:, 2432:2816]
    nk_ref[0] = jnp.concatenate([k_cmp, k_sw[:, 0:128]], axis=1)
    nv_ref[0] = vb[:, 0:256]
    win_ref[0] = jnp.concatenate([k_sw[:, 128:256], vb[:, 256:384]], axis=1)
    kb_ref[0] = jnp.concatenate([k_cmp, k_sw], axis=1).astype(BF16)
    vb_ref[0] = vb.astype(BF16)
    gl = proj[:, 2816:2944]
    gate_ref[0] = 1.0 / (1.0 + jnp.exp(-gl))


def _rope_tables(pos, group, half):
    inv = ROPE_THETA ** (-jnp.arange(half, dtype=F32) / half)
    ang = pos.astype(F32)[:, None] * inv[None, :]
    cos, sin = jnp.cos(ang), jnp.sin(ang)
    n = pos.shape[0]
    pad = group - 2 * half
    c = jnp.concatenate([cos, cos, jnp.ones((n, pad), F32)], axis=1)
    up = jnp.concatenate([-sin, jnp.zeros((n, half + pad), F32)], axis=1)
    dn = jnp.concatenate([jnp.zeros((n, half), F32), sin, jnp.zeros((n, pad), F32)], axis=1)
    rep = LANES // group
    return jnp.stack([jnp.tile(c, (1, rep)), jnp.tile(up, (1, rep)), jnp.tile(dn, (1, rep))])


def _group_mean_matrix(group, dtype):
    i = jnp.arange(256)
    return jnp.where((i[:, None] // group) == (i[None, :] // group), 1.0 / group, 0.0).astype(dtype)


def _in_projection(x, sh, sc, pos, prm, ts, sample=False):
    bx, sx, d = x.shape
    r = sh.shape[1]
    rb = 1 if r == 1 else ts
    td = _rope_tables(pos, A_DQK, A_DQK // ROPE_FRACTION // 2)
    tn = _rope_tables(pos, HEAD_DIM, HEAD_DIM // ROPE_FRACTION // 2)

    def tok(width, dtype):
        return (jax.ShapeDtypeStruct((bx, sx, width), dtype),
                pl.BlockSpec((1, ts, width), lambda s, b: (b, s, 0)))

    sfx = "_f32" if sample else ""
    outs = [tok(512, F32 if sample else BF16), tok(512, F32), tok(512, BF16), tok(512, F32), tok(512, BF16),
            tok(512, F32), tok(256, F32), tok(256, F32), tok(256, F32), tok(384, BF16),
            tok(384, BF16), tok(128, F32)]
    const = lambda shape: pl.BlockSpec(shape, lambda s, b: (0,) * len(shape))
    mod_spec = pl.BlockSpec((1, rb, d), (lambda s, b: (b, 0, 0)) if r == 1 else (lambda s, b: (b, s, 0)))
    return pl.pallas_call(
        _inproj_kernel,
        out_shape=[o[0] for o in outs],
        grid=(sx // ts, bx),
        in_specs=[pl.BlockSpec((1, ts, d), lambda s, b: (b, s, 0)), mod_spec, mod_spec,
                  const((1, d)), const((d, IN_PAD)), const((2, 512)), const((1, 512)),
                  const((1, 256)), const((256, 256)), const((256, 256)),
                  pl.BlockSpec((3, ts, LANES), lambda s, b: (0, s, 0)),
                  pl.BlockSpec((3, ts, LANES), lambda s, b: (0, s, 0))],
        out_specs=[o[1] for o in outs],
        compiler_params=_cparams(("arbitrary", "arbitrary")),
        name="in_projection",
    )(x, sh, sc, prm["attn_g"], prm["w_in"], prm["gd"], prm["gn"], prm["gk"],
      prm["m32" + sfx], prm["m64" + sfx], td, tn)


def _diff_lambda(lam_ref, lam_init):
    lp = lam_ref[...]
    a = jnp.sum(lp[0:1] * lp[1:2], axis=1, keepdims=True)
    b = jnp.sum(lp[2:3] * lp[3:4], axis=1, keepdims=True)
    return jnp.exp(a) - jnp.exp(b) + lam_init


def _head_rms(o, sg):
    lane = lax.broadcasted_iota(I32, (1, LANES), 1)
    sq = o * o
    s0 = jnp.sum(jnp.where(lane < HEAD_DIM, sq, 0.0), axis=1, keepdims=True)
    s1 = jnp.sum(jnp.where(lane >= HEAD_DIM, sq, 0.0), axis=1, keepdims=True)
    ms = jnp.where(lane < HEAD_DIM, s0, s1) * (1.0 / HEAD_DIM)
    return o * lax.rsqrt(ms + EPS) * sg


def _diff_attn_kernel(lam_ref, q_ref, k_ref, v_ref, sg_ref, o_ref, *, tq, lam_init):
    qi = pl.program_id(2)
    q = q_ref[0]
    lane = lax.broadcasted_iota(I32, (1, LANES), 1)
    lam = _diff_lambda(lam_ref, lam_init)
    zero = jnp.zeros_like(q)
    qm = [jnp.where((lane >= A_DQK * i) & (lane < A_DQK * (i + 1)), q, zero) for i in range(4)]
    first_head = lane < HEAD_DIM
    one = jnp.ones((tq, LANES), BF16)

    def chunk(j, carry, masked):
        ms, accs = carry
        start = pl.multiple_of(j * tq, tq)
        kc = k_ref[0, pl.ds(start, tq), :]
        vc = v_ref[0, pl.ds(start, tq), :]
        vaug = (jnp.where(first_head, vc, one), jnp.where(first_head, one, vc))
        ss = [_nt(qm[i], kc) for i in range(4)]
        new_ms = [[] for _ in range(4)]
        alphas = [[] for _ in range(4)]
        ps = [[] for _ in range(4)]
        for r in range(tq // STRIP):
            rows = slice(r * STRIP, (r + 1) * STRIP)
            if masked:
                row = r * STRIP + lax.broadcasted_iota(I32, (STRIP, tq), 0)
                causal = lax.broadcasted_iota(I32, (STRIP, tq), 1) <= row
            for i in range(4):
                s = ss[i][rows]
                if masked:
                    s = jnp.where(causal, s, NEG_BIG)
                m_old = ms[i][rows]
                m_new = jnp.maximum(m_old, jnp.max(s, axis=1, keepdims=True))
                ps[i].append(jnp.exp2(s - jnp.concatenate([m_new] * (tq // LANES), axis=1)).astype(BF16))
                alphas[i].append(jnp.exp2(m_old - m_new))
                new_ms[i].append(m_new)
        new_accs = []
        for i in range(4):
            pv = _mm(jnp.concatenate(ps[i], axis=0), vaug[i // 2])
            new_accs.append(jnp.concatenate(alphas[i], axis=0) * accs[i] + pv)
        return tuple(jnp.concatenate(m, axis=0) for m in new_ms), tuple(new_accs)

    init = (tuple(jnp.full((tq, LANES), NEG_BIG, F32) for _ in range(4)),
            tuple(jnp.zeros((tq, LANES), F32) for _ in range(4)))
    carry = lax.fori_loop(0, qi, lambda j, c: chunk(j, c, False), init)
    _, accs = chunk(qi, carry, True)
    outs = [a / pltpu.roll(a, HEAD_DIM, 1) for a in accs]
    o0 = jnp.where(first_head, outs[0], outs[2])
    o1 = jnp.where(first_head, outs[1], outs[3])
    o = o0 - lam * o1
    o_ref[0] = (_head_rms(o, sg_ref[...]) * (1.0 - lam_init)).astype(BF16)


def _diff_attention_prompt(qa, dkb, dvb, lam_p, sub_g2, lam_init, tq):
    b, s, _ = qa.shape
    kern = functools.partial(_diff_attn_kernel, tq=tq, lam_init=lam_init)
    return pl.pallas_call(
        kern,
        out_shape=jax.ShapeDtypeStruct((b, s, 512), BF16),
        grid=(b, A_HEADS // 2, s // tq),
        in_specs=[pl.BlockSpec((4, A_DQK), lambda b_, p, i: (0, 0)),
                  pl.BlockSpec((1, tq, LANES), lambda b_, p, i: (b_, i, p)),
                  pl.BlockSpec((1, s, LANES), lambda b_, p, i: (b_, 0, p)),
                  pl.BlockSpec((1, s, LANES), lambda b_, p, i: (b_, 0, p)),
                  pl.BlockSpec((1, LANES), lambda b_, p, i: (0, 0))],
        out_specs=pl.BlockSpec((1, tq, LANES), lambda b_, p, i: (b_, i, p)),
        compiler_params=_cparams(("arbitrary", "arbitrary", "arbitrary")),
        name="diff_attention_prompt",
    )(lam_p, qa, dkb, dvb, sub_g2)


def _top_select_t(score_t, n_sel):
    nb = score_t.shape[0]
    blk = lax.broadcasted_iota(I32, (nb, 1), 0)
    cnt = jnp.zeros(score_t.shape, F32)
    for i in range(nb):
        row = score_t[i:i + 1, :]
        gt = jnp.where(row > score_t, 1.0, 0.0)
        eq = jnp.where(row == score_t, 1.0, 0.0)
        cnt = cnt + gt + jnp.where(blk > i, eq, 0.0)
    return jnp.where(cnt < n_sel, 1.0, 0.0)


def _strip_softmax(s, m_old, mask_fn):
    m_new, alpha, ps = [], [], []
    for r in range(s.shape[0] // STRIP):
        rows = slice(r * STRIP, (r + 1) * STRIP)
        sr = s[rows] if mask_fn is None else mask_fn(s[rows], rows)
        mn = jnp.broadcast_to(jnp.max(sr, axis=1, keepdims=True), (STRIP, LANES))
        if m_old is not None:
            mn = jnp.maximum(m_old[rows], mn)
            alpha.append(jnp.exp2(m_old[rows] - mn))
        ps.append(jnp.exp2(sr - jnp.concatenate([mn] * (s.shape[1] // LANES), axis=1)).astype(BF16))
        m_new.append(mn)
    cat = lambda xs: jnp.concatenate(xs, axis=0)
    return cat(m_new), (cat(alpha) if alpha else None), cat(ps)


def _compress(kc, pe_sum, w, nb):
    mean = (jnp.sum(kc.reshape(nb, CMP_BLOCK, HEAD_DIM), axis=1) + pe_sum) * (1.0 / CMP_BLOCK)
    return _mm(mean, w, HI)


def _nsa_prompt_kernel(q_ref, gate_ref, kcmp_ref, vcmp_ref, kslc_ref, vslc_ref, kwin_ref,
                       vwin_ref, pe_ref, cw_ref, ckg_ref, o_ref,
                       kaug, vs, kw, vw, ck, cv, *, tq, tk, seq):
    qi = pl.program_id(1)
    nb = seq // CMP_BLOCK
    n_sel = min(N_SELECT, nb)
    wk = WINDOW + tq

    @pl.when(qi == 0)
    def _():
        rowblk = lax.broadcasted_iota(I32, (seq, HEAD_DIM), 0) // CMP_BLOCK
        colblk = lax.broadcasted_iota(I32, (seq, HEAD_DIM), 1)
        onehot = jnp.where(rowblk == colblk, 1.0, 0.0).astype(BF16)
        pe_k = jnp.sum(pe_ref[0], axis=0, keepdims=True)
        pe_v = jnp.sum(pe_ref[1], axis=0, keepdims=True)
        ones = jnp.ones((seq, HEAD_DIM), BF16)
        for g in range(B_KV_HEADS):
            lo, hi = HEAD_DIM * g, HEAD_DIM * (g + 1)
            kaug[g] = jnp.concatenate([kslc_ref[0][:, lo:hi], onehot], axis=1)
            vs[g] = jnp.concatenate([vslc_ref[0][:, lo:hi], ones], axis=1)
            kw[g] = kwin_ref[0][:, lo:hi]
            vw[g] = jnp.concatenate([vwin_ref[0][:, lo:hi], ones], axis=1)
            c = _compress(kcmp_ref[0][:, lo:hi], pe_k, cw_ref[0], nb)
            ms = jnp.mean(c * c, axis=-1, keepdims=True)
            ck[g] = c * lax.rsqrt(ms + EPS) * ckg_ref[...]
            cv[g] = _compress(vcmp_ref[0][:, lo:hi], pe_v, cw_ref[1], nb)

    q = q_ref[0]
    gates = gate_ref[0]
    q0 = qi * tq
    qpos = q0 + lax.broadcasted_iota(I32, (tq, 1), 0)
    blk = lax.broadcasted_iota(I32, (1, nb), 1)
    complete = (blk + 1) * CMP_BLOCK - 1 <= qpos
    cur = qpos // CMP_BLOCK
    forced = blk * (blk - cur) == 0
    qpos4 = jnp.concatenate([qpos] * B_GROUP, axis=0)
    n_full = q0 // tk
    wstart = pl.multiple_of(jnp.maximum(q0 - WINDOW, 0), tq)
    kpos_w = wstart + lax.broadcasted_iota(I32, (1, wk), 1)
    qpos_t = q0 + lax.broadcasted_iota(I32, (1, tq), 1)
    blk_t = lax.broadcasted_iota(I32, (nb, 1), 0)
    complete_t = (blk_t + 1) * CMP_BLOCK - 1 <= qpos_t
    forced_t = blk_t * (blk_t - qpos_t // CMP_BLOCK) == 0
    eye = jnp.where(lax.broadcasted_iota(I32, (tq, tq), 0) == lax.broadcasted_iota(I32, (tq, tq), 1),
                    1.0, 0.0).astype(BF16)
    o_cmp_all, q4_all, qaug_all = [], [], []

    for g in range(B_KV_HEADS):
        qh = [q[:, HEAD_DIM * (B_GROUP * g + r):HEAD_DIM * (B_GROUP * g + r + 1)]
              for r in range(B_GROUP)]
        ckg = ck[g]
        cvg = cv[g]
        o_cmp = []
        imp_t = jnp.zeros((nb, tq), F32)
        for r in range(B_GROUP):
            s = jnp.where(complete, _nt(qh[r], ckg, HI), NEG_BIG)
            e = jnp.where(complete, jnp.exp(s - jnp.max(s, axis=1, keepdims=True)), 0.0)
            p = e / jnp.maximum(jnp.sum(e, axis=1, keepdims=True), 1e-30)
            o_cmp.append(_mm(p, cvg, HI))
            st = jnp.where(complete_t, _nt(ckg, qh[r], HI), NEG_BIG)
            et = jnp.where(complete_t, jnp.exp(st - jnp.max(st, axis=0, keepdims=True)), 0.0)
            imp_t = imp_t + et / jnp.maximum(jnp.sum(et, axis=0, keepdims=True), 1e-30)
        score_t = jnp.where(forced_t, FORCE_SCORE, jnp.where(complete_t, imp_t, -1.0))
        sel_t = _top_select_t(score_t, n_sel)
        sel = _nt(eye, sel_t.astype(BF16))
        bias = jnp.where(blk <= cur, jnp.where(sel > 0.5, 0.0, NEG_BIG), NEG_BIG).astype(BF16)
        if nb < HEAD_DIM:
            bias = jnp.concatenate([bias, jnp.zeros((tq, HEAD_DIM - nb), BF16)], axis=1)

        q4 = jnp.concatenate([(qh[r] * LOG2E).astype(BF16) for r in range(B_GROUP)], axis=0)
        o_cmp_all.append(o_cmp)
        q4_all.append(q4)
        qaug_all.append(jnp.concatenate([q4, jnp.concatenate([bias] * B_GROUP, axis=0)], axis=1))

    def chunk(j, carry, masked):
        start = pl.multiple_of(j * tk, tk)
        mask_fn = None
        if masked:
            kpos = start + lax.broadcasted_iota(I32, (1, tk), 1)
            mask_fn = lambda sr, rows: jnp.where(kpos <= qpos4[rows], sr, NEG_BIG)
        out = []
        for g in range(B_KV_HEADS):
            m, acc = carry[g]
            s = _nt(qaug_all[g], kaug[g, pl.ds(start, tk), :])
            m_new, alpha, p = _strip_softmax(s, m, mask_fn)
            out.append((m_new, alpha * acc + _mm(p, vs[g, pl.ds(start, tk), :])))
        return tuple(out)

    init = tuple((jnp.full((B_GROUP * tq, LANES), NEG_BIG, F32), jnp.zeros((B_GROUP * tq, LANES), F32))
                 for _ in range(B_KV_HEADS))
    carry = lax.fori_loop(0, n_full, lambda j, c: chunk(j, c, False), init)
    slc = chunk(n_full, carry, True)

    def in_window(sr, rows):
        dlt = qpos4[rows] - kpos_w
        return jnp.where(dlt >= 0, jnp.where(dlt < WINDOW, sr, NEG_BIG), NEG_BIG)

    heads_out = []
    for g in range(B_KV_HEADS):
        acc = slc[g][1]
        o_slc = acc[:, 0:HEAD_DIM] / acc[:, HEAD_DIM:LANES]
        sw = _nt(q4_all[g], kw[g, pl.ds(wstart, wk), :])
        _, _, pw = _strip_softmax(sw, None, in_window)
        accw = _mm(pw, vw[g, pl.ds(wstart, wk), :])
        o_win = accw[:, 0:HEAD_DIM] / accw[:, HEAD_DIM:LANES]
        for r in range(B_GROUP):
            h = B_GROUP * g + r
            rows = slice(r * tq, (r + 1) * tq)
            heads_out.append(gates[:, h:h + 1] * o_cmp_all[g][r]
                             + gates[:, 8 + h:9 + h] * o_slc[rows]
                             + gates[:, 16 + h:17 + h] * o_win[rows])
    o_ref[0] = jnp.concatenate(heads_out, axis=1).astype(BF16)


def _nsa_prompt(qn, gates, nk, nv, kb, vb, pe, cw, ckg, tq, tk):
    b, s, _ = qn.shape
    nb = s // CMP_BLOCK
    kern = functools.partial(_nsa_prompt_kernel, tq=tq, tk=tk, seq=s)
    full = lambda lane_blk: pl.BlockSpec((1, s, LANES), lambda b_, i: (b_, 0, lane_blk))
    const = lambda shape: pl.BlockSpec(shape, lambda b_, i: (0,) * len(shape))
    return pl.pallas_call(
        kern,
        out_shape=jax.ShapeDtypeStruct((b, s, 512), BF16),
        grid=(b, s // tq),
        in_specs=[pl.BlockSpec((1, tq, 512), lambda b_, i: (b_, i, 0)),
                  pl.BlockSpec((1, tq, LANES), lambda b_, i: (b_, i, 0)),
                  full(0), full(0), full(1), full(1), full(2), full(2),
                  const((2, CMP_BLOCK, HEAD_DIM)), const((2, HEAD_DIM, HEAD_DIM)),
                  const((1, HEAD_DIM))],
        out_specs=pl.BlockSpec((1, tq, 512), lambda b_, i: (b_, i, 0)),
        scratch_shapes=[pltpu.VMEM((2, s, LANES), BF16), pltpu.VMEM((2, s, LANES), BF16),
                        pltpu.VMEM((2, s, HEAD_DIM), BF16), pltpu.VMEM((2, s, LANES), BF16),
                        pltpu.VMEM((2, nb, HEAD_DIM), F32), pltpu.VMEM((2, nb, HEAD_DIM), F32)],
        compiler_params=_cparams(("arbitrary", "arbitrary")),
        name="nsa_prompt",
    )(qn, gates, nk, nv, kb, vb, kb, vb, pe, cw, ckg)


def _outproj_kernel(oa_ref, ob_ref, x_ref, gt_ref, sh_ref, sc_ref, g_ref, w_ref, rw_ref, rb_ref,
                    x1_ref, h2_ref, ids_ref, gates_ref):
    y = _mm(oa_ref[0], w_ref[0:512, :]) + _mm(ob_ref[0], w_ref[512:1024, :])
    x1 = x_ref[0] + gt_ref[0] * y
    x1_ref[0] = x1
    ms = jnp.mean(x1 * x1, axis=-1, keepdims=True)
    h2 = x1 * lax.rsqrt(ms + EPS) * g_ref[...]
    h2 = h2 * (1.0 + sc_ref[0]) + sh_ref[0]
    h2_ref[0] = h2
    logits = _nt(rw_ref[...].astype(BF16), h2.astype(BF16)) + rb_ref[...]
    eidx = lax.broadcasted_iota(I32, logits.shape, 0)
    work = logits
    vals, ids = [], []
    for _ in range(TOP_K):
        m = jnp.max(work, axis=0, keepdims=True)
        idx = jnp.min(jnp.where(work == m, eidx, N_EXPERTS), axis=0, keepdims=True)
        vals.append(m)
        ids.append(idx)
        work = jnp.where(eidx == idx, -3e38, work)
    es = [jnp.exp(v - vals[0]) for v in vals]
    tot = es[0] + es[1] + es[2] + es[3]
    ids_ref[0] = jnp.concatenate(ids + ids, axis=0)
    gates_ref[0] = jnp.concatenate([e / tot for e in es] * 2, axis=0)


def _out_projection(oa, ob, x, gt, sh, sc, prm, ts):
    bx, sx, d = x.shape
    r = gt.shape[1]
    rb = 1 if r == 1 else ts
    mod_spec = pl.BlockSpec((1, rb, d), (lambda b, s: (b, 0, 0)) if r == 1 else (lambda b, s: (b, s, 0)))
    const = lambda shape: pl.BlockSpec(shape, lambda b, s: (0,) * len(shape))
    tokspec = lambda w: pl.BlockSpec((1, ts, w), lambda b, s: (b, s, 0))
    nt = sx // ts
    return pl.pallas_call(
        _outproj_kernel,
        out_shape=[jax.ShapeDtypeStruct((bx, sx, d), F32), jax.ShapeDtypeStruct((bx, sx, d), F32),
                   jax.ShapeDtypeStruct((bx * nt, 8, ts), I32),
                   jax.ShapeDtypeStruct((bx * nt, 8, ts), F32)],
        grid=(bx, nt),
        in_specs=[tokspec(512), tokspec(512), tokspec(d), mod_spec, mod_spec, mod_spec,
                  const((1, d)), const((d, d)), const((N_EXPERTS, d)), const((N_EXPERTS, 1))],
        out_specs=[tokspec(d), tokspec(d),
                   pl.BlockSpec((1, 8, ts), lambda b, s: (b * nt + s, 0, 0)),
                   pl.BlockSpec((1, 8, ts), lambda b, s: (b * nt + s, 0, 0))],
        compiler_params=_cparams(("arbitrary", "arbitrary")),
        name="out_projection",
    )(oa, ob, x, gt, sh, sc, prm["ffn_g"], prm["w_out"], prm["router_wt"], prm["router_b"])


def _route_kernel(ids_ref, pos_ref, cnt_ref, carry, *, tt):
    @pl.when(pl.program_id(0) == 0)
    def _():
        carry[...] = jnp.zeros_like(carry)

    ids = ids_ref[0]
    e_iota = lax.broadcasted_iota(I32, (N_EXPERTS, tt), 0)
    hits = [ids[k:k + 1, :] == e_iota for k in range(TOP_K)]
    oh = jnp.zeros((N_EXPERTS, tt), F32)
    for k in range(TOP_K):
        oh = oh + jnp.where(hits[k], 1.0, 0.0)
    r = lax.broadcasted_iota(I32, (tt, tt), 0)
    c = lax.broadcasted_iota(I32, (tt, tt), 1)
    upper = jnp.where(r < c, 1.0, 0.0).astype(BF16)
    before = _mm(oh.astype(BF16), upper) + carry[:, 0:1]
    rows = [jnp.sum(jnp.where(hits[k], before, 0.0), axis=0, keepdims=True) for k in range(TOP_K)]
    pos_ref[0] = jnp.concatenate(rows + rows, axis=0).astype(I32)
    carry[...] = carry[...] + jnp.sum(oh, axis=1, keepdims=True)
    cnt_ref[...] = carry[...]


def _route_positions(ids3, tt):
    nt = ids3.shape[0]
    return pl.pallas_call(
        functools.partial(_route_kernel, tt=tt),
        out_shape=[jax.ShapeDtypeStruct((nt, 8, tt), I32),
                   jax.ShapeDtypeStruct((N_EXPERTS, LANES), F32)],
        grid=(nt,),
        in_specs=[pl.BlockSpec((1, 8, tt), lambda i: (i, 0, 0))],
        out_specs=[pl.BlockSpec((1, 8, tt), lambda i: (i, 0, 0)),
                   pl.BlockSpec((N_EXPERTS, LANES), lambda i: (0, 0))],
        scratch_shapes=[pltpu.VMEM((N_EXPERTS, LANES), F32)],
        compiler_params=_cparams(("arbitrary",)),
        name="moe_route_positions",
    )(ids3)


def _scatter_kernel(dest_hbm, h_ref, xs_in_hbm, xs_hbm, idx, isem, sem, *, tt):
    del xs_in_hbm
    i = pl.program_id(0)
    cp = pltpu.make_async_copy(dest_hbm.at[i], idx, isem)
    cp.start()
    cp.wait()

    def issue(t, _):
        for k in range(TOP_K):
            pltpu.make_async_copy(h_ref.at[pl.ds(t, 1)], xs_hbm.at[pl.ds(idx[k * tt + t], 1)],
                                  sem).start()
        return 0

    lax.fori_loop(0, tt, issue, 0)
    for k in range(TOP_K):
        pltpu.make_async_copy(h_ref, xs_hbm.at[pl.ds(0, tt)], sem).wait()


def _scatter_rows(dest2, h_all, n_rows, tt):
    nt = dest2.shape[0]
    d = h_all.shape[1]
    return pl.pallas_call(
        functools.partial(_scatter_kernel, tt=tt),
        out_shape=jax.ShapeDtypeStruct((n_rows, d), F32),
        grid=(nt,),
        in_specs=[pl.BlockSpec(memory_space=pl.ANY),
                  pl.BlockSpec((tt, d), lambda i: (i, 0)),
                  pl.BlockSpec(memory_space=pl.ANY)],
        out_specs=pl.BlockSpec(memory_space=pl.ANY),
        scratch_shapes=[pltpu.SMEM((TOP_K * tt,), I32), pltpu.SemaphoreType.DMA,
                        pltpu.SemaphoreType.DMA],
        input_output_aliases={2: 0},
        compiler_params=_cparams(("arbitrary",)),
        name="moe_scatter_rows",
    )(dest2, h_all, jnp.zeros((n_rows, d), F32))


def _expert_kernel(be_ref, nu_ref, x_ref, wgu_ref, bgu_ref, wdn_ref, bdn_ref, y_ref, *, d_ff):
    @pl.when(pl.program_id(0) < nu_ref[0])
    def _():
        gu = _mm(x_ref[...].astype(BF16), wgu_ref[0]) + bgu_ref[0]
        g = jnp.minimum(gu[:, :d_ff], SWIGLU_LIMIT)
        u = jnp.clip(gu[:, d_ff:], -SWIGLU_LIMIT, SWIGLU_LIMIT)
        a = g * (1.0 / (1.0 + jnp.exp(-SWIGLU_ALPHA * g))) * (u + 1.0)
        y_ref[...] = _mm(a.astype(BF16), wdn_ref[0]) + bdn_ref[0]

    @pl.when(pl.program_id(0) >= nu_ref[0])
    def _():
        y_ref[...] = jnp.zeros_like(y_ref)


def _expert_matmul(blk_e, n_used, xs, wgu, bgu, wdn, bdn, n_rows, tm):
    d = xs.shape[1]
    d_ff = wdn.shape[1]
    nblk = n_rows // tm
    row = lambda i, be, nu: (jnp.minimum(i, nu[0] - 1), 0)
    wsel = lambda i, be, nu: (be[jnp.minimum(i, nu[0] - 1)], 0, 0)
    return pl.pallas_call(
        functools.partial(_expert_kernel, d_ff=d_ff),
        out_shape=jax.ShapeDtypeStruct((n_rows, d), F32),
        grid_spec=pltpu.PrefetchScalarGridSpec(
            num_scalar_prefetch=2,
            grid=(nblk,),
            in_specs=[pl.BlockSpec((tm, d), row),
                      pl.BlockSpec((1, d, 2 * d_ff), wsel),
                      pl.BlockSpec((1, 1, 2 * d_ff), wsel),
                      pl.BlockSpec((1, d_ff, d), wsel),
                      pl.BlockSpec((1, 1, d), wsel)],
            out_specs=pl.BlockSpec((tm, d), lambda i, be, nu: (i, 0))),
        compiler_params=_cparams(("arbitrary",)),
        name="moe_expert_matmul",
    )(blk_e, n_used, xs, wgu, bgu, wdn, bdn)


def _combine_kernel(dest_hbm, y_hbm, x1_ref, gt_ref, gate_ref, o_ref, idx, buf, isem, sem, *, tt, tile0):
    b = pl.program_id(0)
    s = pl.program_id(1)
    i = tile0 + b * pl.num_programs(1) + s
    cp = pltpu.make_async_copy(dest_hbm.at[i], idx, isem)
    cp.start()
    cp.wait()

    def issue(t, _):
        for k in range(TOP_K):
            pltpu.make_async_copy(y_hbm.at[pl.ds(idx[k * tt + t], 1)], buf.at[k, pl.ds(t, 1)],
                                  sem).start()
        return 0

    lax.fori_loop(0, tt, issue, 0)
    for k in range(TOP_K):
        pltpu.make_async_copy(y_hbm.at[pl.ds(0, tt)], buf.at[k], sem).wait()
    gate = gate_ref[0]
    moe = gate[:, 0:1] * buf[0]
    for k in range(1, TOP_K):
        moe = moe + gate[:, k:k + 1] * buf[k]
    o_ref[0] = x1_ref[0] + gt_ref[0] * moe


def _combine(dest2, y_rows, x1, gt, gate_t, tile0, tt):
    bx, sx, d = x1.shape
    r = gt.shape[1]
    rb = 1 if r == 1 else tt
    nt = sx // tt
    mod_spec = pl.BlockSpec((1, rb, d), (lambda b, s: (b, 0, 0)) if r == 1 else (lambda b, s: (b, s, 0)))
    return pl.pallas_call(
        functools.partial(_combine_kernel, tt=tt, tile0=tile0),
        out_shape=jax.ShapeDtypeStruct((bx, sx, d), F32),
        grid=(bx, nt),
        in_specs=[pl.BlockSpec(memory_space=pl.ANY), pl.BlockSpec(memory_space=pl.ANY),
                  pl.BlockSpec((1, tt, d), lambda b, s: (b, s, 0)), mod_spec,
                  pl.BlockSpec((1, tt, TOP_K), lambda b, s: (b * nt + s, 0, 0))],
        out_specs=pl.BlockSpec((1, tt, d), lambda b, s: (b, s, 0)),
        scratch_shapes=[pltpu.SMEM((TOP_K * tt,), I32), pltpu.VMEM((TOP_K, tt, d), F32),
                        pltpu.SemaphoreType.DMA, pltpu.SemaphoreType.DMA],
        compiler_params=_cparams(("arbitrary", "arbitrary")),
        name="moe_combine",
    )(dest2, y_rows, x1, gt, gate_t)


def _diff_decode_kernel(pt_ref, lam_ref, q_ref, kn_ref, vn_ref, sg_ref, *rest, pps, lam_init):
    k_refs = rest[:pps]
    v_refs = rest[pps:2 * pps]
    o_ref = rest[2 * pps]
    m_s, l_s, a0_s, a1_s = rest[2 * pps + 1:]
    j = pl.program_id(1)
    rnd = lambda x: x.astype(BF16).astype(F32)
    q = rnd(q_ref[0])
    scale = A_DQK ** -0.5
    n_hc = 2 * A_HEADS

    @pl.when(j == 0)
    def _():
        m_s[...] = jnp.full_like(m_s, NEG_BIG)
        l_s[...] = jnp.zeros_like(l_s)
        a0_s[...] = jnp.zeros_like(a0_s)
        a1_s[...] = jnp.zeros_like(a1_s)

    for p in range(pps):
        s = jnp.sum((rnd(k_refs[p][0]) * q).reshape(n_hc, A_DQK, PAGE), axis=1) * scale
        m_old = m_s[...]
        m_new = jnp.maximum(m_old, s)
        pr = jnp.exp(s - m_new)
        alpha = jnp.exp(m_old - m_new)
        m_s[...] = m_new
        l_s[...] = alpha * l_s[...] + pr
        for h in range(A_HEADS):
            rows = pl.ds(HEAD_DIM * h, HEAD_DIM)
            vt = v_refs[p][0, rows, :]
            for c, acc in ((0, a0_s), (1, a1_s)):
                i = 2 * h + c
                acc[rows, :] = alpha[i:i + 1] * acc[rows, :] + pr[i:i + 1] * vt

    @pl.when(j == pl.num_programs(1) - 1)
    def _():
        lam = _diff_lambda(lam_ref, lam_init)
        m = m_s[...]
        s_new = jnp.sum((rnd(kn_ref[0]) * q).reshape(n_hc, A_DQK, 1), axis=1) * scale
        big = jnp.maximum(jnp.max(m, axis=1, keepdims=True), s_new)
        w = jnp.exp(m - big)
        wn = jnp.exp(s_new - big)
        inv = 1.0 / (jnp.sum(l_s[...] * w, axis=1, keepdims=True) + wn)
        sg = sg_ref[...]
        for h in range(A_HEADS):
            rows = pl.ds(HEAD_DIM * h, HEAD_DIM)
            vn = vn_ref[0, rows, :]
            oc = []
            for c, acc in ((0, a0_s), (1, a1_s)):
                i = 2 * h + c
                num = jnp.sum(acc[rows, :] * w[i:i + 1], axis=1, keepdims=True) + wn[i:i + 1] * vn
                oc.append(num * inv[i:i + 1])
            o = oc[0] - lam * oc[1]
            ms = jnp.mean(o * o, axis=0, keepdims=True)
            o_ref[0, rows, :] = o * lax.rsqrt(ms + EPS) * sg * (1.0 - lam_init)


def _diff_attention_sample(page_table, lam_p, q_col, k_new_col, v_new_col, sub_g_col, cache_kt, cache_vt,
                           lam_init, pps):
    nb, n_pages = page_table.shape
    steps = n_pages // pps
    const = lambda shape: pl.BlockSpec(shape, lambda b, j, pt: (0,) * len(shape))
    per_b = pl.BlockSpec((1, 512, 1), lambda b, j, pt: (b, 0, 0))

    def page_spec(p):
        return pl.BlockSpec((1, 512, PAGE), lambda b, j, pt: (pt[b, j * pps + p], 0, 0))

    kern = functools.partial(_diff_decode_kernel, pps=pps, lam_init=lam_init)
    out = pl.pallas_call(
        kern,
        out_shape=jax.ShapeDtypeStruct((nb, 512, 1), F32),
        grid_spec=pltpu.PrefetchScalarGridSpec(
            num_scalar_prefetch=1,
            grid=(nb, steps),
            in_specs=[const((4, A_DQK)), per_b, per_b, per_b, const((HEAD_DIM, 1))]
                     + [page_spec(p) for p in range(pps)] * 2,
            out_specs=pl.BlockSpec((1, 512, 1), lambda b, j, pt: (b, 0, 0)),
            scratch_shapes=[pltpu.VMEM((2 * A_HEADS, PAGE), F32), pltpu.VMEM((2 * A_HEADS, PAGE), F32),
                            pltpu.VMEM((512, PAGE), F32), pltpu.VMEM((512, PAGE), F32)]),
        compiler_params=_cparams(("arbitrary", "arbitrary")),
        name="diff_attention_sample",
    )(page_table, lam_p, q_col, k_new_col, v_new_col, sub_g_col,
      *([cache_kt] * pps), *([cache_vt] * pps))
    return out[:, :, 0]


def _nsa_decode_cmp_kernel(pt_ref, q_ref, kn_ref, vn_ref, pet_ref, cwt_ref, ckg_ref, *rest,
                           pps, n_chunks, past_len):
    k_refs = rest[:pps]
    v_refs = rest[pps:2 * pps]
    ocmp_ref, sel_ref = rest[2 * pps:2 * pps + 2]
    ksum, vsum = rest[2 * pps + 2:]
    j = pl.program_id(1)
    nb_past = past_len // CMP_BLOCK
    per_page = PAGE // CMP_BLOCK
    nbp = n_chunks * LANES
    lane = lax.broadcasted_iota(I32, (1, LANES), 1)

    @pl.when(j == 0)
    def _():
        ksum[...] = jnp.zeros_like(ksum)
        vsum[...] = jnp.zeros_like(vsum)

    base = j * (pps * per_page)
    chunk = base // LANES
    lane0 = base % LANES
    for refs, acc in ((k_refs, ksum), (v_refs, vsum)):
        cur = acc[chunk]
        for p in range(pps):
            pg = refs[p][0]
            for t in range(per_page):
                in_blk = lane // CMP_BLOCK == t
                col = jnp.sum(jnp.where(in_blk, pg, 0.0), axis=1, keepdims=True)
                cur = jnp.where(lane == lane0 + per_page * p + t, col, cur)
        acc[chunk] = cur

    @pl.when(j == pl.num_programs(1) - 1)
    def _():
        q = q_ref[0]
        blk = lax.broadcasted_iota(I32, (1, nbp), 1)
        qpos = past_len
        complete = (blk + 1) * CMP_BLOCK - 1 <= qpos
        cur_blk = qpos // CMP_BLOCK
        forced = blk * (blk - cur_blk) == 0
        is_new = blk == nb_past
        kall = jnp.where(is_new, kn_ref[0], jnp.concatenate([ksum[c] for c in range(n_chunks)], axis=1))
        vall = jnp.where(is_new, vn_ref[0], jnp.concatenate([vsum[c] for c in range(n_chunks)], axis=1))
        pe_k = jnp.sum(pet_ref[0], axis=1, keepdims=True)
        pe_v = jnp.sum(pet_ref[1], axis=1, keepdims=True)
        outs = []
        sel_rows = []
        for g in range(B_KV_HEADS):
            rows = slice(HEAD_DIM * g, HEAD_DIM * (g + 1))
            c = _mm(cwt_ref[0], (kall[rows] + pe_k) * (1.0 / CMP_BLOCK), HI)
            ck = c * lax.rsqrt(jnp.mean(c * c, axis=0, keepdims=True) + EPS) * ckg_ref[...]
            cv = _mm(cwt_ref[1], (vall[rows] + pe_v) * (1.0 / CMP_BLOCK), HI)
            qg = jnp.concatenate(
                [q[:, HEAD_DIM * (B_GROUP * g + r):HEAD_DIM * (B_GROUP * g + r + 1)]
                 for r in range(B_GROUP)] * 2, axis=0)
            s = jnp.where(complete, _mm(qg, ck, HI), NEG_BIG)
            e = jnp.where(complete, jnp.exp(s - jnp.max(s, axis=1, keepdims=True)), 0.0)
            p = e / jnp.maximum(jnp.sum(e, axis=1, keepdims=True), 1e-30)
            outs.append(_nt(p, cv, HI)[0:B_GROUP])
            imp = jnp.sum(p[0:B_GROUP], axis=0, keepdims=True)
            work = jnp.where(forced, FORCE_SCORE, jnp.where(complete, imp, -1.0))
            work = jnp.where(blk <= cur_blk, work, -2.0)
            picked = jnp.zeros((1, LANES), I32)
            for t in range(N_SELECT):
                mx = jnp.max(work, axis=1, keepdims=True)
                first = jnp.min(jnp.where(work == mx, blk, nbp), axis=1, keepdims=True)
                picked = jnp.where(lane == t, first, picked)
                work = jnp.where(blk == first, -3.0, work)
            sel_rows.append(picked)
        ocmp_ref[0] = jnp.concatenate(outs, axis=0)
        sel_ref[0] = jnp.concatenate(sel_rows * 4, axis=0)


def _nsa_decode_cmp(page_table, q, k_new_col, v_new_col, pet, cwt, ckg_col, cache_kt, cache_vt,
                    past_len, pps):
    nb, n_pages = page_table.shape
    steps = n_pages // pps
    n_blocks = past_len // CMP_BLOCK + 1
    n_chunks = -(-n_blocks // LANES)
    assert LANES % (pps * (PAGE // CMP_BLOCK)) == 0 and n_blocks >= N_SELECT
    const = lambda shape: pl.BlockSpec(shape, lambda b, j, pt: (0,) * len(shape))

    def page_spec(p):
        return pl.BlockSpec((1, LANES, PAGE), lambda b, j, pt: (pt[b, j * pps + p], 0, 0))

    kern = functools.partial(_nsa_decode_cmp_kernel, pps=pps, n_chunks=n_chunks, past_len=past_len)
    return pl.pallas_call(
        kern,
        out_shape=[jax.ShapeDtypeStruct((nb, 8, HEAD_DIM), F32),
                   jax.ShapeDtypeStruct((nb, 8, LANES), I32)],
        grid_spec=pltpu.PrefetchScalarGridSpec(
            num_scalar_prefetch=1,
            grid=(nb, steps),
            in_specs=[pl.BlockSpec((1, 1, 512), lambda b, j, pt: (b, 0, 0)),
                      pl.BlockSpec((1, LANES, 1), lambda b, j, pt: (b, 0, 0)),
                      pl.BlockSpec((1, LANES, 1), lambda b, j, pt: (b, 0, 0)),
                      const((2, HEAD_DIM, CMP_BLOCK)), const((2, HEAD_DIM, HEAD_DIM)),
                      const((HEAD_DIM, 1))]
                     + [page_spec(p) for p in range(pps)] * 2,
            out_specs=[pl.BlockSpec((1, 8, HEAD_DIM), lambda b, j, pt: (b, 0, 0)),
                       pl.BlockSpec((1, 8, LANES), lambda b, j, pt: (b, 0, 0))],
            scratch_shapes=[pltpu.VMEM((n_chunks, LANES, LANES), F32),
                            pltpu.VMEM((n_chunks, LANES, LANES), F32)]),
        compiler_params=_cparams(("arbitrary", "arbitrary")),
        name="nsa_sample_compressed",
    )(page_table, q, k_new_col, v_new_col, pet, cwt, ckg_col,
      *([cache_kt] * pps), *([cache_vt] * pps))


def _pick_head(x, g):
    return jnp.where(g == 0, x[:, 0:HEAD_DIM], x[:, HEAD_DIM:LANES])


def _nsa_decode_mix_kernel(pt_ref, sel_ref, q_ref, gate_ref, ocmp_ref, kn_ref, vn_ref, wn_ref,
                           kwin_ref, vwin_ref, *rest, past_len, n_sel):
    k_refs = rest[:n_sel]
    v_refs = rest[n_sel:2 * n_sel]
    o_ref = rest[2 * n_sel]
    b = pl.program_id(0)
    g = pl.program_id(1)
    q = q_ref[0]
    gates = gate_ref[0]
    ocmp = ocmp_ref[0, 0]
    nb_past = past_len // CMP_BLOCK
    kwin = kwin_ref[0]
    vwin = vwin_ref[0]
    w_buf = kwin.shape[1]
    wrow = lax.broadcasted_iota(I32, (1, w_buf), 1)
    wlo = max(w_buf - WINDOW + 1, w_buf - past_len, 0)
    wmask = wrow >= wlo
    qg = jnp.concatenate([q[:, HEAD_DIM * r:HEAD_DIM * (r + 1)] for r in range(B_GROUP)] * 2,
                         axis=0)
    kn = _pick_head(kn_ref[0][:, LANES:2 * LANES], g)
    vn = _pick_head(vn_ref[0][:, LANES:2 * LANES], g)
    s_new = jnp.sum(qg * kn, axis=1, keepdims=True)
    ss = []
    halves = []
    for t in range(n_sel):
        blk = sel_ref[b, g * n_sel + t]
        valid = blk < nb_past
        halves.append(blk % (PAGE // CMP_BLOCK))
        s = _mm(qg, _pick_head(k_refs[t][0], halves[t]), HI)
        ss.append(jnp.where(valid, s, NEG_BIG))
    m = s_new
    for s in ss:
        m = jnp.maximum(m, jnp.max(s, axis=1, keepdims=True))
    l = jnp.exp(s_new - m)
    acc = l * vn
    for t in range(n_sel):
        p = jnp.exp(ss[t] - m)
        l = l + jnp.sum(p, axis=1, keepdims=True)
        acc = acc + _nt(p, _pick_head(v_refs[t][0], halves[t]), HI)
    o_slc = acc / l
    kwn = _pick_head(wn_ref[0][:, 0:LANES], g)
    vwn = _pick_head(wn_ref[0][:, LANES:2 * LANES], g)
    sw_new = jnp.sum(qg * kwn, axis=1, keepdims=True)
    sw = jnp.where(wmask, _mm(qg, kwin, HI), NEG_BIG)
    mw = jnp.maximum(sw_new, jnp.max(sw, axis=1, keepdims=True))
    pw = jnp.exp(sw - mw)
    pn = jnp.exp(sw_new - mw)
    lw = pn + jnp.sum(pw, axis=1, keepdims=True)
    o_win = (pn * vwn + _nt(pw, vwin, HI)) / lw
    heads = []
    for r in range(B_GROUP):
        def gate(c, r=r):
            return jnp.where(g == 0, gates[:, 8 * c + r:8 * c + r + 1],
                             gates[:, 8 * c + B_GROUP + r:8 * c + B_GROUP + r + 1])
        heads.append(gate(0) * ocmp[r:r + 1] + gate(1) * o_slc[r:r + 1] + gate(2) * o_win[r:r + 1])
    o_ref[0] = jnp.broadcast_to(jnp.concatenate(heads, axis=1), (8, 256))


def _nsa_decode_mix(page_table, sel2, q, gates, ocmp, nk_new, nv_new, win_new, state_t,
                    cache_kt, cache_vt, past_len, n_sel):
    nb = page_table.shape[0]
    w_buf = state_t.shape[2]
    nb_past = past_len // CMP_BLOCK
    per_page = PAGE // CMP_BLOCK
    per_b = lambda r, w: pl.BlockSpec((1, r, w), lambda b, g, pt, sl: (b, 0, 0))

    def blk_spec(t):
        def imap(b, g, pt, sl):
            blk = jnp.minimum(sl[b, g * n_sel + t], nb_past - 1)
            return (pt[b, blk // per_page], B_KV_HEADS + g, 0)
        return pl.BlockSpec((1, HEAD_DIM, PAGE), imap)

    kern = functools.partial(_nsa_decode_mix_kernel, past_len=past_len, n_sel=n_sel)
    out = pl.pallas_call(
        kern,
        out_shape=jax.ShapeDtypeStruct((nb, 8, 512), F32),
        grid_spec=pltpu.PrefetchScalarGridSpec(
            num_scalar_prefetch=2,
            grid=(nb, B_KV_HEADS),
            in_specs=[pl.BlockSpec((1, 1, 256), lambda b, g, pt, sl: (b, 0, g)),
                      per_b(1, LANES),
                      pl.BlockSpec((1, 1, B_GROUP, HEAD_DIM), lambda b, g, pt, sl: (b, g, 0, 0)),
                      per_b(1, 256), per_b(1, 256), per_b(1, 256),
                      pl.BlockSpec((1, HEAD_DIM, w_buf), lambda b, g, pt, sl: (b, g, 0)),
                      pl.BlockSpec((1, HEAD_DIM, w_buf), lambda b, g, pt, sl: (b, B_KV_HEADS + g, 0))]
                     + [blk_spec(t) for t in range(n_sel)] * 2,
            out_specs=pl.BlockSpec((1, 8, 256), lambda b, g, pt, sl: (b, 0, g))),
        compiler_params=_cparams(("arbitrary", "arbitrary")),
        name="nsa_sample_mix",
    )(page_table, sel2, q, gates, ocmp, nk_new, nv_new, win_new, state_t, state_t,
      *([cache_kt] * n_sel), *([cache_vt] * n_sel))
    return out[:, 0, :]


def _moe(h2_list, ids_list, gates_list, x1_list, gt_list, prm, tt, tm):
    d = h2_list[0].shape[-1]
    n_group = [h.shape[0] * h.shape[1] for h in h2_list]
    n_tok = sum(n_group)
    nt = -(-n_tok // tt)
    ntp = nt * tt
    ids = jnp.concatenate(ids_list, axis=1)[:TOP_K]
    ids = jnp.pad(ids, ((0, 8 - TOP_K), (0, ntp - n_tok)), constant_values=N_EXPERTS)
    ids3 = ids.reshape(8, nt, tt).transpose(1, 0, 2)
    pos3, cnt = _route_positions(ids3, tt)
    counts = cnt[:, 0].astype(I32)
    padded = (counts + tm - 1) // tm * tm
    ends_p = jnp.cumsum(padded)
    starts_p = ends_p - padded
    na = n_tok * TOP_K
    nblk = -(-(na + N_EXPERTS * (tm - 1)) // tm)
    n_rows = nblk * tm
    ids4 = ids3[:, :TOP_K, :]
    valid = ids4 < N_EXPERTS
    e_ar = jnp.arange(N_EXPERTS, dtype=I32)
    start_of = jnp.sum(jnp.where(ids4[..., None] == e_ar, starts_p, 0), axis=-1)
    dest = start_of + pos3[:, :TOP_K, :]
    tok_id = (jnp.arange(nt, dtype=I32)[:, None, None] * tt + jnp.arange(tt, dtype=I32)[None, None, :])
    pad_rank = (tok_id - n_tok) * TOP_K + jnp.arange(TOP_K, dtype=I32)[None, :, None]
    dest_scatter = jnp.where(valid, dest, n_rows + pad_rank).reshape(nt, TOP_K * tt)
    dest_gather = jnp.where(valid, dest, 0).reshape(nt, TOP_K * tt)
    n_trash = (ntp - n_tok) * TOP_K
    blk_start = jnp.arange(nblk, dtype=I32) * tm
    blk_e = jnp.minimum(jnp.sum(jnp.where(ends_p[None, :] <= blk_start[:, None], 1, 0), axis=1),
                        N_EXPERTS - 1).astype(I32)
    n_used = (ends_p[-1:] // tm).astype(I32)

    h_all = jnp.concatenate([h.reshape(-1, d) for h in h2_list], axis=0)
    h_all = jnp.pad(h_all, ((0, ntp - n_tok), (0, 0)))
    xs = _scatter_rows(dest_scatter, h_all, n_rows + max(n_trash, 8), tt)
    y_rows = _expert_matmul(blk_e, n_used, xs, prm["w_gu"], prm["b_gu"], prm["w_dn"], prm["b_dn"],
                            n_rows, tm)

    gates = jnp.concatenate(gates_list, axis=1)[:TOP_K]
    gates = jnp.pad(gates, ((0, 0), (0, ntp - n_tok)))
    gate_t = gates.reshape(TOP_K, nt, tt).transpose(1, 2, 0)
    outs = []
    tok0 = 0
    for x1, gt, n in zip(x1_list, gt_list, n_group):
        bx, sx, _ = x1.shape
        tile0 = tok0 // tt
        if sx % tt:
            padn = tt - sx
            x1p = jnp.pad(x1, ((0, 0), (0, padn), (0, 0)))
            gtp = jnp.pad(gt, ((0, 0), (0, padn), (0, 0)))
            o = _combine(dest_gather, y_rows, x1p, gtp, gate_t[tile0:tile0 + 1], tile0, tt)[:, :sx]
        else:
            o = _combine(dest_gather, y_rows, x1, gt, gate_t[tile0:tile0 + bx * (sx // tt)], tile0, tt)
        outs.append(o)
        tok0 += n
    return outs


def _flatten_rows(a):
    return a.transpose(1, 0, 2).reshape(8, -1)


def kernel(x_prompt, x_sample, c_prompt, c_sample, cache_diff_k, cache_diff_v, cache_nsa_k, cache_nsa_v, state_win_kv, page_table, attn_norm_g, ffn_norm_g, ada_w, ada_b, w_in, w_out, diff_q_norm_g, diff_k_norm_g, diff_lambda, diff_sub_norm_g, nsa_q_norm_g, nsa_k_norm_g, nsa_ck_norm_g, nsa_cmp_pe, nsa_cmp_w, router_w, router_b, expert_w_gu, expert_b_gu, expert_w_down, expert_b_down):
    depth = w_in.shape[0]
    assert depth == 1, "single-layer trunk"
    bp, sp, d = x_prompt.shape
    bs, ss, _ = x_sample.shape
    assert ss == 1
    n_pages = page_table.shape[1]
    past_len = n_pages * PAGE
    w_buf = state_win_kv.shape[2]
    n_pool = cache_diff_k.shape[1]
    l = 0
    lam_init = 0.8 - 0.6 * math.exp(-0.3 * l)
    d_ff = expert_w_down.shape[2]

    prm = {
        "attn_g": attn_norm_g[l].reshape(1, d),
        "ffn_g": ffn_norm_g[l].reshape(1, d),
        "w_in": jnp.pad(w_in[l], ((0, 0), (0, IN_PAD - w_in.shape[2]))).astype(BF16),
        "gd": jnp.stack([jnp.tile(diff_q_norm_g[l], 16), jnp.tile(diff_k_norm_g[l], 16)]),
        "gn": jnp.tile(nsa_q_norm_g[l], 8).reshape(1, 512),
        "gk": jnp.concatenate([jnp.tile(nsa_k_norm_g[l, 0], 2), jnp.tile(nsa_k_norm_g[l, 1], 2)]).reshape(1, 256),
        "m32": _group_mean_matrix(A_DQK, BF16),
        "m64": _group_mean_matrix(HEAD_DIM, BF16),
        "m32_f32": _group_mean_matrix(A_DQK, F32),
        "m64_f32": _group_mean_matrix(HEAD_DIM, F32),
        "w_out": w_out[l].astype(BF16),
        "router_wt": router_w[l].T,
        "router_b": router_b[l].reshape(N_EXPERTS, 1),
        "w_gu": expert_w_gu[l].astype(BF16),
        "b_gu": expert_b_gu[l].reshape(N_EXPERTS, 1, 2 * d_ff),
        "w_dn": expert_w_down[l].astype(BF16),
        "b_dn": expert_b_down[l].reshape(N_EXPERTS, 1, d),
    }
    lam_p = diff_lambda[l]
    sub_g2 = jnp.tile(diff_sub_norm_g[l], 2).reshape(1, LANES)
    sub_g8 = jnp.tile(diff_sub_norm_g[l], 8).reshape(1, 512)
    pe = nsa_cmp_pe[l]
    cw = nsa_cmp_w[l]
    ckg = nsa_ck_norm_g[l].reshape(1, HEAD_DIM)

    n_c = bp + bs
    n_cp = -(-n_c // 8) * 8
    c_all = jnp.pad(jnp.concatenate([c_prompt, c_sample], axis=0), ((0, n_cp - n_c), (0, 0)))
    mod = _modulation(c_all, ada_w[l], ada_b[l])
    mod_p = mod[:bp].reshape(bp, 1, 6, d)
    mod_s = mod[bp:n_c].reshape(1, bs, 6, d)
    sh1p, sc1p, gt1p, sh2p, sc2p, gt2p = [mod_p[:, :, i] for i in range(6)]
    sh1s, sc1s, gt1s, sh2s, sc2s, gt2s = [mod_s[:, :, i] for i in range(6)]

    ts = min(512, sp)
    pos_p = jnp.arange(sp, dtype=I32)
    (qa, dk, dkb, dv, dvb, qn, nk, nv, win, kb, vb, gate) = _in_projection(
        x_prompt, sh1p, sc1p, pos_p, prm, ts)
    tq = min(256, sp)
    o_a = _diff_attention_prompt(qa, dkb, dvb, lam_p, sub_g2, lam_init, tq)
    tqn = min(256, sp)
    o_b = _nsa_prompt(qn, gate, nk, nv, kb, vb, pe, cw, ckg, tqn, min(512, sp))
    x1p, h2p, idsp, gatesp = _out_projection(o_a, o_b, x_prompt, gt1p, sh2p, sc2p, prm, ts)

    xs_ = x_sample.reshape(1, bs, d)
    pos_s = jnp.full((bs,), past_len, I32)
    (qa_s, dk_s, _, dv_s, _, qn_s, nk_s, nv_s, win_s, _, _, gate_s) = _in_projection(
        xs_, sh1s, sc1s, pos_s, prm, bs, sample=True)
    ckt = jnp.transpose(cache_diff_k[l], (0, 2, 3, 1)).reshape(n_pool, 512, PAGE)
    cvt = jnp.transpose(cache_diff_v[l], (0, 2, 3, 1)).reshape(n_pool, 512, PAGE)
    nkt = jnp.transpose(cache_nsa_k[l], (0, 2, 3, 4, 1)).reshape(n_pool, 256, PAGE)
    nvt = jnp.transpose(cache_nsa_v[l], (0, 2, 3, 4, 1)).reshape(n_pool, 256, PAGE)
    state = state_win_kv[l].reshape(bs, w_buf, 256)
    state_t = jnp.transpose(state_win_kv[l], (0, 2, 3, 4, 1)).reshape(bs, 256, w_buf)
    pps = 8 if n_pages % 8 == 0 else 1
    as3 = lambda a: a.reshape(bs, 1, a.shape[-1])
    col = lambda a: a.reshape(bs, a.shape[-1], 1)
    o_a_s = _diff_attention_sample(page_table, lam_p, col(qa_s[0]), col(dk_s[0]), col(dv_s[0]),
                                   diff_sub_norm_g[l].reshape(HEAD_DIM, 1), ckt, cvt, lam_init, pps)
    ocmp_s, sel_s = _nsa_decode_cmp(page_table, as3(qn_s[0]), col(nk_s[0][:, :LANES]),
                                    col(nv_s[0][:, :LANES]), jnp.transpose(pe, (0, 2, 1)),
                                    jnp.transpose(cw, (0, 2, 1)), ckg.reshape(HEAD_DIM, 1),
                                    nkt, nvt, past_len, pps)
    n_sel = min(N_SELECT, past_len // CMP_BLOCK + 1)
    sel2 = jnp.concatenate([sel_s[:, 0, :n_sel], sel_s[:, 1, :n_sel]], axis=1)
    o_b_s = _nsa_decode_mix(page_table, sel2, as3(qn_s[0]), as3(gate_s[0]),
                            ocmp_s.reshape(bs, B_KV_HEADS, B_GROUP, HEAD_DIM), as3(nk_s[0]), as3(nv_s[0]),
                            as3(win_s[0]), state_t, nkt, nvt, past_len, n_sel)
    x1s, h2s, idss, gatess = _out_projection(o_a_s.astype(BF16).reshape(1, bs, 512),
                                             o_b_s.astype(BF16).reshape(1, bs, 512),
                                             xs_, gt1s, sh2s, sc2s, prm, bs)

    y_p, y_s = _moe([h2p, h2s], [_flatten_rows(idsp), _flatten_rows(idss)],
                    [_flatten_rows(gatesp), _flatten_rows(gatess)], [x1p, x1s], [gt2p, gt2s],
                    prm, 256, 256)

    p_win = win[:, sp - w_buf:] if sp >= w_buf else jnp.pad(win, ((0, 0), (w_buf - sp, 0), (0, 0)))
    s_win = jnp.concatenate([state, win_s[0][:, None, :]], axis=1)[:, 1:]
    return (y_p, y_s.reshape(bs, 1, d),
            dk.reshape(1, bp, sp, A_HEADS, 2 * A_DQK), dv.reshape(1, bp, sp, A_HEADS, HEAD_DIM),
            nk.reshape(1, bp, sp, 2, B_KV_HEADS, HEAD_DIM), nv.reshape(1, bp, sp, 2, B_KV_HEADS, HEAD_DIM),
            p_win.reshape(1, bp, w_buf, 2, B_KV_HEADS, HEAD_DIM),
            dk_s.reshape(1, bs, 1, A_HEADS, 2 * A_DQK), dv_s.reshape(1, bs, 1, A_HEADS, HEAD_DIM),
            nk_s.reshape(1, bs, 1, 2, B_KV_HEADS, HEAD_DIM), nv_s.reshape(1, bs, 1, 2, B_KV_HEADS, HEAD_DIM),
            s_win.reshape(1, bs, w_buf, 2, B_KV_HEADS, HEAD_DIM))
```

```python
import functools
import math

import jax
import jax.numpy as jnp
from jax import lax
from jax.experimental import pallas as pl
from jax.experimental.pallas import tpu as pltpu

F32 = jnp.float32
BF16 = jnp.bfloat16
I32 = jnp.int32
HI = lax.Precision.HIGHEST

HEAD_DIM = 64
A_HEADS = 8
A_DQK = 32
B_HEADS = 8
B_KV_HEADS = 2
B_GROUP = 4
CMP_BLOCK = 64
N_SELECT = 16
WINDOW = 512
ROPE_THETA = 500000.0
ROPE_FRACTION = 4
N_EXPERTS = 32
TOP_K = 4
SWIGLU_LIMIT = 7.0
SWIGLU_ALPHA = 1.702
EPS = 1e-6
NEG_BIG = -1e30
FORCE_SCORE = 1e4
PAGE = 128

A_Q = 512
IN_PAD = 2944
LANES = 128
STRIP = 32
LOG2E = 1.4426950408889634
VMEM_LIMIT = 56 * 1024 * 1024


def _cparams(sem, vmem=VMEM_LIMIT):
    return pltpu.CompilerParams(dimension_semantics=sem, vmem_limit_bytes=vmem)


def _nt(a, b, precision=None):
    return lax.dot_general(a, b, (((1,), (1,)), ((), ())),
                           preferred_element_type=F32, precision=precision)


def _mm(a, b, precision=None):
    return jnp.dot(a, b, preferred_element_type=F32, precision=precision)


def _mod_kernel(c_ref, w_ref, b_ref, o_ref):
    c = c_ref[...]
    s = c / (1.0 + jnp.exp(-c))
    o_ref[...] = _mm(s.astype(BF16), w_ref[...].astype(BF16)) + b_ref[...]


def _modulation(c_all, ada_w, ada_b):
    n, d = c_all.shape
    width = ada_w.shape[1]
    tn = 1024
    return pl.pallas_call(
        _mod_kernel,
        out_shape=jax.ShapeDtypeStruct((n, width), F32),
        grid=(width // tn,),
        in_specs=[pl.BlockSpec((n, d), lambda j: (0, 0)),
                  pl.BlockSpec((d, tn), lambda j: (0, j)),
                  pl.BlockSpec((1, tn), lambda j: (0, j))],
        out_specs=pl.BlockSpec((n, tn), lambda j: (0, j)),
        compiler_params=_cparams(("arbitrary",)),
        name="adaln_mod",
    )(c_all, ada_w, ada_b.reshape(1, width))


def _group_norm(seg, gmat, gvec):
    out = []
    for j in range(seg.shape[1] // 256):
        c = seg[:, 256 * j:256 * (j + 1)]
        if gmat.dtype == F32:
            ms = _mm(c * c, gmat, HI)
        else:
            ms = _mm((c * c).astype(BF16), gmat)
        out.append(c * lax.rsqrt(ms + EPS) * gvec[:, 256 * j:256 * (j + 1)])
    return out[0] if len(out) == 1 else jnp.concatenate(out, axis=1)


def _rope_lanes(seg, tab_ref, half):
    w = seg.shape[1]
    rep = w // LANES
    cos = jnp.concatenate([tab_ref[0]] * rep, axis=1)
    s_up = jnp.concatenate([tab_ref[1]] * rep, axis=1)
    s_dn = jnp.concatenate([tab_ref[2]] * rep, axis=1)
    return (seg * cos + pltpu.roll(seg, w - half, 1) * s_up
            + pltpu.roll(seg, half, 1) * s_dn)


def _inproj_kernel(x_ref, sh_ref, sc_ref, g_ref, w_ref, gd_ref, gn_ref, gk_ref,
                   m32_ref, m64_ref, td_ref, tn_ref,
                   qa_ref, dk_ref, dkb_ref, dv_ref, dvb_ref, qn_ref, nk_ref, nv_ref,
                   win_ref, kb_ref, vb_ref, gate_ref):
    x = x_ref[0]
    ms = jnp.mean(x * x, axis=-1, keepdims=True)
    h = x * lax.rsqrt(ms + EPS) * g_ref[...]
    h = h * (1.0 + sc_ref[0]) + sh_ref[0]
    proj = _mm(h.astype(BF16), w_ref[...])
    m32 = m32_ref[...]
    m64 = m64_ref[...]
    gd = gd_ref[...]

    qa = _rope_lanes(_group_norm(proj[:, 0:512], m32, gd[0:1]), td_ref, 4)
    q_scale = A_DQK ** -0.5 * LOG2E if qa_ref.dtype == BF16 else 1.0
    qa_ref[0] = (qa * q_scale).astype(qa_ref.dtype)
    ka = _rope_lanes(_group_norm(proj[:, 512:1024], m32, gd[1:2]), td_ref, 4)
    dk_ref[0] = ka
    dkb_ref[0] = ka.astype(BF16)
    va = proj[:, 1024:1536]
    dv_ref[0] = va
    dvb_ref[0] = va.astype(BF16)
    qn = _rope_lanes(_group_norm(proj[:, 1536:2048], m64, gn_ref[...]), tn_ref, 8)
    qn_ref[0] = qn * (HEAD_DIM ** -0.5)
    k_cmp = _rope_lanes(proj[:, 2048:2176], tn_ref, 8)
    k_sw = _rope_lanes(_group_norm(proj[:, 2176:2432], m64, gk_ref[...]), tn_ref, 8)
    vb = proj[:, 2432:2816]
    nk_ref[0] = jnp.concatenate([k_cmp, k_sw[:, 0:128]], axis=1)
    nv_ref[0] = vb[:, 0:256]
    win_ref[0] = jnp.concatenate([k_sw[:, 128:256], vb[:, 256:384]], axis=1)
    kb_ref[0] = jnp.concatenate([k_cmp, k_sw], axis=1).astype(BF16)
    vb_ref[0] = vb.astype(BF16)
    gl = proj[:, 2816:2944]
    gate_ref[0] = 1.0 / (1.0 + jnp.exp(-gl))


def _rope_tables(pos, group, half):
    inv = ROPE_THETA ** (-jnp.arange(half, dtype=F32) / half)
    ang = pos.astype(F32)[:, None] * inv[None, :]
    cos, sin = jnp.cos(ang), jnp.sin(ang)
    n = pos.shape[0]
    pad = group - 2 * half
    c = jnp.concatenate([cos, cos, jnp.ones((n, pad), F32)], axis=1)
    up = jnp.concatenate([-sin, jnp.zeros((n, half + pad), F32)], axis=1)
    dn = jnp.concatenate([jnp.zeros((n, half), F32), sin, jnp.zeros((n, pad), F32)], axis=1)
    rep = LANES // group
    return jnp.stack([jnp.tile(c, (1, rep)), jnp.tile(up, (1, rep)), jnp.tile(dn, (1, rep))])


def _group_mean_matrix(group, dtype):
    i = jnp.arange(256)
    return jnp.where((i[:, None] // group) == (i[None, :] // group), 1.0 / group, 0.0).astype(dtype)


def _in_projection(x, sh, sc, pos, prm, ts, sample=False):
    bx, sx, d = x.shape
    r = sh.shape[1]
    rb = 1 if r == 1 else ts
    td = _rope_tables(pos, A_DQK, A_DQK // ROPE_FRACTION // 2)
    tn = _rope_tables(pos, HEAD_DIM, HEAD_DIM // ROPE_FRACTION // 2)

    def tok(width, dtype):
        return (jax.ShapeDtypeStruct((bx, sx, width), dtype),
                pl.BlockSpec((1, ts, width), lambda s, b: (b, s, 0)))

    sfx = "_f32" if sample else ""
    outs = [tok(512, F32 if sample else BF16), tok(512, F32), tok(512, BF16), tok(512, F32), tok(512, BF16),
            tok(512, F32), tok(256, F32), tok(256, F32), tok(256, F32), tok(384, BF16),
            tok(384, BF16), tok(128, F32)]
    const = lambda shape: pl.BlockSpec(shape, lambda s, b: (0,) * len(shape))
    mod_spec = pl.BlockSpec((1, rb, d), (lambda s, b: (b, 0, 0)) if r == 1 else (lambda s, b: (b, s, 0)))
    return pl.pallas_call(
        _inproj_kernel,
        out_shape=[o[0] for o in outs],
        grid=(sx // ts, bx),
        in_specs=[pl.BlockSpec((1, ts, d), lambda s, b: (b, s, 0)), mod_spec, mod_spec,
                  const((1, d)), const((d, IN_PAD)), const((2, 512)), const((1, 512)),
                  const((1, 256)), const((256, 256)), const((256, 256)),
                  pl.BlockSpec((3, ts, LANES), lambda s, b: (0, s, 0)),
                  pl.BlockSpec((3, ts, LANES), lambda s, b: (0, s, 0))],
        out_specs=[o[1] for o in outs],
        compiler_params=_cparams(("arbitrary", "arbitrary")),
        name="in_projection",
    )(x, sh, sc, prm["attn_g"], prm["w_in"], prm["gd"], prm["gn"], prm["gk"],
      prm["m32" + sfx], prm["m64" + sfx], td, tn)


def _diff_lambda(lam_ref, lam_init):
    lp = lam_ref[...]
    a = jnp.sum(lp[0:1] * lp[1:2], axis=1, keepdims=True)
    b = jnp.sum(lp[2:3] * lp[3:4], axis=1, keepdims=True)
    return jnp.exp(a) - jnp.exp(b) + lam_init


def _head_rms(o, sg):
    lane = lax.broadcasted_iota(I32, (1, LANES), 1)
    sq = o * o
    s0 = jnp.sum(jnp.where(lane < HEAD_DIM, sq, 0.0), axis=1, keepdims=True)
    s1 = jnp.sum(jnp.where(lane >= HEAD_DIM, sq, 0.0), axis=1, keepdims=True)
    ms = jnp.where(lane < HEAD_DIM, s0, s1) * (1.0 / HEAD_DIM)
    return o * lax.rsqrt(ms + EPS) * sg


def _diff_attn_kernel(lam_ref, q_ref, k_ref, v_ref, sg_ref, o_ref, *, tq, tk, lam_init):
    qi = pl.program_id(2)
    q = q_ref[0]
    lane = lax.broadcasted_iota(I32, (1, LANES), 1)
    lam = _diff_lambda(lam_ref, lam_init)
    zero = jnp.zeros_like(q)
    qm = [jnp.where((lane >= A_DQK * i) & (lane < A_DQK * (i + 1)), q, zero) for i in range(4)]
    first_head = lane < HEAD_DIM
    one = jnp.ones((tk, LANES), BF16)
    q0 = qi * tq

    def chunk(j, carry, masked):
        ms, accs = carry
        start = pl.multiple_of(j * tk, tk)
        kc = k_ref[0, pl.ds(start, tk), :]
        vc = v_ref[0, pl.ds(start, tk), :]
        vaug = (jnp.where(first_head, vc, one), jnp.where(first_head, one, vc))
        ss = [_nt(qm[i], kc) for i in range(4)]
        new_ms = [[] for _ in range(4)]
        alphas = [[] for _ in range(4)]
        ps = [[] for _ in range(4)]
        for r in range(tq // STRIP):
            rows = slice(r * STRIP, (r + 1) * STRIP)
            if masked:
                row = q0 + r * STRIP + lax.broadcasted_iota(I32, (STRIP, tk), 0)
                causal = start + lax.broadcasted_iota(I32, (STRIP, tk), 1) <= row
            for i in range(4):
                s = ss[i][rows]
                if masked:
                    s = jnp.where(causal, s, NEG_BIG)
                m_old = ms[i][rows]
                m_new = jnp.maximum(m_old, jnp.max(s, axis=1, keepdims=True))
                ps[i].append(jnp.exp2(s - jnp.concatenate([m_new] * (tk // LANES), axis=1)).astype(BF16))
                alphas[i].append(jnp.exp2(m_old - m_new))
                new_ms[i].append(m_new)
        new_accs = []
        for i in range(4):
            pv = _mm(jnp.concatenate(ps[i], axis=0), vaug[i // 2])
            new_accs.append(jnp.concatenate(alphas[i], axis=0) * accs[i] + pv)
        return tuple(jnp.concatenate(m, axis=0) for m in new_ms), tuple(new_accs)

    init = (tuple(jnp.full((tq, LANES), NEG_BIG, F32) for _ in range(4)),
            tuple(jnp.zeros((tq, LANES), F32) for _ in range(4)))
    n_full = q0 // tk
    carry = lax.fori_loop(0, n_full, lambda j, c: chunk(j, c, False), init)
    _, accs = chunk(n_full, carry, True)
    outs = [a / pltpu.roll(a, HEAD_DIM, 1) for a in accs]
    o0 = jnp.where(first_head, outs[0], outs[2])
    o1 = jnp.where(first_head, outs[1], outs[3])
    o = o0 - lam * o1
    o_ref[0] = (_head_rms(o, sg_ref[...]) * (1.0 - lam_init)).astype(BF16)


def _diff_attention_prompt(qa, dkb, dvb, lam_p, sub_g2, lam_init, tq, tk):
    b, s, _ = qa.shape
    assert tk % tq == 0 and s % tk == 0
    kern = functools.partial(_diff_attn_kernel, tq=tq, tk=tk, lam_init=lam_init)
    return pl.pallas_call(
        kern,
        out_shape=jax.ShapeDtypeStruct((b, s, 512), BF16),
        grid=(b, A_HEADS // 2, s // tq),
        in_specs=[pl.BlockSpec((4, A_DQK), lambda b_, p, i: (0, 0)),
                  pl.BlockSpec((1, tq, LANES), lambda b_, p, i: (b_, i, p)),
                  pl.BlockSpec((1, s, LANES), lambda b_, p, i: (b_, 0, p)),
                  pl.BlockSpec((1, s, LANES), lambda b_, p, i: (b_, 0, p)),
                  pl.BlockSpec((1, LANES), lambda b_, p, i: (0, 0))],
        out_specs=pl.BlockSpec((1, tq, LANES), lambda b_, p, i: (b_, i, p)),
        compiler_params=_cparams(("arbitrary", "arbitrary", "arbitrary")),
        name="diff_attention_prompt",
    )(lam_p, qa, dkb, dvb, sub_g2)


def _top_select_t(score_t, n_sel):
    nb = score_t.shape[0]
    blk = lax.broadcasted_iota(I32, (nb, 1), 0)
    cnt = jnp.zeros(score_t.shape, F32)
    for i in range(nb):
        row = score_t[i:i + 1, :]
        gt = jnp.where(row > score_t, 1.0, 0.0)
        eq = jnp.where(row == score_t, 1.0, 0.0)
        cnt = cnt + gt + jnp.where(blk > i, eq, 0.0)
    return jnp.where(cnt < n_sel, 1.0, 0.0)


def _strip_softmax(s, m_old, mask_fn):
    m_new, alpha, ps = [], [], []
    for r in range(s.shape[0] // STRIP):
        rows = slice(r * STRIP, (r + 1) * STRIP)
        sr = s[rows] if mask_fn is None else mask_fn(s[rows], rows)
        mn = jnp.broadcast_to(jnp.max(sr, axis=1, keepdims=True), (STRIP, LANES))
        if m_old is not None:
            mn = jnp.maximum(m_old[rows], mn)
            alpha.append(jnp.exp2(m_old[rows] - mn))
        ps.append(jnp.exp2(sr - jnp.concatenate([mn] * (s.shape[1] // LANES), axis=1)).astype(BF16))
        m_new.append(mn)
    cat = lambda xs: jnp.concatenate(xs, axis=0)
    return cat(m_new), (cat(alpha) if alpha else None), cat(ps)


def _compress(kc, pe_sum, w, nb):
    mean = (jnp.sum(kc.reshape(nb, CMP_BLOCK, HEAD_DIM), axis=1) + pe_sum) * (1.0 / CMP_BLOCK)
    return _mm(mean, w, HI)


def _nsa_prompt_kernel(q_ref, gate_ref, kcmp_ref, vcmp_ref, kslc_ref, vslc_ref, kwin_ref,
                       vwin_ref, pe_ref, cw_ref, ckg_ref, o_ref,
                       kaug, vs, kw, vw, ck, cv, *, tq, tk, seq):
    qi = pl.program_id(1)
    nb = seq // CMP_BLOCK
    n_sel = min(N_SELECT, nb)
    wk = WINDOW + tq

    @pl.when(qi == 0)
    def _():
        rowblk = lax.broadcasted_iota(I32, (seq, HEAD_DIM), 0) // CMP_BLOCK
        colblk = lax.broadcasted_iota(I32, (seq, HEAD_DIM), 1)
        onehot = jnp.where(rowblk == colblk, 1.0, 0.0).astype(BF16)
        pe_k = jnp.sum(pe_ref[0], axis=0, keepdims=True)
        pe_v = jnp.sum(pe_ref[1], axis=0, keepdims=True)
        ones = jnp.ones((seq, HEAD_DIM), BF16)
        for g in range(B_KV_HEADS):
            lo, hi = HEAD_DIM * g, HEAD_DIM * (g + 1)
            kaug[g] = jnp.concatenate([kslc_ref[0][:, lo:hi], onehot], axis=1)
            vs[g] = jnp.concatenate([vslc_ref[0][:, lo:hi], ones], axis=1)
            kw[g] = kwin_ref[0][:, lo:hi]
            vw[g] = jnp.concatenate([vwin_ref[0][:, lo:hi], ones], axis=1)
            c = _compress(kcmp_ref[0][:, lo:hi], pe_k, cw_ref[0], nb)
            ms = jnp.mean(c * c, axis=-1, keepdims=True)
            ck[g] = c * lax.rsqrt(ms + EPS) * ckg_ref[...]
            cv[g] = _compress(vcmp_ref[0][:, lo:hi], pe_v, cw_ref[1], nb)

    q = q_ref[0]
    gates = gate_ref[0]
    q0 = qi * tq
    qpos = q0 + lax.broadcasted_iota(I32, (tq, 1), 0)
    blk = lax.broadcasted_iota(I32, (1, nb), 1)
    complete = (blk + 1) * CMP_BLOCK - 1 <= qpos
    cur = qpos // CMP_BLOCK
    forced = blk * (blk - cur) == 0
    qpos4 = jnp.concatenate([qpos] * B_GROUP, axis=0)
    n_full = q0 // tk
    wstart = pl.multiple_of(jnp.maximum(q0 - WINDOW, 0), tq)
    kpos_w = wstart + lax.broadcasted_iota(I32, (1, wk), 1)
    qpos_t = q0 + lax.broadcasted_iota(I32, (1, tq), 1)
    blk_t = lax.broadcasted_iota(I32, (nb, 1), 0)
    complete_t = (blk_t + 1) * CMP_BLOCK - 1 <= qpos_t
    forced_t = blk_t * (blk_t - qpos_t // CMP_BLOCK) == 0
    eye = jnp.where(lax.broadcasted_iota(I32, (tq, tq), 0) == lax.broadcasted_iota(I32, (tq, tq), 1),
                    1.0, 0.0).astype(BF16)
    o_cmp_all, q4_all, qaug_all = [], [], []

    for g in range(B_KV_HEADS):
        qh = [q[:, HEAD_DIM * (B_GROUP * g + r):HEAD_DIM * (B_GROUP * g + r + 1)]
              for r in range(B_GROUP)]
        ckg = ck[g]
        cvg = cv[g]
        o_cmp = []
        imp_t = jnp.zeros((nb, tq), F32)
        for r in range(B_GROUP):
            s = jnp.where(complete, _nt(qh[r], ckg, HI), NEG_BIG)
            e = jnp.where(complete, jnp.exp(s - jnp.max(s, axis=1, keepdims=True)), 0.0)
            p = e / jnp.maximum(jnp.sum(e, axis=1, keepdims=True), 1e-30)
            o_cmp.append(_mm(p, cvg, HI))
            st = jnp.where(complete_t, _nt(ckg, qh[r], HI), NEG_BIG)
            et = jnp.where(complete_t, jnp.exp(st - jnp.max(st, axis=0, keepdims=True)), 0.0)
            imp_t = imp_t + et / jnp.maximum(jnp.sum(et, axis=0, keepdims=True), 1e-30)
        score_t = jnp.where(forced_t, FORCE_SCORE, jnp.where(complete_t, imp_t, -1.0))
        sel_t = _top_select_t(score_t, n_sel)
        sel = _nt(eye, sel_t.astype(BF16))
        bias = jnp.where(blk <= cur, jnp.where(sel > 0.5, 0.0, NEG_BIG), NEG_BIG).astype(BF16)
        if nb < HEAD_DIM:
            bias = jnp.concatenate([bias, jnp.zeros((tq, HEAD_DIM - nb), BF16)], axis=1)

        q4 = jnp.concatenate([(qh[r] * LOG2E).astype(BF16) for r in range(B_GROUP)], axis=0)
        o_cmp_all.append(o_cmp)
        q4_all.append(q4)
        qaug_all.append(jnp.concatenate([q4, jnp.concatenate([bias] * B_GROUP, axis=0)], axis=1))

    def chunk(j, carry, masked):
        start = pl.multiple_of(j * tk, tk)
        mask_fn = None
        if masked:
            kpos = start + lax.broadcasted_iota(I32, (1, tk), 1)
            mask_fn = lambda sr, rows: jnp.where(kpos <= qpos4[rows], sr, NEG_BIG)
        out = []
        for g in range(B_KV_HEADS):
            m, acc = carry[g]
            s = _nt(qaug_all[g], kaug[g, pl.ds(start, tk), :])
            m_new, alpha, p = _strip_softmax(s, m, mask_fn)
            out.append((m_new, alpha * acc + _mm(p, vs[g, pl.ds(start, tk), :])))
        return tuple(out)

    init = tuple((jnp.full((B_GROUP * tq, LANES), NEG_BIG, F32), jnp.zeros((B_GROUP * tq, LANES), F32))
                 for _ in range(B_KV_HEADS))
    carry = lax.fori_loop(0, n_full, lambda j, c: chunk(j, c, False), init)
    slc = chunk(n_full, carry, True)

    def in_window(sr, rows):
        dlt = qpos4[rows] - kpos_w
        return jnp.where(dlt >= 0, jnp.where(dlt < WINDOW, sr, NEG_BIG), NEG_BIG)

    heads_out = []
    for g in range(B_KV_HEADS):
        acc = slc[g][1]
        o_slc = acc[:, 0:HEAD_DIM] / acc[:, HEAD_DIM:LANES]
        sw = _nt(q4_all[g], kw[g, pl.ds(wstart, wk), :])
        _, _, pw = _strip_softmax(sw, None, in_window)
        accw = _mm(pw, vw[g, pl.ds(wstart, wk), :])
        o_win = accw[:, 0:HEAD_DIM] / accw[:, HEAD_DIM:LANES]
        for r in range(B_GROUP):
            h = B_GROUP * g + r
            rows = slice(r * tq, (r + 1) * tq)
            heads_out.append(gates[:, h:h + 1] * o_cmp_all[g][r]
                             + gates[:, 8 + h:9 + h] * o_slc[rows]
                             + gates[:, 16 + h:17 + h] * o_win[rows])
    o_ref[0] = jnp.concatenate(heads_out, axis=1).astype(BF16)


def _nsa_prompt(qn, gates, nk, nv, kb, vb, pe, cw, ckg, tq, tk):
    b, s, _ = qn.shape
    nb = s // CMP_BLOCK
    kern = functools.partial(_nsa_prompt_kernel, tq=tq, tk=tk, seq=s)
    full = lambda lane_blk: pl.BlockSpec((1, s, LANES), lambda b_, i: (b_, 0, lane_blk))
    const = lambda shape: pl.BlockSpec(shape, lambda b_, i: (0,) * len(shape))
    return pl.pallas_call(
        kern,
        out_shape=jax.ShapeDtypeStruct((b, s, 512), BF16),
        grid=(b, s // tq),
        in_specs=[pl.BlockSpec((1, tq, 512), lambda b_, i: (b_, i, 0)),
                  pl.BlockSpec((1, tq, LANES), lambda b_, i: (b_, i, 0)),
                  full(0), full(0), full(1), full(1), full(2), full(2),
                  const((2, CMP_BLOCK, HEAD_DIM)), const((2, HEAD_DIM, HEAD_DIM)),
                  const((1, HEAD_DIM))],
        out_specs=pl.BlockSpec((1, tq, 512), lambda b_, i: (b_, i, 0)),
        scratch_shapes=[pltpu.VMEM((2, s, LANES), BF16), pltpu.VMEM((2, s, LANES), BF16),
                        pltpu.VMEM((2, s, HEAD_DIM), BF16), pltpu.VMEM((2, s, LANES), BF16),
                        pltpu.VMEM((2, nb, HEAD_DIM), F32), pltpu.VMEM((2, nb, HEAD_DIM), F32)],
        compiler_params=_cparams(("arbitrary", "arbitrary")),
        name="nsa_prompt",
    )(qn, gates, nk, nv, kb, vb, kb, vb, pe, cw, ckg)


def _outproj_kernel(oa_ref, ob_ref, x_ref, gt_ref, sh_ref, sc_ref, g_ref, w_ref, rw_ref, rb_ref,
                    x1_ref, h2_ref, ids_ref, gates_ref):
    y = _mm(oa_ref[0], w_ref[0:512, :]) + _mm(ob_ref[0], w_ref[512:1024, :])
    x1 = x_ref[0] + gt_ref[0] * y
    x1_ref[0] = x1
    ms = jnp.mean(x1 * x1, axis=-1, keepdims=True)
    h2 = x1 * lax.rsqrt(ms + EPS) * g_ref[...]
    h2 = h2 * (1.0 + sc_ref[0]) + sh_ref[0]
    h2_ref[0] = h2
    logits = _nt(rw_ref[...].astype(BF16), h2.astype(BF16)) + rb_ref[...]
    eidx = lax.broadcasted_iota(I32, logits.shape, 0)
    work = logits
    vals, ids = [], []
    for _ in range(TOP_K):
        m = jnp.max(work, axis=0, keepdims=True)
        idx = jnp.min(jnp.where(work == m, eidx, N_EXPERTS), axis=0, keepdims=True)
        vals.append(m)
        ids.append(idx)
        work = jnp.where(eidx == idx, -3e38, work)
    es = [jnp.exp(v - vals[0]) for v in vals]
    tot = es[0] + es[1] + es[2] + es[3]
    ids_ref[0] = jnp.concatenate(ids + ids, axis=0)
    gates_ref[0] = jnp.concatenate([e / tot for e in es] * 2, axis=0)


def _out_projection(oa, ob, x, gt, sh, sc, prm, ts):
    bx, sx, d = x.shape
    r = gt.shape[1]
    rb = 1 if r == 1 else ts
    mod_spec = pl.BlockSpec((1, rb, d), (lambda b, s: (b, 0, 0)) if r == 1 else (lambda b, s: (b, s, 0)))
    const = lambda shape: pl.BlockSpec(shape, lambda b, s: (0,) * len(shape))
    tokspec = lambda w: pl.BlockSpec((1, ts, w), lambda b, s: (b, s, 0))
    nt = sx // ts
    return pl.pallas_call(
        _outproj_kernel,
        out_shape=[jax.ShapeDtypeStruct((bx, sx, d), F32), jax.ShapeDtypeStruct((bx, sx, d), F32),
                   jax.ShapeDtypeStruct((bx * nt, 8, ts), I32),
                   jax.ShapeDtypeStruct((bx * nt, 8, ts), F32)],
        grid=(bx, nt),
        in_specs=[tokspec(512), tokspec(512), tokspec(d), mod_spec, mod_spec, mod_spec,
                  const((1, d)), const((d, d)), const((N_EXPERTS, d)), const((N_EXPERTS, 1))],
        out_specs=[tokspec(d), tokspec(d),
                   pl.BlockSpec((1, 8, ts), lambda b, s: (b * nt + s, 0, 0)),
                   pl.BlockSpec((1, 8, ts), lambda b, s: (b * nt + s, 0, 0))],
        compiler_params=_cparams(("arbitrary", "arbitrary")),
        name="out_projection",
    )(oa, ob, x, gt, sh, sc, prm["ffn_g"], prm["w_out"], prm["router_wt"], prm["router_b"])


def _route_kernel(ids_ref, pos_ref, cnt_ref, carry, *, tt):
    @pl.when(pl.program_id(0) == 0)
    def _():
        carry[...] = jnp.zeros_like(carry)

    ids = ids_ref[0]
    e_iota = lax.broadcasted_iota(I32, (N_EXPERTS, tt), 0)
    hits = [ids[k:k + 1, :] == e_iota for k in range(TOP_K)]
    oh = jnp.zeros((N_EXPERTS, tt), F32)
    for k in range(TOP_K):
        oh = oh + jnp.where(hits[k], 1.0, 0.0)
    r = lax.broadcasted_iota(I32, (tt, tt), 0)
    c = lax.broadcasted_iota(I32, (tt, tt), 1)
    upper = jnp.where(r < c, 1.0, 0.0).astype(BF16)
    before = _mm(oh.astype(BF16), upper) + carry[:, 0:1]
    rows = [jnp.sum(jnp.where(hits[k], before, 0.0), axis=0, keepdims=True) for k in range(TOP_K)]
    pos_ref[0] = jnp.concatenate(rows + rows, axis=0).astype(I32)
    carry[...] = carry[...] + jnp.sum(oh, axis=1, keepdims=True)
    cnt_ref[...] = carry[...]


def _route_positions(ids3, tt):
    nt = ids3.shape[0]
    return pl.pallas_call(
        functools.partial(_route_kernel, tt=tt),
        out_shape=[jax.ShapeDtypeStruct((nt, 8, tt), I32),
                   jax.ShapeDtypeStruct((N_EXPERTS, LANES), F32)],
        grid=(nt,),
        in_specs=[pl.BlockSpec((1, 8, tt), lambda i: (i, 0, 0))],
        out_specs=[pl.BlockSpec((1, 8, tt), lambda i: (i, 0, 0)),
                   pl.BlockSpec((N_EXPERTS, LANES), lambda i: (0, 0))],
        scratch_shapes=[pltpu.VMEM((N_EXPERTS, LANES), F32)],
        compiler_params=_cparams(("arbitrary",)),
        name="moe_route_positions",
    )(ids3)


def _scatter_kernel(zs_ref, dest_hbm, h_ref, xs_hbm, idx, zbuf, isem, sem, zsem, *, tt, tm):
    i = pl.program_id(0)

    @pl.when(i == 0)
    def _():
        zbuf[...] = jnp.zeros_like(zbuf)
        for e in range(N_EXPERTS):
            pltpu.make_async_copy(zbuf, xs_hbm.at[pl.ds(pl.multiple_of(zs_ref[e], tm), tm)], zsem).start()
        for e in range(N_EXPERTS):
            pltpu.make_async_copy(zbuf, xs_hbm.at[pl.ds(0, tm)], zsem).wait()

    cp = pltpu.make_async_copy(dest_hbm.at[i], idx, isem)
    cp.start()
    cp.wait()

    def issue(t, _):
        for k in range(TOP_K):
            pltpu.make_async_copy(h_ref.at[pl.ds(t, 1)], xs_hbm.at[pl.ds(idx[k * tt + t], 1)],
                                  sem).start()
        return 0

    lax.fori_loop(0, tt, issue, 0)
    for k in range(TOP_K):
        pltpu.make_async_copy(h_ref, xs_hbm.at[pl.ds(0, tt)], sem).wait()


def _scatter_rows(zero_starts, dest2, h_all, n_rows, tt, tm):
    nt = dest2.shape[0]
    d = h_all.shape[1]
    return pl.pallas_call(
        functools.partial(_scatter_kernel, tt=tt, tm=tm),
        out_shape=jax.ShapeDtypeStruct((n_rows, d), F32),
        grid_spec=pltpu.PrefetchScalarGridSpec(
            num_scalar_prefetch=1,
            grid=(nt,),
            in_specs=[pl.BlockSpec(memory_space=pl.ANY),
                      pl.BlockSpec((tt, d), lambda i, zs: (i, 0))],
            out_specs=pl.BlockSpec(memory_space=pl.ANY),
            scratch_shapes=[pltpu.SMEM((TOP_K * tt,), I32), pltpu.VMEM((tm, d), F32),
                            pltpu.SemaphoreType.DMA, pltpu.SemaphoreType.DMA,
                            pltpu.SemaphoreType.DMA]),
        compiler_params=_cparams(("arbitrary",)),
        name="moe_scatter_rows",
    )(zero_starts, dest2, h_all)


def _expert_kernel(be_ref, nu_ref, x_ref, wgu_ref, bgu_ref, wdn_ref, bdn_ref, y_ref, wgu_bf, wdn_bf,
                   *, d_ff):
    i = pl.program_id(0)
    last = jnp.minimum(i, nu_ref[0] - 1)

    @pl.when((i == 0) | (be_ref[last] != be_ref[jnp.maximum(last - 1, 0)]))
    def _():
        wgu_bf[...] = wgu_ref[0].astype(BF16)
        wdn_bf[...] = wdn_ref[0].astype(BF16)

    @pl.when(i < nu_ref[0])
    def _():
        gu = _mm(x_ref[...].astype(BF16), wgu_bf[...]) + bgu_ref[0]
        g = jnp.minimum(gu[:, :d_ff], SWIGLU_LIMIT)
        u = jnp.clip(gu[:, d_ff:], -SWIGLU_LIMIT, SWIGLU_LIMIT)
        a = g * (1.0 / (1.0 + jnp.exp(-SWIGLU_ALPHA * g))) * (u + 1.0)
        y_ref[...] = _mm(a.astype(BF16), wdn_bf[...]) + bdn_ref[0]

    @pl.when(pl.program_id(0) >= nu_ref[0])
    def _():
        y_ref[...] = jnp.zeros_like(y_ref)


def _expert_matmul(blk_e, n_used, xs, wgu, bgu, wdn, bdn, n_rows, tm):
    d = xs.shape[1]
    d_ff = wdn.shape[1]
    nblk = n_rows // tm
    row = lambda i, be, nu: (jnp.minimum(i, nu[0] - 1), 0)
    wsel = lambda i, be, nu: (be[jnp.minimum(i, nu[0] - 1)], 0, 0)
    return pl.pallas_call(
        functools.partial(_expert_kernel, d_ff=d_ff),
        out_shape=jax.ShapeDtypeStruct((n_rows, d), F32),
        grid_spec=pltpu.PrefetchScalarGridSpec(
            num_scalar_prefetch=2,
            grid=(nblk,),
            in_specs=[pl.BlockSpec((tm, d), row),
                      pl.BlockSpec((1, d, 2 * d_ff), wsel),
                      pl.BlockSpec((1, 1, 2 * d_ff), wsel),
                      pl.BlockSpec((1, d_ff, d), wsel),
                      pl.BlockSpec((1, 1, d), wsel)],
            out_specs=pl.BlockSpec((tm, d), lambda i, be, nu: (i, 0)),
            scratch_shapes=[pltpu.VMEM((d, 2 * d_ff), BF16), pltpu.VMEM((d_ff, d), BF16)]),
        compiler_params=_cparams(("arbitrary",)),
        name="moe_expert_matmul",
    )(blk_e, n_used, xs, wgu, bgu, wdn, bdn)


def _combine_kernel(dest_hbm, y_hbm, x1_ref, gt_ref, gate_ref, o_ref, idx, buf, isem, sem, *, tt, tile0):
    b = pl.program_id(0)
    s = pl.program_id(1)
    i = tile0 + b * pl.num_programs(1) + s
    cp = pltpu.make_async_copy(dest_hbm.at[i], idx, isem)
    cp.start()
    cp.wait()

    def issue(t, _):
        for k in range(TOP_K):
            pltpu.make_async_copy(y_hbm.at[pl.ds(idx[k * tt + t], 1)], buf.at[k, pl.ds(t, 1)],
                                  sem).start()
        return 0

    lax.fori_loop(0, tt, issue, 0)
    for k in range(TOP_K):
        pltpu.make_async_copy(y_hbm.at[pl.ds(0, tt)], buf.at[k], sem).wait()
    gate = gate_ref[0]
    moe = gate[:, 0:1] * buf[0]
    for k in range(1, TOP_K):
        moe = moe + gate[:, k:k + 1] * buf[k]
    o_ref[0] = x1_ref[0] + gt_ref[0] * moe


def _combine(dest2, y_rows, x1, gt, gate_t, tile0, tt):
    bx, sx, d = x1.shape
    r = gt.shape[1]
    rb = 1 if r == 1 else tt
    nt = sx // tt
    mod_spec = pl.BlockSpec((1, rb, d), (lambda b, s: (b, 0, 0)) if r == 1 else (lambda b, s: (b, s, 0)))
    return pl.pallas_call(
        functools.partial(_combine_kernel, tt=tt, tile0=tile0),
        out_shape=jax.ShapeDtypeStruct((bx, sx, d), F32),
        grid=(bx, nt),
        in_specs=[pl.BlockSpec(memory_space=pl.ANY), pl.BlockSpec(memory_space=pl.ANY),
                  pl.BlockSpec((1, tt, d), lambda b, s: (b, s, 0)), mod_spec,
                  pl.BlockSpec((1, tt, TOP_K), lambda b, s: (b * nt + s, 0, 0))],
        out_specs=pl.BlockSpec((1, tt, d), lambda b, s: (b, s, 0)),
        scratch_shapes=[pltpu.SMEM((TOP_K * tt,), I32), pltpu.VMEM((TOP_K, tt, d), F32),
                        pltpu.SemaphoreType.DMA, pltpu.SemaphoreType.DMA],
        compiler_params=_cparams(("arbitrary", "arbitrary")),
        name="moe_combine",
    )(dest2, y_rows, x1, gt, gate_t)


def _diff_decode_kernel(pt_ref, lam_ref, q_ref, kn_ref, vn_ref, sg_ref, *rest, pps, lam_init):
    k_refs = rest[:pps]
    v_refs = rest[pps:2 * pps]
    o_ref = rest[2 * pps]
    m_s, l_s, a0_s, a1_s = rest[2 * pps + 1:]
    j = pl.program_id(1)
    rnd = lambda x: x.astype(BF16).astype(F32)
    q = rnd(q_ref[0])
    scale = A_DQK ** -0.5
    n_hc = 2 * A_HEADS

    @pl.when(j == 0)
    def _():
        m_s[...] = jnp.full_like(m_s, NEG_BIG)
        l_s[...] = jnp.zeros_like(l_s)
        a0_s[...] = jnp.zeros_like(a0_s)
        a1_s[...] = jnp.zeros_like(a1_s)

    for p in range(pps):
        s = jnp.sum((rnd(k_refs[p][0]) * q).reshape(n_hc, A_DQK, PAGE), axis=1) * scale
        m_old = m_s[...]
        m_new = jnp.maximum(m_old, s)
        pr = jnp.exp(s - m_new)
        alpha = jnp.exp(m_old - m_new)
        m_s[...] = m_new
        l_s[...] = alpha * l_s[...] + pr
        for h in range(A_HEADS):
            rows = pl.ds(HEAD_DIM * h, HEAD_DIM)
            vt = v_refs[p][0, rows, :]
            for c, acc in ((0, a0_s), (1, a1_s)):
                i = 2 * h + c
                acc[rows, :] = alpha[i:i + 1] * acc[rows, :] + pr[i:i + 1] * vt

    @pl.when(j == pl.num_programs(1) - 1)
    def _():
        lam = _diff_lambda(lam_ref, lam_init)
        m = m_s[...]
        s_new = jnp.sum((rnd(kn_ref[0]) * q).reshape(n_hc, A_DQK, 1), axis=1) * scale
        big = jnp.maximum(jnp.max(m, axis=1, keepdims=True), s_new)
        w = jnp.exp(m - big)
        wn = jnp.exp(s_new - big)
        inv = 1.0 / (jnp.sum(l_s[...] * w, axis=1, keepdims=True) + wn)
        sg = sg_ref[...]
        for h in range(A_HEADS):
            rows = pl.ds(HEAD_DIM * h, HEAD_DIM)
            vn = vn_ref[0, rows, :]
            oc = []
            for c, acc in ((0, a0_s), (1, a1_s)):
                i = 2 * h + c
                num = jnp.sum(acc[rows, :] * w[i:i + 1], axis=1, keepdims=True) + wn[i:i + 1] * vn
                oc.append(num * inv[i:i + 1])
            o = oc[0] - lam * oc[1]
            ms = jnp.mean(o * o, axis=0, keepdims=True)
            o_ref[0, rows, :] = o * lax.rsqrt(ms + EPS) * sg * (1.0 - lam_init)


def _diff_attention_sample(page_table, lam_p, q_col, k_new_col, v_new_col, sub_g_col, cache_kt, cache_vt,
                           lam_init, pps):
    nb, n_pages = page_table.shape
    steps = n_pages // pps
    const = lambda shape: pl.BlockSpec(shape, lambda b, j, pt: (0,) * len(shape))
    per_b = pl.BlockSpec((1, 512, 1), lambda b, j, pt: (b, 0, 0))

    def page_spec(p):
        return pl.BlockSpec((1, 512, PAGE), lambda b, j, pt: (pt[b, j * pps + p], 0, 0))

    kern = functools.partial(_diff_decode_kernel, pps=pps, lam_init=lam_init)
    out = pl.pallas_call(
        kern,
        out_shape=jax.ShapeDtypeStruct((nb, 512, 1), F32),
        grid_spec=pltpu.PrefetchScalarGridSpec(
            num_scalar_prefetch=1,
            grid=(nb, steps),
            in_specs=[const((4, A_DQK)), per_b, per_b, per_b, const((HEAD_DIM, 1))]
                     + [page_spec(p) for p in range(pps)] * 2,
            out_specs=pl.BlockSpec((1, 512, 1), lambda b, j, pt: (b, 0, 0)),
            scratch_shapes=[pltpu.VMEM((2 * A_HEADS, PAGE), F32), pltpu.VMEM((2 * A_HEADS, PAGE), F32),
                            pltpu.VMEM((512, PAGE), F32), pltpu.VMEM((512, PAGE), F32)]),
        compiler_params=_cparams(("arbitrary", "arbitrary")),
        name="diff_attention_sample",
    )(page_table, lam_p, q_col, k_new_col, v_new_col, sub_g_col,
      *([cache_kt] * pps), *([cache_vt] * pps))
    return out[:, :, 0]


def _nsa_decode_cmp_kernel(pt_ref, q_ref, kn_ref, vn_ref, pet_ref, cwt_ref, ckg_ref, *rest,
                           pps, n_chunks, past_len):
    k_refs = rest[:pps]
    v_refs = rest[pps:2 * pps]
    ocmp_ref, sel_ref = rest[2 * pps:2 * pps + 2]
    ksum, vsum = rest[2 * pps + 2:]
    j = pl.program_id(1)
    nb_past = past_len // CMP_BLOCK
    per_page = PAGE // CMP_BLOCK
    nbp = n_chunks * LANES
    lane = lax.broadcasted_iota(I32, (1, LANES), 1)

    @pl.when(j == 0)
    def _():
        ksum[...] = jnp.zeros_like(ksum)
        vsum[...] = jnp.zeros_like(vsum)

    base = j * (pps * per_page)
    chunk = base // LANES
    lane0 = base % LANES
    for refs, acc in ((k_refs, ksum), (v_refs, vsum)):
        cur = acc[chunk]
        for p in range(pps):
            pg = refs[p][0]
            for t in range(per_page):
                in_blk = lane // CMP_BLOCK == t
                col = jnp.sum(jnp.where(in_blk, pg, 0.0), axis=1, keepdims=True)
                cur = jnp.where(lane == lane0 + per_page * p + t, col, cur)
        acc[chunk] = cur

    @pl.when(j == pl.num_programs(1) - 1)
    def _():
        q = q_ref[0]
        blk = lax.broadcasted_iota(I32, (1, nbp), 1)
        qpos = past_len
        complete = (blk + 1) * CMP_BLOCK - 1 <= qpos
        cur_blk = qpos // CMP_BLOCK
        forced = blk * (blk - cur_blk) == 0
        is_new = blk == nb_past
        kall = jnp.where(is_new, kn_ref[0], jnp.concatenate([ksum[c] for c in range(n_chunks)], axis=1))
        vall = jnp.where(is_new, vn_ref[0], jnp.concatenate([vsum[c] for c in range(n_chunks)], axis=1))
        pe_k = jnp.sum(pet_ref[0], axis=1, keepdims=True)
        pe_v = jnp.sum(pet_ref[1], axis=1, keepdims=True)
        outs = []
        sel_rows = []
        for g in range(B_KV_HEADS):
            rows = slice(HEAD_DIM * g, HEAD_DIM * (g + 1))
            c = _mm(cwt_ref[0], (kall[rows] + pe_k) * (1.0 / CMP_BLOCK), HI)
            ck = c * lax.rsqrt(jnp.mean(c * c, axis=0, keepdims=True) + EPS) * ckg_ref[...]
            cv = _mm(cwt_ref[1], (vall[rows] + pe_v) * (1.0 / CMP_BLOCK), HI)
            qg = jnp.concatenate(
                [q[:, HEAD_DIM * (B_GROUP * g + r):HEAD_DIM * (B_GROUP * g + r + 1)]
                 for r in range(B_GROUP)] * 2, axis=0)
            s = jnp.where(complete, _mm(qg, ck, HI), NEG_BIG)
            e = jnp.where(complete, jnp.exp(s - jnp.max(s, axis=1, keepdims=True)), 0.0)
            p = e / jnp.maximum(jnp.sum(e, axis=1, keepdims=True), 1e-30)
            outs.append(_nt(p, cv, HI)[0:B_GROUP])
            imp = jnp.sum(p[0:B_GROUP], axis=0, keepdims=True)
            work = jnp.where(forced, FORCE_SCORE, jnp.where(complete, imp, -1.0))
            work = jnp.where(blk <= cur_blk, work, -2.0)
            picked = jnp.zeros((1, LANES), I32)
            for t in range(N_SELECT):
                mx = jnp.max(work, axis=1, keepdims=True)
                first = jnp.min(jnp.where(work == mx, blk, nbp), axis=1, keepdims=True)
                picked = jnp.where(lane == t, first, picked)
                work = jnp.where(blk == first, -3.0, work)
            sel_rows.append(picked)
        ocmp_ref[0] = jnp.concatenate(outs, axis=0)
        sel_ref[0] = jnp.concatenate(sel_rows * 4, axis=0)


def _nsa_decode_cmp(page_table, q, k_new_col, v_new_col, pet, cwt, ckg_col, cache_kt, cache_vt,
                    past_len, pps):
    nb, n_pages = page_table.shape
    steps = n_pages // pps
    n_blocks = past_len // CMP_BLOCK + 1
    n_chunks = -(-n_blocks // LANES)
    assert LANES % (pps * (PAGE // CMP_BLOCK)) == 0 and n_blocks >= N_SELECT
    const = lambda shape: pl.BlockSpec(shape, lambda b, j, pt: (0,) * len(shape))

    def page_spec(p):
        return pl.BlockSpec((1, LANES, PAGE), lambda b, j, pt: (pt[b, j * pps + p], 0, 0))

    kern = functools.partial(_nsa_decode_cmp_kernel, pps=pps, n_chunks=n_chunks, past_len=past_len)
    return pl.pallas_call(
        kern,
        out_shape=[jax.ShapeDtypeStruct((nb, 8, HEAD_DIM), F32),
                   jax.ShapeDtypeStruct((nb, 8, LANES), I32)],
        grid_spec=pltpu.PrefetchScalarGridSpec(
            num_scalar_prefetch=1,
            grid=(nb, steps),
            in_specs=[pl.BlockSpec((1, 1, 512), lambda b, j, pt: (b, 0, 0)),
                      pl.BlockSpec((1, LANES, 1), lambda b, j, pt: (b, 0, 0)),
                      pl.BlockSpec((1, LANES, 1), lambda b, j, pt: (b, 0, 0)),
                      const((2, HEAD_DIM, CMP_BLOCK)), const((2, HEAD_DIM, HEAD_DIM)),
                      const((HEAD_DIM, 1))]
                     + [page_spec(p) for p in range(pps)] * 2,
            out_specs=[pl.BlockSpec((1, 8, HEAD_DIM), lambda b, j, pt: (b, 0, 0)),
                       pl.BlockSpec((1, 8, LANES), lambda b, j, pt: (b, 0, 0))],
            scratch_shapes=[pltpu.VMEM((n_chunks, LANES, LANES), F32),
                            pltpu.VMEM((n_chunks, LANES, LANES), F32)]),
        compiler_params=_cparams(("arbitrary", "arbitrary")),
        name="nsa_sample_compressed",
    )(page_table, q, k_new_col, v_new_col, pet, cwt, ckg_col,
      *([cache_kt] * pps), *([cache_vt] * pps))


def _pick_head(x, g):
    return jnp.where(g == 0, x[:, 0:HEAD_DIM], x[:, HEAD_DIM:LANES])


def _nsa_decode_mix_kernel(pt_ref, sel_ref, q_ref, gate_ref, ocmp_ref, kn_ref, vn_ref, wn_ref,
                           kwin_ref, vwin_ref, *rest, past_len, n_sel):
    k_refs = rest[:n_sel]
    v_refs = rest[n_sel:2 * n_sel]
    o_ref = rest[2 * n_sel]
    b = pl.program_id(0)
    g = pl.program_id(1)
    q = q_ref[0]
    gates = gate_ref[0]
    ocmp = ocmp_ref[0, 0]
    nb_past = past_len // CMP_BLOCK
    kwin = kwin_ref[0]
    vwin = vwin_ref[0]
    w_buf = kwin.shape[1]
    wrow = lax.broadcasted_iota(I32, (1, w_buf), 1)
    wlo = max(w_buf - WINDOW + 1, w_buf - past_len, 0)
    wmask = wrow >= wlo
    qg = jnp.concatenate([q[:, HEAD_DIM * r:HEAD_DIM * (r + 1)] for r in range(B_GROUP)] * 2,
                         axis=0)
    kn = _pick_head(kn_ref[0][:, LANES:2 * LANES], g)
    vn = _pick_head(vn_ref[0][:, LANES:2 * LANES], g)
    s_new = jnp.sum(qg * kn, axis=1, keepdims=True)
    ss = []
    halves = []
    for t in range(n_sel):
        blk = sel_ref[b, g * n_sel + t]
        valid = blk < nb_past
        halves.append(blk % (PAGE // CMP_BLOCK))
        s = _mm(qg, _pick_head(k_refs[t][0], halves[t]), HI)
        ss.append(jnp.where(valid, s, NEG_BIG))
    m = s_new
    for s in ss:
        m = jnp.maximum(m, jnp.max(s, axis=1, keepdims=True))
    l = jnp.exp(s_new - m)
    acc = l * vn
    for t in range(n_sel):
        p = jnp.exp(ss[t] - m)
        l = l + jnp.sum(p, axis=1, keepdims=True)
        acc = acc + _nt(p, _pick_head(v_refs[t][0], halves[t]), HI)
    o_slc = acc / l
    kwn = _pick_head(wn_ref[0][:, 0:LANES], g)
    vwn = _pick_head(wn_ref[0][:, LANES:2 * LANES], g)
    sw_new = jnp.sum(qg * kwn, axis=1, keepdims=True)
    sw = jnp.where(wmask, _mm(qg, kwin, HI), NEG_BIG)
    mw = jnp.maximum(sw_new, jnp.max(sw, axis=1, keepdims=True))
    pw = jnp.exp(sw - mw)
    pn = jnp.exp(sw_new - mw)
    lw = pn + jnp.sum(pw, axis=1, keepdims=True)
    o_win = (pn * vwn + _nt(pw, vwin, HI)) / lw
    heads = []
    for r in range(B_GROUP):
        def gate(c, r=r):
            return jnp.where(g == 0, gates[:, 8 * c + r:8 * c + r + 1],
                             gates[:, 8 * c + B_GROUP + r:8 * c + B_GROUP + r + 1])
        heads.append(gate(0) * ocmp[r:r + 1] + gate(1) * o_slc[r:r + 1] + gate(2) * o_win[r:r + 1])
    o_ref[0] = jnp.broadcast_to(jnp.concatenate(heads, axis=1), (8, 256))


def _nsa_decode_mix(page_table, sel2, q, gates, ocmp, nk_new, nv_new, win_new, state_t,
                    cache_kt, cache_vt, past_len, n_sel):
    nb = page_table.shape[0]
    w_buf = state_t.shape[2]
    nb_past = past_len // CMP_BLOCK
    per_page = PAGE // CMP_BLOCK
    per_b = lambda r, w: pl.BlockSpec((1, r, w), lambda b, g, pt, sl: (b, 0, 0))

    def blk_spec(t):
        def imap(b, g, pt, sl):
            blk = jnp.minimum(sl[b, g * n_sel + t], nb_past - 1)
            return (pt[b, blk // per_page], B_KV_HEADS + g, 0)
        return pl.BlockSpec((1, HEAD_DIM, PAGE), imap)

    kern = functools.partial(_nsa_decode_mix_kernel, past_len=past_len, n_sel=n_sel)
    out = pl.pallas_call(
        kern,
        out_shape=jax.ShapeDtypeStruct((nb, 8, 512), F32),
        grid_spec=pltpu.PrefetchScalarGridSpec(
            num_scalar_prefetch=2,
            grid=(nb, B_KV_HEADS),
            in_specs=[pl.BlockSpec((1, 1, 256), lambda b, g, pt, sl: (b, 0, g)),
                      per_b(1, LANES),
                      pl.BlockSpec((1, 1, B_GROUP, HEAD_DIM), lambda b, g, pt, sl: (b, g, 0, 0)),
                      per_b(1, 256), per_b(1, 256), per_b(1, 256),
                      pl.BlockSpec((1, HEAD_DIM, w_buf), lambda b, g, pt, sl: (b, g, 0)),
                      pl.BlockSpec((1, HEAD_DIM, w_buf), lambda b, g, pt, sl: (b, B_KV_HEADS + g, 0))]
                     + [blk_spec(t) for t in range(n_sel)] * 2,
            out_specs=pl.BlockSpec((1, 8, 256), lambda b, g, pt, sl: (b, 0, g))),
        compiler_params=_cparams(("arbitrary", "arbitrary")),
        name="nsa_sample_mix",
    )(page_table, sel2, q, gates, ocmp, nk_new, nv_new, win_new, state_t, state_t,
      *([cache_kt] * n_sel), *([cache_vt] * n_sel))
    return out[:, 0, :]


def _moe(h2_list, ids_list, gates_list, x1_list, gt_list, prm, tt, tm):
    d = h2_list[0].shape[-1]
    n_group = [h.shape[0] * h.shape[1] for h in h2_list]
    n_tok = sum(n_group)
    nt = -(-n_tok // tt)
    ntp = nt * tt
    ids = jnp.concatenate(ids_list, axis=1)[:TOP_K]
    ids = jnp.pad(ids, ((0, 8 - TOP_K), (0, ntp - n_tok)), constant_values=N_EXPERTS)
    ids3 = ids.reshape(8, nt, tt).transpose(1, 0, 2)
    pos3, cnt = _route_positions(ids3, tt)
    counts = cnt[:, 0].astype(I32)
    padded = (counts + tm - 1) // tm * tm
    ends_p = jnp.cumsum(padded)
    starts_p = ends_p - padded
    na = n_tok * TOP_K
    nblk = -(-(na + N_EXPERTS * (tm - 1)) // tm)
    n_rows = nblk * tm
    ids4 = ids3[:, :TOP_K, :]
    valid = ids4 < N_EXPERTS
    e_ar = jnp.arange(N_EXPERTS, dtype=I32)
    start_of = jnp.sum(jnp.where(ids4[..., None] == e_ar, starts_p, 0), axis=-1)
    dest = start_of + pos3[:, :TOP_K, :]
    tok_id = (jnp.arange(nt, dtype=I32)[:, None, None] * tt + jnp.arange(tt, dtype=I32)[None, None, :])
    pad_rank = (tok_id - n_tok) * TOP_K + jnp.arange(TOP_K, dtype=I32)[None, :, None]
    dest_scatter = jnp.where(valid, dest, n_rows + pad_rank).reshape(nt, TOP_K * tt)
    dest_gather = jnp.where(valid, dest, 0).reshape(nt, TOP_K * tt)
    n_trash = (ntp - n_tok) * TOP_K
    blk_start = jnp.arange(nblk, dtype=I32) * tm
    blk_e = jnp.minimum(jnp.sum(jnp.where(ends_p[None, :] <= blk_start[:, None], 1, 0), axis=1),
                        N_EXPERTS - 1).astype(I32)
    n_used = (ends_p[-1:] // tm).astype(I32)

    h_all = jnp.concatenate([h.reshape(-1, d) for h in h2_list], axis=0)
    h_all = jnp.pad(h_all, ((0, ntp - n_tok), (0, 0)))
    zero_starts = jnp.where(padded > 0, ends_p - tm, n_rows - tm).astype(I32)
    xs = _scatter_rows(zero_starts, dest_scatter, h_all, n_rows + max(n_trash, 8), tt, tm)
    y_rows = _expert_matmul(blk_e, n_used, xs, prm["w_gu"], prm["b_gu"], prm["w_dn"], prm["b_dn"],
                            n_rows, tm)

    gates = jnp.concatenate(gates_list, axis=1)[:TOP_K]
    gates = jnp.pad(gates, ((0, 0), (0, ntp - n_tok)))
    gate_t = gates.reshape(TOP_K, nt, tt).transpose(1, 2, 0)
    outs = []
    tok0 = 0
    for x1, gt, n in zip(x1_list, gt_list, n_group):
        bx, sx, _ = x1.shape
        tile0 = tok0 // tt
        if sx % tt:
            padn = tt - sx
            x1p = jnp.pad(x1, ((0, 0), (0, padn), (0, 0)))
            gtp = jnp.pad(gt, ((0, 0), (0, padn), (0, 0)))
            o = _combine(dest_gather, y_rows, x1p, gtp, gate_t[tile0:tile0 + 1], tile0, tt)[:, :sx]
        else:
            o = _combine(dest_gather, y_rows, x1, gt, gate_t[tile0:tile0 + bx * (sx // tt)], tile0, tt)
        outs.append(o)
        tok0 += n
    return outs


def _flatten_rows(a):
    return a.transpose(1, 0, 2).reshape(8, -1)


def kernel(x_prompt, x_sample, c_prompt, c_sample, cache_diff_k, cache_diff_v, cache_nsa_k, cache_nsa_v, state_win_kv, page_table, attn_norm_g, ffn_norm_g, ada_w, ada_b, w_in, w_out, diff_q_norm_g, diff_k_norm_g, diff_lambda, diff_sub_norm_g, nsa_q_norm_g, nsa_k_norm_g, nsa_ck_norm_g, nsa_cmp_pe, nsa_cmp_w, router_w, router_b, expert_w_gu, expert_b_gu, expert_w_down, expert_b_down):
    depth = w_in.shape[0]
    assert depth == 1, "single-layer trunk"
    bp, sp, d = x_prompt.shape
    bs, ss, _ = x_sample.shape
    assert ss == 1
    n_pages = page_table.shape[1]
    past_len = n_pages * PAGE
    w_buf = state_win_kv.shape[2]
    n_pool = cache_diff_k.shape[1]
    l = 0
    lam_init = 0.8 - 0.6 * math.exp(-0.3 * l)
    d_ff = expert_w_down.shape[2]

    prm = {
        "attn_g": attn_norm_g[l].reshape(1, d),
        "ffn_g": ffn_norm_g[l].reshape(1, d),
        "w_in": jnp.pad(w_in[l], ((0, 0), (0, IN_PAD - w_in.shape[2]))).astype(BF16),
        "gd": jnp.stack([jnp.tile(diff_q_norm_g[l], 16), jnp.tile(diff_k_norm_g[l], 16)]),
        "gn": jnp.tile(nsa_q_norm_g[l], 8).reshape(1, 512),
        "gk": jnp.concatenate([jnp.tile(nsa_k_norm_g[l, 0], 2), jnp.tile(nsa_k_norm_g[l, 1], 2)]).reshape(1, 256),
        "m32": _group_mean_matrix(A_DQK, BF16),
        "m64": _group_mean_matrix(HEAD_DIM, BF16),
        "m32_f32": _group_mean_matrix(A_DQK, F32),
        "m64_f32": _group_mean_matrix(HEAD_DIM, F32),
        "w_out": w_out[l].astype(BF16),
        "router_wt": router_w[l].T,
        "router_b": router_b[l].reshape(N_EXPERTS, 1),
        "w_gu": expert_w_gu[l],
        "b_gu": expert_b_gu[l].reshape(N_EXPERTS, 1, 2 * d_ff),
        "w_dn": expert_w_down[l],
        "b_dn": expert_b_down[l].reshape(N_EXPERTS, 1, d),
    }
    lam_p = diff_lambda[l]
    sub_g2 = jnp.tile(diff_sub_norm_g[l], 2).reshape(1, LANES)
    sub_g8 = jnp.tile(diff_sub_norm_g[l], 8).reshape(1, 512)
    pe = nsa_cmp_pe[l]
    cw = nsa_cmp_w[l]
    ckg = nsa_ck_norm_g[l].reshape(1, HEAD_DIM)

    n_c = bp + bs
    n_cp = -(-n_c // 8) * 8
    c_all = jnp.pad(jnp.concatenate([c_prompt, c_sample], axis=0), ((0, n_cp - n_c), (0, 0)))
    mod = _modulation(c_all, ada_w[l], ada_b[l])
    mod_p = mod[:bp].reshape(bp, 1, 6, d)
    mod_s = mod[bp:n_c].reshape(1, bs, 6, d)
    sh1p, sc1p, gt1p, sh2p, sc2p, gt2p = [mod_p[:, :, i] for i in range(6)]
    sh1s, sc1s, gt1s, sh2s, sc2s, gt2s = [mod_s[:, :, i] for i in range(6)]

    ts = min(512, sp)
    pos_p = jnp.arange(sp, dtype=I32)
    (qa, dk, dkb, dv, dvb, qn, nk, nv, win, kb, vb, gate) = _in_projection(
        x_prompt, sh1p, sc1p, pos_p, prm, ts)
    tq = min(512, sp)
    o_a = _diff_attention_prompt(qa, dkb, dvb, lam_p, sub_g2, lam_init, tq, min(512, sp))
    tqn = min(256, sp)
    o_b = _nsa_prompt(qn, gate, nk, nv, kb, vb, pe, cw, ckg, tqn, min(512, sp))
    x1p, h2p, idsp, gatesp = _out_projection(o_a, o_b, x_prompt, gt1p, sh2p, sc2p, prm, ts)

    xs_ = x_sample.reshape(1, bs, d)
    pos_s = jnp.full((bs,), past_len, I32)
    (qa_s, dk_s, _, dv_s, _, qn_s, nk_s, nv_s, win_s, _, _, gate_s) = _in_projection(
        xs_, sh1s, sc1s, pos_s, prm, bs, sample=True)
    ckt = jnp.transpose(cache_diff_k[l], (0, 2, 3, 1)).reshape(n_pool, 512, PAGE)
    cvt = jnp.transpose(cache_diff_v[l], (0, 2, 3, 1)).reshape(n_pool, 512, PAGE)
    nkt = jnp.transpose(cache_nsa_k[l], (0, 2, 3, 4, 1)).reshape(n_pool, 256, PAGE)
    nvt = jnp.transpose(cache_nsa_v[l], (0, 2, 3, 4, 1)).reshape(n_pool, 256, PAGE)
    state = state_win_kv[l].reshape(bs, w_buf, 256)
    state_t = jnp.transpose(state_win_kv[l], (0, 2, 3, 4, 1)).reshape(bs, 256, w_buf)
    pps = 8 if n_pages % 8 == 0 else 1
    as3 = lambda a: a.reshape(bs, 1, a.shape[-1])
    col = lambda a: a.reshape(bs, a.shape[-1], 1)
    o_a_s = _diff_attention_sample(page_table, lam_p, col(qa_s[0]), col(dk_s[0]), col(dv_s[0]),
                                   diff_sub_norm_g[l].reshape(HEAD_DIM, 1), ckt, cvt, lam_init, pps)
    ocmp_s, sel_s = _nsa_decode_cmp(page_table, as3(qn_s[0]), col(nk_s[0][:, :LANES]),
                                    col(nv_s[0][:, :LANES]), jnp.transpose(pe, (0, 2, 1)),
                                    jnp.transpose(cw, (0, 2, 1)), ckg.reshape(HEAD_DIM, 1),
                                    nkt, nvt, past_len, pps)
    n_sel = min(N_SELECT, past_len // CMP_BLOCK + 1)
    sel2 = jnp.concatenate([sel_s[:, 0, :n_sel], sel_s[:, 1, :n_sel]], axis=1)
    o_b_s = _nsa_decode_mix(page_table, sel2, as3(qn_s[0]), as3(gate_s[0]),
                            ocmp_s.reshape(bs, B_KV_HEADS, B_GROUP, HEAD_DIM), as3(nk_s[0]), as3(nv_s[0]),
                            as3(win_s[0]), state_t, nkt, nvt, past_len, n_sel)
    x1s, h2s, idss, gatess = _out_projection(o_a_s.astype(BF16).reshape(1, bs, 512),
                                             o_b_s.astype(BF16).reshape(1, bs, 512),
                                             xs_, gt1s, sh2s, sc2s, prm, bs)

    y_p, y_s = _moe([h2p, h2s], [_flatten_rows(idsp), _flatten_rows(idss)],
                    [_flatten_rows(gatesp), _flatten_rows(gatess)], [x1p, x1s], [gt2p, gt2s],
                    prm, 256, 256)

    p_win = win[:, sp - w_buf:] if sp >= w_buf else jnp.pad(win, ((0, 0), (w_buf - sp, 0), (0, 0)))
    s_win = jnp.concatenate([state, win_s[0][:, None, :]], axis=1)[:, 1:]
    return (y_p, y_s.reshape(bs, 1, d),
            dk.reshape(1, bp, sp, A_HEADS, 2 * A_DQK), dv.reshape(1, bp, sp, A_HEADS, HEAD_DIM),
            nk.reshape(1, bp, sp, 2, B_KV_HEADS, HEAD_DIM), nv.reshape(1, bp, sp, 2, B_KV_HEADS, HEAD_DIM),
            p_win.reshape(1, bp, w_buf, 2, B_KV_HEADS, HEAD_DIM),
            dk_s.reshape(1, bs, 1, A_HEADS, 2 * A_DQK), dv_s.reshape(1, bs, 1, A_HEADS, HEAD_DIM),
            nk_s.reshape(1, bs, 1, 2, B_KV_HEADS, HEAD_DIM), nv_s.reshape(1, bs, 1, 2, B_KV_HEADS, HEAD_DIM),
            s_win.reshape(1, bs, w_buf, 2, B_KV_HEADS, HEAD_DIM))
```

```python
import functools
import math

import jax
import jax.numpy as jnp
from jax import lax
from jax.experimental import pallas as pl
from jax.experimental.pallas import tpu as pltpu

F32 = jnp.float32
BF16 = jnp.bfloat16
I32 = jnp.int32
HI = lax.Precision.HIGHEST

HEAD_DIM = 64
A_HEADS = 8
A_DQK = 32
B_HEADS = 8
B_KV_HEADS = 2
B_GROUP = 4
CMP_BLOCK = 64
N_SELECT = 16
WINDOW = 512
ROPE_THETA = 500000.0
ROPE_FRACTION = 4
N_EXPERTS = 32
TOP_K = 4
SWIGLU_LIMIT = 7.0
SWIGLU_ALPHA = 1.702
EPS = 1e-6
NEG_BIG = -1e30
FORCE_SCORE = 1e4
PAGE = 128

A_Q = 512
IN_PAD = 2944
LANES = 128
STRIP = 32
LOG2E = 1.4426950408889634
VMEM_LIMIT = 56 * 1024 * 1024
VMEM_LIMIT_LARGE = 60 * 1024 * 1024


def _cparams(sem, vmem=VMEM_LIMIT):
    return pltpu.CompilerParams(dimension_semantics=sem, vmem_limit_bytes=vmem)


def _nt(a, b, precision=None):
    return lax.dot_general(a, b, (((1,), (1,)), ((), ())),
                           preferred_element_type=F32, precision=precision)


def _mm(a, b, precision=None):
    return jnp.dot(a, b, preferred_element_type=F32, precision=precision)


def _mod_kernel(c_ref, w_ref, b_ref, o_ref):
    c = c_ref[...]
    s = c / (1.0 + jnp.exp(-c))
    o_ref[...] = _mm(s.astype(BF16), w_ref[...].astype(BF16)) + b_ref[...]


def _modulation(c_all, ada_w, ada_b):
    n, d = c_all.shape
    width = ada_w.shape[1]
    tn = 1024
    return pl.pallas_call(
        _mod_kernel,
        out_shape=jax.ShapeDtypeStruct((n, width), F32),
        grid=(width // tn,),
        in_specs=[pl.BlockSpec((n, d), lambda j: (0, 0)),
                  pl.BlockSpec((d, tn), lambda j: (0, j)),
                  pl.BlockSpec((1, tn), lambda j: (0, j))],
        out_specs=pl.BlockSpec((n, tn), lambda j: (0, j)),
        compiler_params=_cparams(("arbitrary",)),
        name="adaln_mod",
    )(c_all, ada_w, ada_b.reshape(1, width))


def _group_norm(seg, gmat, gvec):
    out = []
    for j in range(seg.shape[1] // 256):
        c = seg[:, 256 * j:256 * (j + 1)]
        if gmat.dtype == F32:
            ms = _mm(c * c, gmat, HI)
        else:
            ms = _mm((c * c).astype(BF16), gmat)
        out.append(c * lax.rsqrt(ms + EPS) * gvec[:, 256 * j:256 * (j + 1)])
    return out[0] if len(out) == 1 else jnp.concatenate(out, axis=1)


def _rope_lanes(seg, tab_ref, half):
    w = seg.shape[1]
    rep = w // LANES
    cos = jnp.concatenate([tab_ref[0]] * rep, axis=1)
    s_up = jnp.concatenate([tab_ref[1]] * rep, axis=1)
    s_dn = jnp.concatenate([tab_ref[2]] * rep, axis=1)
    return (seg * cos + pltpu.roll(seg, w - half, 1) * s_up
            + pltpu.roll(seg, half, 1) * s_dn)


def _inproj_kernel(x_ref, sh_ref, sc_ref, g_ref, w_ref, gd_ref, gn_ref, gk_ref,
                   m32_ref, m64_ref, td_ref, tn_ref,
                   qa_ref, dk_ref, dkb_ref, dv_ref, dvb_ref, qn_ref, nk_ref, nv_ref,
                   win_ref, kb_ref, vb_ref, gate_ref):
    x = x_ref[0]
    ms = jnp.mean(x * x, axis=-1, keepdims=True)
    h = x * lax.rsqrt(ms + EPS) * g_ref[...]
    h = h * (1.0 + sc_ref[0]) + sh_ref[0]
    proj = _mm(h.astype(BF16), w_ref[...])
    m32 = m32_ref[...]
    m64 = m64_ref[...]
    gd = gd_ref[...]

    qa = _rope_lanes(_group_norm(proj[:, 0:512], m32, gd[0:1]), td_ref, 4)
    q_scale = A_DQK ** -0.5 * LOG2E if qa_ref.dtype == BF16 else 1.0
    qa_ref[0] = (qa * q_scale).astype(qa_ref.dtype)
    ka = _rope_lanes(_group_norm(proj[:, 512:1024], m32, gd[1:2]), td_ref, 4)
    dk_ref[0] = ka
    dkb_ref[0] = ka.astype(BF16)
    va = proj[:, 1024:1536]
    dv_ref[0] = va
    dvb_ref[0] = va.astype(BF16)
    qn = _rope_lanes(_group_norm(proj[:, 1536:2048], m64, gn_ref[...]), tn_ref, 8)
    qn_ref[0] = qn * (HEAD_DIM ** -0.5)
    k_cmp = _rope_lanes(proj[:, 2048:2176], tn_ref, 8)
    k_sw = _rope_lanes(_group_norm(proj[:, 2176:2432], m64, gk_ref[...]), tn_ref, 8)
    vb = proj[:, 2432:2816]
    nk_ref[0] = jnp.concatenate([k_cmp, k_sw[:, 0:128]], axis=1)
    nv_ref[0] = vb[:, 0:256]
    win_ref[0] = jnp.concatenate([k_sw[:, 128:256], vb[:, 256:384]], axis=1)
    kb_ref[0] = jnp.concatenate([k_cmp, k_sw], axis=1).astype(BF16)
    vb_ref[0] = vb.astype(BF16)
    gl = proj[:, 2816:2944]
    gate_ref[0] = 1.0 / (1.0 + jnp.exp(-gl))


def _rope_tables(pos, group, half):
    inv = ROPE_THETA ** (-jnp.arange(half, dtype=F32) / half)
    ang = pos.astype(F32)[:, None] * inv[None, :]
    cos, sin = jnp.cos(ang), jnp.sin(ang)
    n = pos.shape[0]
    pad = group - 2 * half
    c = jnp.concatenate([cos, cos, jnp.ones((n, pad), F32)], axis=1)
    up = jnp.concatenate([-sin, jnp.zeros((n, half + pad), F32)], axis=1)
    dn = jnp.concatenate([jnp.zeros((n, half), F32), sin, jnp.zeros((n, pad), F32)], axis=1)
    rep = LANES // group
    return jnp.stack([jnp.tile(c, (1, rep)), jnp.tile(up, (1, rep)), jnp.tile(dn, (1, rep))])


def _group_mean_matrix(group, dtype):
    i = jnp.arange(256)
    return jnp.where((i[:, None] // group) == (i[None, :] // group), 1.0 / group, 0.0).astype(dtype)


def _in_projection(x, sh, sc, pos, prm, ts, sample=False):
    bx, sx, d = x.shape
    r = sh.shape[1]
    rb = 1 if r == 1 else ts
    td = _rope_tables(pos, A_DQK, A_DQK // ROPE_FRACTION // 2)
    tn = _rope_tables(pos, HEAD_DIM, HEAD_DIM // ROPE_FRACTION // 2)

    def tok(width, dtype):
        return (jax.ShapeDtypeStruct((bx, sx, width), dtype),
                pl.BlockSpec((1, ts, width), lambda s, b: (b, s, 0)))

    sfx = "_f32" if sample else ""
    outs = [tok(512, F32 if sample else BF16), tok(512, F32), tok(512, BF16), tok(512, F32), tok(512, BF16),
            tok(512, F32), tok(256, F32), tok(256, F32), tok(256, F32), tok(384, BF16),
            tok(384, BF16), tok(128, F32)]
    const = lambda shape: pl.BlockSpec(shape, lambda s, b: (0,) * len(shape))
    mod_spec = pl.BlockSpec((1, rb, d), (lambda s, b: (b, 0, 0)) if r == 1 else (lambda s, b: (b, s, 0)))
    return pl.pallas_call(
        _inproj_kernel,
        out_shape=[o[0] for o in outs],
        grid=(sx // ts, bx),
        in_specs=[pl.BlockSpec((1, ts, d), lambda s, b: (b, s, 0)), mod_spec, mod_spec,
                  const((1, d)), const((d, IN_PAD)), const((2, 512)), const((1, 512)),
                  const((1, 256)), const((256, 256)), const((256, 256)),
                  pl.BlockSpec((3, ts, LANES), lambda s, b: (0, s, 0)),
                  pl.BlockSpec((3, ts, LANES), lambda s, b: (0, s, 0))],
        out_specs=[o[1] for o in outs],
        compiler_params=_cparams(("arbitrary", "arbitrary")),
        name="in_projection",
    )(x, sh, sc, prm["attn_g"], prm["w_in"], prm["gd"], prm["gn"], prm["gk"],
      prm["m32" + sfx], prm["m64" + sfx], td, tn)


def _diff_lambda(lam_ref, lam_init):
    lp = lam_ref[...]
    a = jnp.sum(lp[0:1] * lp[1:2], axis=1, keepdims=True)
    b = jnp.sum(lp[2:3] * lp[3:4], axis=1, keepdims=True)
    return jnp.exp(a) - jnp.exp(b) + lam_init


def _head_rms(o, sg):
    lane = lax.broadcasted_iota(I32, (1, LANES), 1)
    sq = o * o
    s0 = jnp.sum(jnp.where(lane < HEAD_DIM, sq, 0.0), axis=1, keepdims=True)
    s1 = jnp.sum(jnp.where(lane >= HEAD_DIM, sq, 0.0), axis=1, keepdims=True)
    ms = jnp.where(lane < HEAD_DIM, s0, s1) * (1.0 / HEAD_DIM)
    return o * lax.rsqrt(ms + EPS) * sg


def _diff_attn_kernel(lam_ref, q_ref, k_ref, v_ref, sg_ref, o_ref, *, tq, tk, lam_init):
    qi = pl.program_id(2)
    q = q_ref[0]
    lane = lax.broadcasted_iota(I32, (1, LANES), 1)
    lam = _diff_lambda(lam_ref, lam_init)
    zero = jnp.zeros_like(q)
    qm = [jnp.where((lane >= A_DQK * i) & (lane < A_DQK * (i + 1)), q, zero) for i in range(4)]
    first_head = lane < HEAD_DIM
    one = jnp.ones((tk, LANES), BF16)
    q0 = qi * tq

    def chunk(j, carry, masked):
        ms, accs = carry
        start = pl.multiple_of(j * tk, tk)
        kc = k_ref[0, pl.ds(start, tk), :]
        vc = v_ref[0, pl.ds(start, tk), :]
        vaug = (jnp.where(first_head, vc, one), jnp.where(first_head, one, vc))
        ss = [_nt(qm[i], kc) for i in range(4)]
        new_ms = [[] for _ in range(4)]
        alphas = [[] for _ in range(4)]
        ps = [[] for _ in range(4)]
        for r in range(tq // STRIP):
            rows = slice(r * STRIP, (r + 1) * STRIP)
            if masked:
                row = q0 + r * STRIP + lax.broadcasted_iota(I32, (STRIP, tk), 0)
                causal = start + lax.broadcasted_iota(I32, (STRIP, tk), 1) <= row
            for i in range(4):
                s = ss[i][rows]
                if masked:
                    s = jnp.where(causal, s, NEG_BIG)
                m_old = ms[i][rows]
                m_new = jnp.maximum(m_old, jnp.max(s, axis=1, keepdims=True))
                ps[i].append(jnp.exp2(s - jnp.concatenate([m_new] * (tk // LANES), axis=1)).astype(BF16))
                alphas[i].append(jnp.exp2(m_old - m_new))
                new_ms[i].append(m_new)
        new_accs = []
        for i in range(4):
            pv = _mm(jnp.concatenate(ps[i], axis=0), vaug[i // 2])
            new_accs.append(jnp.concatenate(alphas[i], axis=0) * accs[i] + pv)
        return tuple(jnp.concatenate(m, axis=0) for m in new_ms), tuple(new_accs)

    init = (tuple(jnp.full((tq, LANES), NEG_BIG, F32) for _ in range(4)),
            tuple(jnp.zeros((tq, LANES), F32) for _ in range(4)))
    n_full = q0 // tk
    carry = lax.fori_loop(0, n_full, lambda j, c: chunk(j, c, False), init)
    _, accs = chunk(n_full, carry, True)
    outs = [a / pltpu.roll(a, HEAD_DIM, 1) for a in accs]
    o0 = jnp.where(first_head, outs[0], outs[2])
    o1 = jnp.where(first_head, outs[1], outs[3])
    o = o0 - lam * o1
    o_ref[0] = (_head_rms(o, sg_ref[...]) * (1.0 - lam_init)).astype(BF16)


def _diff_attention_prompt(qa, dkb, dvb, lam_p, sub_g2, lam_init, tq, tk):
    b, s, _ = qa.shape
    assert tk % tq == 0 and s % tk == 0
    kern = functools.partial(_diff_attn_kernel, tq=tq, tk=tk, lam_init=lam_init)
    return pl.pallas_call(
        kern,
        out_shape=jax.ShapeDtypeStruct((b, s, 512), BF16),
        grid=(b, A_HEADS // 2, s // tq),
        in_specs=[pl.BlockSpec((4, A_DQK), lambda b_, p, i: (0, 0)),
                  pl.BlockSpec((1, tq, LANES), lambda b_, p, i: (b_, i, p)),
                  pl.BlockSpec((1, s, LANES), lambda b_, p, i: (b_, 0, p)),
                  pl.BlockSpec((1, s, LANES), lambda b_, p, i: (b_, 0, p)),
                  pl.BlockSpec((1, LANES), lambda b_, p, i: (0, 0))],
        out_specs=pl.BlockSpec((1, tq, LANES), lambda b_, p, i: (b_, i, p)),
        compiler_params=_cparams(("arbitrary", "arbitrary", "arbitrary")),
        name="diff_attention_prompt",
    )(lam_p, qa, dkb, dvb, sub_g2)


def _top_select_t(score_t, n_sel):
    nb = score_t.shape[0]
    blk = lax.broadcasted_iota(I32, (nb, 1), 0)
    cnt = jnp.zeros(score_t.shape, F32)
    for i in range(nb):
        row = score_t[i:i + 1, :]
        gt = jnp.where(row > score_t, 1.0, 0.0)
        eq = jnp.where(row == score_t, 1.0, 0.0)
        cnt = cnt + gt + jnp.where(blk > i, eq, 0.0)
    return jnp.where(cnt < n_sel, 1.0, 0.0)


def _strip_softmax(s, m_old, mask_fn):
    m_new, alpha, ps = [], [], []
    for r in range(s.shape[0] // STRIP):
        rows = slice(r * STRIP, (r + 1) * STRIP)
        sr = s[rows] if mask_fn is None else mask_fn(s[rows], rows)
        mn = jnp.broadcast_to(jnp.max(sr, axis=1, keepdims=True), (STRIP, LANES))
        if m_old is not None:
            mn = jnp.maximum(m_old[rows], mn)
            alpha.append(jnp.exp2(m_old[rows] - mn))
        ps.append(jnp.exp2(sr - jnp.concatenate([mn] * (s.shape[1] // LANES), axis=1)).astype(BF16))
        m_new.append(mn)
    cat = lambda xs: jnp.concatenate(xs, axis=0)
    return cat(m_new), (cat(alpha) if alpha else None), cat(ps)


def _compress(kc, pe_sum, w, nb):
    mean = (jnp.sum(kc.reshape(nb, CMP_BLOCK, HEAD_DIM), axis=1) + pe_sum) * (1.0 / CMP_BLOCK)
    return _mm(mean, w, HI)


def _nsa_prompt_kernel(q_ref, gate_ref, kcmp_ref, vcmp_ref, kslc_ref, vslc_ref, kwin_ref,
                       vwin_ref, pe_ref, cw_ref, ckg_ref, o_ref,
                       kaug, vs, kw, vw, ck, cv, *, tq, tk, seq):
    qi = pl.program_id(1)
    nb = seq // CMP_BLOCK
    n_sel = min(N_SELECT, nb)
    wk = WINDOW + tq

    @pl.when(qi == 0)
    def _():
        rowblk = lax.broadcasted_iota(I32, (seq, HEAD_DIM), 0) // CMP_BLOCK
        colblk = lax.broadcasted_iota(I32, (seq, HEAD_DIM), 1)
        onehot = jnp.where(rowblk == colblk, 1.0, 0.0).astype(BF16)
        pe_k = jnp.sum(pe_ref[0], axis=0, keepdims=True)
        pe_v = jnp.sum(pe_ref[1], axis=0, keepdims=True)
        ones = jnp.ones((seq, HEAD_DIM), BF16)
        for g in range(B_KV_HEADS):
            lo, hi = HEAD_DIM * g, HEAD_DIM * (g + 1)
            kaug[g] = jnp.concatenate([kslc_ref[0][:, lo:hi], onehot], axis=1)
            vs[g] = jnp.concatenate([vslc_ref[0][:, lo:hi], ones], axis=1)
            kw[g] = kwin_ref[0][:, lo:hi]
            vw[g] = jnp.concatenate([vwin_ref[0][:, lo:hi], ones], axis=1)
            c = _compress(kcmp_ref[0][:, lo:hi], pe_k, cw_ref[0], nb)
            ms = jnp.mean(c * c, axis=-1, keepdims=True)
            ck[g] = c * lax.rsqrt(ms + EPS) * ckg_ref[...]
            cv[g] = _compress(vcmp_ref[0][:, lo:hi], pe_v, cw_ref[1], nb)

    q = q_ref[0]
    gates = gate_ref[0]
    q0 = qi * tq
    qpos = q0 + lax.broadcasted_iota(I32, (tq, 1), 0)
    blk = lax.broadcasted_iota(I32, (1, nb), 1)
    complete = (blk + 1) * CMP_BLOCK - 1 <= qpos
    cur = qpos // CMP_BLOCK
    forced = blk * (blk - cur) == 0
    qpos4 = jnp.concatenate([qpos] * B_GROUP, axis=0)
    n_full = q0 // tk
    wstart = pl.multiple_of(jnp.maximum(q0 - WINDOW, 0), tq)
    kpos_w = wstart + lax.broadcasted_iota(I32, (1, wk), 1)
    qpos_t = q0 + lax.broadcasted_iota(I32, (1, tq), 1)
    blk_t = lax.broadcasted_iota(I32, (nb, 1), 0)
    complete_t = (blk_t + 1) * CMP_BLOCK - 1 <= qpos_t
    forced_t = blk_t * (blk_t - qpos_t // CMP_BLOCK) == 0
    eye = jnp.where(lax.broadcasted_iota(I32, (tq, tq), 0) == lax.broadcasted_iota(I32, (tq, tq), 1),
                    1.0, 0.0).astype(BF16)
    o_cmp_all, q4_all, qaug_all = [], [], []

    for g in range(B_KV_HEADS):
        qh = [q[:, HEAD_DIM * (B_GROUP * g + r):HEAD_DIM * (B_GROUP * g + r + 1)]
              for r in range(B_GROUP)]
        ckg = ck[g]
        cvg = cv[g]
        o_cmp = []
        imp_t = jnp.zeros((nb, tq), F32)
        for r in range(B_GROUP):
            s = jnp.where(complete, _nt(qh[r], ckg, HI), NEG_BIG)
            e = jnp.where(complete, jnp.exp(s - jnp.max(s, axis=1, keepdims=True)), 0.0)
            p = e / jnp.maximum(jnp.sum(e, axis=1, keepdims=True), 1e-30)
            o_cmp.append(_mm(p, cvg, HI))
            st = jnp.where(complete_t, _nt(ckg, qh[r], HI), NEG_BIG)
            et = jnp.where(complete_t, jnp.exp(st - jnp.max(st, axis=0, keepdims=True)), 0.0)
            imp_t = imp_t + et / jnp.maximum(jnp.sum(et, axis=0, keepdims=True), 1e-30)
        score_t = jnp.where(forced_t, FORCE_SCORE, jnp.where(complete_t, imp_t, -1.0))
        sel_t = _top_select_t(score_t, n_sel)
        sel = _nt(eye, sel_t.astype(BF16))
        bias = jnp.where(blk <= cur, jnp.where(sel > 0.5, 0.0, NEG_BIG), NEG_BIG).astype(BF16)
        if nb < HEAD_DIM:
            bias = jnp.concatenate([bias, jnp.zeros((tq, HEAD_DIM - nb), BF16)], axis=1)

        q4 = jnp.concatenate([(qh[r] * LOG2E).astype(BF16) for r in range(B_GROUP)], axis=0)
        o_cmp_all.append(o_cmp)
        q4_all.append(q4)
        qaug_all.append(jnp.concatenate([q4, jnp.concatenate([bias] * B_GROUP, axis=0)], axis=1))

    def chunk(j, carry, masked):
        start = pl.multiple_of(j * tk, tk)
        mask_fn = None
        if masked:
            kpos = start + lax.broadcasted_iota(I32, (1, tk), 1)
            mask_fn = lambda sr, rows: jnp.where(kpos <= qpos4[rows], sr, NEG_BIG)
        out = []
        for g in range(B_KV_HEADS):
            m, acc = carry[g]
            s = _nt(qaug_all[g], kaug[g, pl.ds(start, tk), :])
            m_new, alpha, p = _strip_softmax(s, m, mask_fn)
            out.append((m_new, alpha * acc + _mm(p, vs[g, pl.ds(start, tk), :])))
        return tuple(out)

    init = tuple((jnp.full((B_GROUP * tq, LANES), NEG_BIG, F32), jnp.zeros((B_GROUP * tq, LANES), F32))
                 for _ in range(B_KV_HEADS))
    carry = lax.fori_loop(0, n_full, lambda j, c: chunk(j, c, False), init)
    slc = chunk(n_full, carry, True)

    def in_window(sr, rows):
        dlt = qpos4[rows] - kpos_w
        return jnp.where(dlt >= 0, jnp.where(dlt < WINDOW, sr, NEG_BIG), NEG_BIG)

    heads_out = []
    for g in range(B_KV_HEADS):
        acc = slc[g][1]
        o_slc = acc[:, 0:HEAD_DIM] / acc[:, HEAD_DIM:LANES]
        sw = _nt(q4_all[g], kw[g, pl.ds(wstart, wk), :])
        _, _, pw = _strip_softmax(sw, None, in_window)
        accw = _mm(pw, vw[g, pl.ds(wstart, wk), :])
        o_win = accw[:, 0:HEAD_DIM] / accw[:, HEAD_DIM:LANES]
        for r in range(B_GROUP):
            h = B_GROUP * g + r
            rows = slice(r * tq, (r + 1) * tq)
            heads_out.append(gates[:, h:h + 1] * o_cmp_all[g][r]
                             + gates[:, 8 + h:9 + h] * o_slc[rows]
                             + gates[:, 16 + h:17 + h] * o_win[rows])
    o_ref[0] = jnp.concatenate(heads_out, axis=1).astype(BF16)


def _nsa_prompt(qn, gates, nk, nv, kb, vb, pe, cw, ckg, tq, tk):
    b, s, _ = qn.shape
    nb = s // CMP_BLOCK
    kern = functools.partial(_nsa_prompt_kernel, tq=tq, tk=tk, seq=s)
    full = lambda lane_blk: pl.BlockSpec((1, s, LANES), lambda b_, i: (b_, 0, lane_blk))
    const = lambda shape: pl.BlockSpec(shape, lambda b_, i: (0,) * len(shape))
    return pl.pallas_call(
        kern,
        out_shape=jax.ShapeDtypeStruct((b, s, 512), BF16),
        grid=(b, s // tq),
        in_specs=[pl.BlockSpec((1, tq, 512), lambda b_, i: (b_, i, 0)),
                  pl.BlockSpec((1, tq, LANES), lambda b_, i: (b_, i, 0)),
                  full(0), full(0), full(1), full(1), full(2), full(2),
                  const((2, CMP_BLOCK, HEAD_DIM)), const((2, HEAD_DIM, HEAD_DIM)),
                  const((1, HEAD_DIM))],
        out_specs=pl.BlockSpec((1, tq, 512), lambda b_, i: (b_, i, 0)),
        scratch_shapes=[pltpu.VMEM((2, s, LANES), BF16), pltpu.VMEM((2, s, LANES), BF16),
                        pltpu.VMEM((2, s, HEAD_DIM), BF16), pltpu.VMEM((2, s, LANES), BF16),
                        pltpu.VMEM((2, nb, HEAD_DIM), F32), pltpu.VMEM((2, nb, HEAD_DIM), F32)],
        compiler_params=_cparams(("arbitrary", "arbitrary"), VMEM_LIMIT_LARGE),
        name="nsa_prompt",
    )(qn, gates, nk, nv, kb, vb, kb, vb, pe, cw, ckg)


def _outproj_kernel(oa_ref, ob_ref, x_ref, gt_ref, sh_ref, sc_ref, g_ref, w_ref, rw_ref, rb_ref,
                    x1_ref, h2_ref, ids_ref, gates_ref):
    y = _mm(oa_ref[0], w_ref[0:512, :]) + _mm(ob_ref[0], w_ref[512:1024, :])
    x1 = x_ref[0] + gt_ref[0] * y
    x1_ref[0] = x1
    ms = jnp.mean(x1 * x1, axis=-1, keepdims=True)
    h2 = x1 * lax.rsqrt(ms + EPS) * g_ref[...]
    h2 = h2 * (1.0 + sc_ref[0]) + sh_ref[0]
    h2_ref[0] = h2
    logits = _nt(rw_ref[...].astype(BF16), h2.astype(BF16)) + rb_ref[...]
    eidx = lax.broadcasted_iota(I32, logits.shape, 0)
    work = logits
    vals, ids = [], []
    for _ in range(TOP_K):
        m = jnp.max(work, axis=0, keepdims=True)
        idx = jnp.min(jnp.where(work == m, eidx, N_EXPERTS), axis=0, keepdims=True)
        vals.append(m)
        ids.append(idx)
        work = jnp.where(eidx == idx, -3e38, work)
    es = [jnp.exp(v - vals[0]) for v in vals]
    tot = es[0] + es[1] + es[2] + es[3]
    ids_ref[0] = jnp.concatenate(ids + ids, axis=0)
    gates_ref[0] = jnp.concatenate([e / tot for e in es] * 2, axis=0)


def _out_projection(oa, ob, x, gt, sh, sc, prm, ts):
    bx, sx, d = x.shape
    r = gt.shape[1]
    rb = 1 if r == 1 else ts
    mod_spec = pl.BlockSpec((1, rb, d), (lambda b, s: (b, 0, 0)) if r == 1 else (lambda b, s: (b, s, 0)))
    const = lambda shape: pl.BlockSpec(shape, lambda b, s: (0,) * len(shape))
    tokspec = lambda w: pl.BlockSpec((1, ts, w), lambda b, s: (b, s, 0))
    nt = sx // ts
    return pl.pallas_call(
        _outproj_kernel,
        out_shape=[jax.ShapeDtypeStruct((bx, sx, d), F32), jax.ShapeDtypeStruct((bx, sx, d), F32),
                   jax.ShapeDtypeStruct((bx * nt, 8, ts), I32),
                   jax.ShapeDtypeStruct((bx * nt, 8, ts), F32)],
        grid=(bx, nt),
        in_specs=[tokspec(512), tokspec(512), tokspec(d), mod_spec, mod_spec, mod_spec,
                  const((1, d)), const((d, d)), const((N_EXPERTS, d)), const((N_EXPERTS, 1))],
        out_specs=[tokspec(d), tokspec(d),
                   pl.BlockSpec((1, 8, ts), lambda b, s: (b * nt + s, 0, 0)),
                   pl.BlockSpec((1, 8, ts), lambda b, s: (b * nt + s, 0, 0))],
        compiler_params=_cparams(("arbitrary", "arbitrary")),
        name="out_projection",
    )(oa, ob, x, gt, sh, sc, prm["ffn_g"], prm["w_out"], prm["router_wt"], prm["router_b"])


def _route_kernel(ids_ref, pos_ref, cnt_ref, carry, *, tt):
    @pl.when(pl.program_id(0) == 0)
    def _():
        carry[...] = jnp.zeros_like(carry)

    ids = ids_ref[0]
    e_iota = lax.broadcasted_iota(I32, (N_EXPERTS, tt), 0)
    hits = [ids[k:k + 1, :] == e_iota for k in range(TOP_K)]
    oh = jnp.zeros((N_EXPERTS, tt), F32)
    for k in range(TOP_K):
        oh = oh + jnp.where(hits[k], 1.0, 0.0)
    r = lax.broadcasted_iota(I32, (tt, tt), 0)
    c = lax.broadcasted_iota(I32, (tt, tt), 1)
    upper = jnp.where(r < c, 1.0, 0.0).astype(BF16)
    before = _mm(oh.astype(BF16), upper) + carry[:, 0:1]
    rows = [jnp.sum(jnp.where(hits[k], before, 0.0), axis=0, keepdims=True) for k in range(TOP_K)]
    pos_ref[0] = jnp.concatenate(rows + rows, axis=0).astype(I32)
    carry[...] = carry[...] + jnp.sum(oh, axis=1, keepdims=True)
    cnt_ref[...] = carry[...]


def _route_positions(ids3, tt):
    nt = ids3.shape[0]
    return pl.pallas_call(
        functools.partial(_route_kernel, tt=tt),
        out_shape=[jax.ShapeDtypeStruct((nt, 8, tt), I32),
                   jax.ShapeDtypeStruct((N_EXPERTS, LANES), F32)],
        grid=(nt,),
        in_specs=[pl.BlockSpec((1, 8, tt), lambda i: (i, 0, 0))],
        out_specs=[pl.BlockSpec((1, 8, tt), lambda i: (i, 0, 0)),
                   pl.BlockSpec((N_EXPERTS, LANES), lambda i: (0, 0))],
        scratch_shapes=[pltpu.VMEM((N_EXPERTS, LANES), F32)],
        compiler_params=_cparams(("arbitrary",)),
        name="moe_route_positions",
    )(ids3)


def _scatter_kernel(zs_ref, nu_ref, dest_hbm, h_ref, xs_hbm, idx, zbuf, isem, sem, zsem, *, tt, tm, nblk):
    i = pl.program_id(0)

    @pl.when(i == 0)
    def _():
        zbuf[...] = jnp.zeros_like(zbuf)

        def zero_block(row0):
            return pltpu.make_async_copy(zbuf, xs_hbm.at[pl.ds(pl.multiple_of(row0, tm), tm)], zsem)

        for e in range(N_EXPERTS):
            zero_block(zs_ref[e]).start()

        def start_tail(b, _):
            zero_block(b * tm).start()
            return 0

        def wait_tail(b, _):
            zero_block(b * tm).wait()
            return 0

        lax.fori_loop(nu_ref[0], nblk, start_tail, 0)
        for e in range(N_EXPERTS):
            zero_block(zs_ref[e]).wait()
        lax.fori_loop(nu_ref[0], nblk, wait_tail, 0)

    cp = pltpu.make_async_copy(dest_hbm.at[i], idx, isem)
    cp.start()
    cp.wait()

    def issue(t, _):
        for k in range(TOP_K):
            pltpu.make_async_copy(h_ref.at[pl.ds(t, 1)], xs_hbm.at[pl.ds(idx[k * tt + t], 1)],
                                  sem).start()
        return 0

    lax.fori_loop(0, tt, issue, 0, unroll=4)
    for k in range(TOP_K):
        pltpu.make_async_copy(h_ref, xs_hbm.at[pl.ds(0, tt)], sem).wait()


def _scatter_rows(zero_starts, n_used, dest2, h_all, n_rows, nblk, tt, tm):
    nt = dest2.shape[0]
    d = h_all.shape[1]
    return pl.pallas_call(
        functools.partial(_scatter_kernel, tt=tt, tm=tm, nblk=nblk),
        out_shape=jax.ShapeDtypeStruct((n_rows, d), F32),
        grid_spec=pltpu.PrefetchScalarGridSpec(
            num_scalar_prefetch=2,
            grid=(nt,),
            in_specs=[pl.BlockSpec(memory_space=pl.ANY),
                      pl.BlockSpec((tt, d), lambda i, zs, nu: (i, 0))],
            out_specs=pl.BlockSpec(memory_space=pl.ANY),
            scratch_shapes=[pltpu.SMEM((TOP_K * tt,), I32), pltpu.VMEM((tm, d), F32),
                            pltpu.SemaphoreType.DMA, pltpu.SemaphoreType.DMA,
                            pltpu.SemaphoreType.DMA]),
        compiler_params=_cparams(("arbitrary",)),
        name="moe_scatter_rows",
    )(zero_starts, n_used, dest2, h_all)


def _expert_kernel(be_ref, nu_ref, x_ref, wgu_ref, bgu_ref, wdn_ref, bdn_ref, y_ref, wgu_bf, wdn_bf,
                   *, d_ff):
    i = pl.program_id(0)
    last = jnp.minimum(i, nu_ref[0] - 1)

    @pl.when((i == 0) | (be_ref[last] != be_ref[jnp.maximum(last - 1, 0)]))
    def _():
        wgu_bf[...] = wgu_ref[0].astype(BF16)
        wdn_bf[...] = wdn_ref[0].astype(BF16)

    @pl.when(i < nu_ref[0])
    def _():
        gu = _mm(x_ref[...].astype(BF16), wgu_bf[...]) + bgu_ref[0]
        g = jnp.minimum(gu[:, :d_ff], SWIGLU_LIMIT)
        u = jnp.clip(gu[:, d_ff:], -SWIGLU_LIMIT, SWIGLU_LIMIT)
        a = g * (1.0 / (1.0 + jnp.exp(-SWIGLU_ALPHA * g))) * (u + 1.0)
        y_ref[...] = _mm(a.astype(BF16), wdn_bf[...]) + bdn_ref[0]

    @pl.when(pl.program_id(0) >= nu_ref[0])
    def _():
        y_ref[...] = jnp.zeros_like(y_ref)


def _expert_matmul(blk_e, n_used, xs, wgu, bgu, wdn, bdn, n_rows, tm):
    d = xs.shape[1]
    d_ff = wdn.shape[1]
    nblk = n_rows // tm
    row = lambda i, be, nu: (jnp.minimum(i, nu[0] - 1), 0)
    wsel = lambda i, be, nu: (be[jnp.minimum(i, nu[0] - 1)], 0, 0)
    return pl.pallas_call(
        functools.partial(_expert_kernel, d_ff=d_ff),
        out_shape=jax.ShapeDtypeStruct((n_rows, d), F32),
        grid_spec=pltpu.PrefetchScalarGridSpec(
            num_scalar_prefetch=2,
            grid=(nblk,),
            in_specs=[pl.BlockSpec((tm, d), row),
                      pl.BlockSpec((1, d, 2 * d_ff), wsel),
                      pl.BlockSpec((1, 1, 2 * d_ff), wsel),
                      pl.BlockSpec((1, d_ff, d), wsel),
                      pl.BlockSpec((1, 1, d), wsel)],
            out_specs=pl.BlockSpec((tm, d), lambda i, be, nu: (i, 0)),
            scratch_shapes=[pltpu.VMEM((d, 2 * d_ff), BF16), pltpu.VMEM((d_ff, d), BF16)]),
        compiler_params=_cparams(("arbitrary",)),
        name="moe_expert_matmul",
    )(blk_e, n_used, xs, wgu, bgu, wdn, bdn)


def _combine_kernel(dest_hbm, y_hbm, x1_ref, gt_ref, gate_ref, o_ref, idx, buf, isem, sem, *, tt, tile0):
    b = pl.program_id(0)
    s = pl.program_id(1)
    i = tile0 + b * pl.num_programs(1) + s
    cp = pltpu.make_async_copy(dest_hbm.at[i], idx, isem)
    cp.start()
    cp.wait()

    def issue(t, _):
        for k in range(TOP_K):
            pltpu.make_async_copy(y_hbm.at[pl.ds(idx[k * tt + t], 1)], buf.at[k, pl.ds(t, 1)],
                                  sem).start()
        return 0

    lax.fori_loop(0, tt, issue, 0, unroll=4)
    for k in range(TOP_K):
        pltpu.make_async_copy(y_hbm.at[pl.ds(0, tt)], buf.at[k], sem).wait()
    gate = gate_ref[0]
    moe = gate[:, 0:1] * buf[0]
    for k in range(1, TOP_K):
        moe = moe + gate[:, k:k + 1] * buf[k]
    o_ref[0] = x1_ref[0] + gt_ref[0] * moe


def _combine(dest2, y_rows, x1, gt, gate_t, tile0, tt):
    bx, sx, d = x1.shape
    r = gt.shape[1]
    rb = 1 if r == 1 else tt
    nt = sx // tt
    mod_spec = pl.BlockSpec((1, rb, d), (lambda b, s: (b, 0, 0)) if r == 1 else (lambda b, s: (b, s, 0)))
    return pl.pallas_call(
        functools.partial(_combine_kernel, tt=tt, tile0=tile0),
        out_shape=jax.ShapeDtypeStruct((bx, sx, d), F32),
        grid=(bx, nt),
        in_specs=[pl.BlockSpec(memory_space=pl.ANY), pl.BlockSpec(memory_space=pl.ANY),
                  pl.BlockSpec((1, tt, d), lambda b, s: (b, s, 0)), mod_spec,
                  pl.BlockSpec((1, tt, TOP_K), lambda b, s: (b * nt + s, 0, 0))],
        out_specs=pl.BlockSpec((1, tt, d), lambda b, s: (b, s, 0)),
        scratch_shapes=[pltpu.SMEM((TOP_K * tt,), I32), pltpu.VMEM((TOP_K, tt, d), F32),
                        pltpu.SemaphoreType.DMA, pltpu.SemaphoreType.DMA],
        compiler_params=_cparams(("arbitrary", "arbitrary")),
        name="moe_combine",
    )(dest2, y_rows, x1, gt, gate_t)


def _diff_decode_kernel(pt_ref, lam_ref, q_ref, kn_ref, vn_ref, sg_ref, *rest, pps, lam_init):
    k_refs = rest[:pps]
    v_refs = rest[pps:2 * pps]
    o_ref = rest[2 * pps]
    m_s, l_s, a0_s, a1_s = rest[2 * pps + 1:]
    j = pl.program_id(1)
    rnd = lambda x: x.astype(BF16).astype(F32)
    q = rnd(q_ref[0])
    scale = A_DQK ** -0.5
    n_hc = 2 * A_HEADS

    @pl.when(j == 0)
    def _():
        m_s[...] = jnp.full_like(m_s, NEG_BIG)
        l_s[...] = jnp.zeros_like(l_s)
        a0_s[...] = jnp.zeros_like(a0_s)
        a1_s[...] = jnp.zeros_like(a1_s)

    for p in range(pps):
        s = jnp.sum((rnd(k_refs[p][0]) * q).reshape(n_hc, A_DQK, PAGE), axis=1) * scale
        m_old = m_s[...]
        m_new = jnp.maximum(m_old, s)
        pr = jnp.exp(s - m_new)
        alpha = jnp.exp(m_old - m_new)
        m_s[...] = m_new
        l_s[...] = alpha * l_s[...] + pr
        for h in range(A_HEADS):
            rows = pl.ds(HEAD_DIM * h, HEAD_DIM)
            vt = v_refs[p][0, rows, :]
            for c, acc in ((0, a0_s), (1, a1_s)):
                i = 2 * h + c
                acc[rows, :] = alpha[i:i + 1] * acc[rows, :] + pr[i:i + 1] * vt

    @pl.when(j == pl.num_programs(1) - 1)
    def _():
        lam = _diff_lambda(lam_ref, lam_init)
        m = m_s[...]
        s_new = jnp.sum((rnd(kn_ref[0]) * q).reshape(n_hc, A_DQK, 1), axis=1) * scale
        big = jnp.maximum(jnp.max(m, axis=1, keepdims=True), s_new)
        w = jnp.exp(m - big)
        wn = jnp.exp(s_new - big)
        inv = 1.0 / (jnp.sum(l_s[...] * w, axis=1, keepdims=True) + wn)
        sg = sg_ref[...]
        for h in range(A_HEADS):
            rows = pl.ds(HEAD_DIM * h, HEAD_DIM)
            vn = vn_ref[0, rows, :]
            oc = []
            for c, acc in ((0, a0_s), (1, a1_s)):
                i = 2 * h + c
                num = jnp.sum(acc[rows, :] * w[i:i + 1], axis=1, keepdims=True) + wn[i:i + 1] * vn
                oc.append(num * inv[i:i + 1])
            o = oc[0] - lam * oc[1]
            ms = jnp.mean(o * o, axis=0, keepdims=True)
            o_ref[0, rows, :] = o * lax.rsqrt(ms + EPS) * sg * (1.0 - lam_init)


def _diff_attention_sample(page_table, lam_p, q_col, k_new_col, v_new_col, sub_g_col, cache_kt, cache_vt,
                           lam_init, pps):
    nb, n_pages = page_table.shape
    steps = n_pages // pps
    const = lambda shape: pl.BlockSpec(shape, lambda b, j, pt: (0,) * len(shape))
    per_b = pl.BlockSpec((1, 512, 1), lambda b, j, pt: (b, 0, 0))

    def page_spec(p):
        return pl.BlockSpec((1, 512, PAGE), lambda b, j, pt: (pt[b, j * pps + p], 0, 0))

    kern = functools.partial(_diff_decode_kernel, pps=pps, lam_init=lam_init)
    out = pl.pallas_call(
        kern,
        out_shape=jax.ShapeDtypeStruct((nb, 512, 1), F32),
        grid_spec=pltpu.PrefetchScalarGridSpec(
            num_scalar_prefetch=1,
            grid=(nb, steps),
            in_specs=[const((4, A_DQK)), per_b, per_b, per_b, const((HEAD_DIM, 1))]
                     + [page_spec(p) for p in range(pps)] * 2,
            out_specs=pl.BlockSpec((1, 512, 1), lambda b, j, pt: (b, 0, 0)),
            scratch_shapes=[pltpu.VMEM((2 * A_HEADS, PAGE), F32), pltpu.VMEM((2 * A_HEADS, PAGE), F32),
                            pltpu.VMEM((512, PAGE), F32), pltpu.VMEM((512, PAGE), F32)]),
        compiler_params=_cparams(("arbitrary", "arbitrary")),
        name="diff_attention_sample",
    )(page_table, lam_p, q_col, k_new_col, v_new_col, sub_g_col,
      *([cache_kt] * pps), *([cache_vt] * pps))
    return out[:, :, 0]


def _nsa_decode_cmp_kernel(pt_ref, q_ref, kn_ref, vn_ref, pet_ref, cwt_ref, ckg_ref, *rest,
                           pps, n_chunks, past_len):
    k_refs = rest[:pps]
    v_refs = rest[pps:2 * pps]
    ocmp_ref, sel_ref = rest[2 * pps:2 * pps + 2]
    ksum, vsum = rest[2 * pps + 2:]
    j = pl.program_id(1)
    nb_past = past_len // CMP_BLOCK
    per_page = PAGE // CMP_BLOCK
    nbp = n_chunks * LANES
    lane = lax.broadcasted_iota(I32, (1, LANES), 1)

    @pl.when(j == 0)
    def _():
        ksum[...] = jnp.zeros_like(ksum)
        vsum[...] = jnp.zeros_like(vsum)

    base = j * (pps * per_page)
    chunk = base // LANES
    lane0 = base % LANES
    for refs, acc in ((k_refs, ksum), (v_refs, vsum)):
        cur = acc[chunk]
        for p in range(pps):
            pg = refs[p][0]
            for t in range(per_page):
                in_blk = lane // CMP_BLOCK == t
                col = jnp.sum(jnp.where(in_blk, pg, 0.0), axis=1, keepdims=True)
                cur = jnp.where(lane == lane0 + per_page * p + t, col, cur)
        acc[chunk] = cur

    @pl.when(j == pl.num_programs(1) - 1)
    def _():
        q = q_ref[0]
        blk = lax.broadcasted_iota(I32, (1, nbp), 1)
        qpos = past_len
        complete = (blk + 1) * CMP_BLOCK - 1 <= qpos
        cur_blk = qpos // CMP_BLOCK
        forced = blk * (blk - cur_blk) == 0
        is_new = blk == nb_past
        kall = jnp.where(is_new, kn_ref[0], jnp.concatenate([ksum[c] for c in range(n_chunks)], axis=1))
        vall = jnp.where(is_new, vn_ref[0], jnp.concatenate([vsum[c] for c in range(n_chunks)], axis=1))
        pe_k = jnp.sum(pet_ref[0], axis=1, keepdims=True)
        pe_v = jnp.sum(pet_ref[1], axis=1, keepdims=True)
        outs = []
        sel_rows = []
        for g in range(B_KV_HEADS):
            rows = slice(HEAD_DIM * g, HEAD_DIM * (g + 1))
            c = _mm(cwt_ref[0], (kall[rows] + pe_k) * (1.0 / CMP_BLOCK), HI)
            ck = c * lax.rsqrt(jnp.mean(c * c, axis=0, keepdims=True) + EPS) * ckg_ref[...]
            cv = _mm(cwt_ref[1], (vall[rows] + pe_v) * (1.0 / CMP_BLOCK), HI)
            qg = jnp.concatenate(
                [q[:, HEAD_DIM * (B_GROUP * g + r):HEAD_DIM * (B_GROUP * g + r + 1)]
                 for r in range(B_GROUP)] * 2, axis=0)
            s = jnp.where(complete, _mm(qg, ck, HI), NEG_BIG)
            e = jnp.where(complete, jnp.exp(s - jnp.max(s, axis=1, keepdims=True)), 0.0)
            p = e / jnp.maximum(jnp.sum(e, axis=1, keepdims=True), 1e-30)
            outs.append(_nt(p, cv, HI)[0:B_GROUP])
            imp = jnp.sum(p[0:B_GROUP], axis=0, keepdims=True)
            work = jnp.where(forced, FORCE_SCORE, jnp.where(complete, imp, -1.0))
            work = jnp.where(blk <= cur_blk, work, -2.0)
            picked = jnp.zeros((1, LANES), I32)
            for t in range(N_SELECT):
                mx = jnp.max(work, axis=1, keepdims=True)
                first = jnp.min(jnp.where(work == mx, blk, nbp), axis=1, keepdims=True)
                picked = jnp.where(lane == t, first, picked)
                work = jnp.where(blk == first, -3.0, work)
            sel_rows.append(picked)
        ocmp_ref[0] = jnp.concatenate(outs, axis=0)
        sel_ref[0] = jnp.concatenate(sel_rows * 4, axis=0)


def _nsa_decode_cmp(page_table, q, k_new_col, v_new_col, pet, cwt, ckg_col, cache_kt, cache_vt,
                    past_len, pps):
    nb, n_pages = page_table.shape
    steps = n_pages // pps
    n_blocks = past_len // CMP_BLOCK + 1
    n_chunks = -(-n_blocks // LANES)
    assert LANES % (pps * (PAGE // CMP_BLOCK)) == 0 and n_blocks >= N_SELECT
    const = lambda shape: pl.BlockSpec(shape, lambda b, j, pt: (0,) * len(shape))

    def page_spec(p):
        return pl.BlockSpec((1, LANES, PAGE), lambda b, j, pt: (pt[b, j * pps + p], 0, 0))

    kern = functools.partial(_nsa_decode_cmp_kernel, pps=pps, n_chunks=n_chunks, past_len=past_len)
    return pl.pallas_call(
        kern,
        out_shape=[jax.ShapeDtypeStruct((nb, 8, HEAD_DIM), F32),
                   jax.ShapeDtypeStruct((nb, 8, LANES), I32)],
        grid_spec=pltpu.PrefetchScalarGridSpec(
            num_scalar_prefetch=1,
            grid=(nb, steps),
            in_specs=[pl.BlockSpec((1, 1, 512), lambda b, j, pt: (b, 0, 0)),
                      pl.BlockSpec((1, LANES, 1), lambda b, j, pt: (b, 0, 0)),
                      pl.BlockSpec((1, LANES, 1), lambda b, j, pt: (b, 0, 0)),
                      const((2, HEAD_DIM, CMP_BLOCK)), const((2, HEAD_DIM, HEAD_DIM)),
                      const((HEAD_DIM, 1))]
                     + [page_spec(p) for p in range(pps)] * 2,
            out_specs=[pl.BlockSpec((1, 8, HEAD_DIM), lambda b, j, pt: (b, 0, 0)),
                       pl.BlockSpec((1, 8, LANES), lambda b, j, pt: (b, 0, 0))],
            scratch_shapes=[pltpu.VMEM((n_chunks, LANES, LANES), F32),
                            pltpu.VMEM((n_chunks, LANES, LANES), F32)]),
        compiler_params=_cparams(("arbitrary", "arbitrary")),
        name="nsa_sample_compressed",
    )(page_table, q, k_new_col, v_new_col, pet, cwt, ckg_col,
      *([cache_kt] * pps), *([cache_vt] * pps))


def _pick_head(x, g):
    return jnp.where(g == 0, x[:, 0:HEAD_DIM], x[:, HEAD_DIM:LANES])


def _nsa_decode_mix_kernel(pt_ref, sel_ref, q_ref, gate_ref, ocmp_ref, kn_ref, vn_ref, wn_ref,
                           kwin_ref, vwin_ref, *rest, past_len, n_sel):
    k_refs = rest[:n_sel]
    v_refs = rest[n_sel:2 * n_sel]
    o_ref = rest[2 * n_sel]
    b = pl.program_id(0)
    g = pl.program_id(1)
    q = q_ref[0]
    gates = gate_ref[0]
    ocmp = ocmp_ref[0, 0]
    nb_past = past_len // CMP_BLOCK
    kwin = kwin_ref[0]
    vwin = vwin_ref[0]
    w_buf = kwin.shape[1]
    wrow = lax.broadcasted_iota(I32, (1, w_buf), 1)
    wlo = max(w_buf - WINDOW + 1, w_buf - past_len, 0)
    wmask = wrow >= wlo
    qg = jnp.concatenate([q[:, HEAD_DIM * r:HEAD_DIM * (r + 1)] for r in range(B_GROUP)] * 2,
                         axis=0)
    kn = _pick_head(kn_ref[0][:, LANES:2 * LANES], g)
    vn = _pick_head(vn_ref[0][:, LANES:2 * LANES], g)
    s_new = jnp.sum(qg * kn, axis=1, keepdims=True)
    ss = []
    halves = []
    for t in range(n_sel):
        blk = sel_ref[b, g * n_sel + t]
        valid = blk < nb_past
        halves.append(blk % (PAGE // CMP_BLOCK))
        s = _mm(qg, _pick_head(k_refs[t][0], halves[t]), HI)
        ss.append(jnp.where(valid, s, NEG_BIG))
    m = s_new
    for s in ss:
        m = jnp.maximum(m, jnp.max(s, axis=1, keepdims=True))
    l = jnp.exp(s_new - m)
    acc = l * vn
    for t in range(n_sel):
        p = jnp.exp(ss[t] - m)
        l = l + jnp.sum(p, axis=1, keepdims=True)
        acc = acc + _nt(p, _pick_head(v_refs[t][0], halves[t]), HI)
    o_slc = acc / l
    kwn = _pick_head(wn_ref[0][:, 0:LANES], g)
    vwn = _pick_head(wn_ref[0][:, LANES:2 * LANES], g)
    sw_new = jnp.sum(qg * kwn, axis=1, keepdims=True)
    sw = jnp.where(wmask, _mm(qg, kwin, HI), NEG_BIG)
    mw = jnp.maximum(sw_new, jnp.max(sw, axis=1, keepdims=True))
    pw = jnp.exp(sw - mw)
    pn = jnp.exp(sw_new - mw)
    lw = pn + jnp.sum(pw, axis=1, keepdims=True)
    o_win = (pn * vwn + _nt(pw, vwin, HI)) / lw
    heads = []
    for r in range(B_GROUP):
        def gate(c, r=r):
            return jnp.where(g == 0, gates[:, 8 * c + r:8 * c + r + 1],
                             gates[:, 8 * c + B_GROUP + r:8 * c + B_GROUP + r + 1])
        heads.append(gate(0) * ocmp[r:r + 1] + gate(1) * o_slc[r:r + 1] + gate(2) * o_win[r:r + 1])
    o_ref[0] = jnp.broadcast_to(jnp.concatenate(heads, axis=1), (8, 256))


def _nsa_decode_mix(page_table, sel2, q, gates, ocmp, nk_new, nv_new, win_new, state_t,
                    cache_kt, cache_vt, past_len, n_sel):
    nb = page_table.shape[0]
    w_buf = state_t.shape[2]
    nb_past = past_len // CMP_BLOCK
    per_page = PAGE // CMP_BLOCK
    per_b = lambda r, w: pl.BlockSpec((1, r, w), lambda b, g, pt, sl: (b, 0, 0))

    def blk_spec(t):
        def imap(b, g, pt, sl):
            blk = jnp.minimum(sl[b, g * n_sel + t], nb_past - 1)
            return (pt[b, blk // per_page], B_KV_HEADS + g, 0)
        return pl.BlockSpec((1, HEAD_DIM, PAGE), imap)

    kern = functools.partial(_nsa_decode_mix_kernel, past_len=past_len, n_sel=n_sel)
    out = pl.pallas_call(
        kern,
        out_shape=jax.ShapeDtypeStruct((nb, 8, 512), F32),
        grid_spec=pltpu.PrefetchScalarGridSpec(
            num_scalar_prefetch=2,
            grid=(nb, B_KV_HEADS),
            in_specs=[pl.BlockSpec((1, 1, 256), lambda b, g, pt, sl: (b, 0, g)),
                      per_b(1, LANES),
                      pl.BlockSpec((1, 1, B_GROUP, HEAD_DIM), lambda b, g, pt, sl: (b, g, 0, 0)),
                      per_b(1, 256), per_b(1, 256), per_b(1, 256),
                      pl.BlockSpec((1, HEAD_DIM, w_buf), lambda b, g, pt, sl: (b, g, 0)),
                      pl.BlockSpec((1, HEAD_DIM, w_buf), lambda b, g, pt, sl: (b, B_KV_HEADS + g, 0))]
                     + [blk_spec(t) for t in range(n_sel)] * 2,
            out_specs=pl.BlockSpec((1, 8, 256), lambda b, g, pt, sl: (b, 0, g))),
        compiler_params=_cparams(("arbitrary", "arbitrary")),
        name="nsa_sample_mix",
    )(page_table, sel2, q, gates, ocmp, nk_new, nv_new, win_new, state_t, state_t,
      *([cache_kt] * n_sel), *([cache_vt] * n_sel))
    return out[:, 0, :]


def _moe(h2_list, ids_list, gates_list, x1_list, gt_list, prm, tt, tm):
    d = h2_list[0].shape[-1]
    n_group = [h.shape[0] * h.shape[1] for h in h2_list]
    n_tok = sum(n_group)
    nt = -(-n_tok // tt)
    ntp = nt * tt
    ids = jnp.concatenate(ids_list, axis=1)[:TOP_K]
    ids = jnp.pad(ids, ((0, 8 - TOP_K), (0, ntp - n_tok)), constant_values=N_EXPERTS)
    ids3 = ids.reshape(8, nt, tt).transpose(1, 0, 2)
    pos3, cnt = _route_positions(ids3, tt)
    counts = cnt[:, 0].astype(I32)
    padded = (counts + tm - 1) // tm * tm
    ends_p = jnp.cumsum(padded)
    starts_p = ends_p - padded
    na = n_tok * TOP_K
    nblk = -(-(na + N_EXPERTS * (tm - 1)) // tm)
    n_rows = nblk * tm
    ids4 = ids3[:, :TOP_K, :]
    valid = ids4 < N_EXPERTS
    e_ar = jnp.arange(N_EXPERTS, dtype=I32)
    start_of = jnp.sum(jnp.where(ids4[..., None] == e_ar, starts_p, 0), axis=-1)
    dest = start_of + pos3[:, :TOP_K, :]
    tok_id = (jnp.arange(nt, dtype=I32)[:, None, None] * tt + jnp.arange(tt, dtype=I32)[None, None, :])
    pad_rank = (tok_id - n_tok) * TOP_K + jnp.arange(TOP_K, dtype=I32)[None, :, None]
    dest_scatter = jnp.where(valid, dest, n_rows + pad_rank).reshape(nt, TOP_K * tt)
    dest_gather = jnp.where(valid, dest, 0).reshape(nt, TOP_K * tt)
    n_trash = (ntp - n_tok) * TOP_K
    blk_start = jnp.arange(nblk, dtype=I32) * tm
    blk_e = jnp.minimum(jnp.sum(jnp.where(ends_p[None, :] <= blk_start[:, None], 1, 0), axis=1),
                        N_EXPERTS - 1).astype(I32)
    n_used = (ends_p[-1:] // tm).astype(I32)

    h_all = jnp.concatenate([h.reshape(-1, d) for h in h2_list], axis=0)
    h_all = jnp.pad(h_all, ((0, ntp - n_tok), (0, 0)))
    zero_starts = jnp.where(padded > 0, ends_p - tm, n_rows - tm).astype(I32)
    xs = _scatter_rows(zero_starts, n_used, dest_scatter, h_all, n_rows + max(n_trash, 8), nblk, tt, tm)
    y_rows = _expert_matmul(blk_e, n_used, xs, prm["w_gu"], prm["b_gu"], prm["w_dn"], prm["b_dn"],
                            n_rows, tm)

    gates = jnp.concatenate(gates_list, axis=1)[:TOP_K]
    gates = jnp.pad(gates, ((0, 0), (0, ntp - n_tok)))
    gate_t = gates.reshape(TOP_K, nt, tt).transpose(1, 2, 0)
    outs = []
    tok0 = 0
    for x1, gt, n in zip(x1_list, gt_list, n_group):
        bx, sx, _ = x1.shape
        tile0 = tok0 // tt
        if sx % tt:
            padn = tt - sx
            x1p = jnp.pad(x1, ((0, 0), (0, padn), (0, 0)))
            gtp = jnp.pad(gt, ((0, 0), (0, padn), (0, 0)))
            o = _combine(dest_gather, y_rows, x1p, gtp, gate_t[tile0:tile0 + 1], tile0, tt)[:, :sx]
        else:
            o = _combine(dest_gather, y_rows, x1, gt, gate_t[tile0:tile0 + bx * (sx // tt)], tile0, tt)
        outs.append(o)
        tok0 += n
    return outs


def _flatten_rows(a):
    return a.transpose(1, 0, 2).reshape(8, -1)


def kernel(x_prompt, x_sample, c_prompt, c_sample, cache_diff_k, cache_diff_v, cache_nsa_k, cache_nsa_v, state_win_kv, page_table, attn_norm_g, ffn_norm_g, ada_w, ada_b, w_in, w_out, diff_q_norm_g, diff_k_norm_g, diff_lambda, diff_sub_norm_g, nsa_q_norm_g, nsa_k_norm_g, nsa_ck_norm_g, nsa_cmp_pe, nsa_cmp_w, router_w, router_b, expert_w_gu, expert_b_gu, expert_w_down, expert_b_down):
    depth = w_in.shape[0]
    assert depth == 1, "single-layer trunk"
    bp, sp, d = x_prompt.shape
    bs, ss, _ = x_sample.shape
    assert ss == 1
    n_pages = page_table.shape[1]
    past_len = n_pages * PAGE
    w_buf = state_win_kv.shape[2]
    n_pool = cache_diff_k.shape[1]
    l = 0
    lam_init = 0.8 - 0.6 * math.exp(-0.3 * l)
    d_ff = expert_w_down.shape[2]

    prm = {
        "attn_g": attn_norm_g[l].reshape(1, d),
        "ffn_g": ffn_norm_g[l].reshape(1, d),
        "w_in": jnp.pad(w_in[l], ((0, 0), (0, IN_PAD - w_in.shape[2]))).astype(BF16),
        "gd": jnp.stack([jnp.tile(diff_q_norm_g[l], 16), jnp.tile(diff_k_norm_g[l], 16)]),
        "gn": jnp.tile(nsa_q_norm_g[l], 8).reshape(1, 512),
        "gk": jnp.concatenate([jnp.tile(nsa_k_norm_g[l, 0], 2), jnp.tile(nsa_k_norm_g[l, 1], 2)]).reshape(1, 256),
        "m32": _group_mean_matrix(A_DQK, BF16),
        "m64": _group_mean_matrix(HEAD_DIM, BF16),
        "m32_f32": _group_mean_matrix(A_DQK, F32),
        "m64_f32": _group_mean_matrix(HEAD_DIM, F32),
        "w_out": w_out[l].astype(BF16),
        "router_wt": router_w[l].T,
        "router_b": router_b[l].reshape(N_EXPERTS, 1),
        "w_gu": expert_w_gu[l],
        "b_gu": expert_b_gu[l].reshape(N_EXPERTS, 1, 2 * d_ff),
        "w_dn": expert_w_down[l],
        "b_dn": expert_b_down[l].reshape(N_EXPERTS, 1, d),
    }
    lam_p = diff_lambda[l]
    sub_g2 = jnp.tile(diff_sub_norm_g[l], 2).reshape(1, LANES)
    sub_g8 = jnp.tile(diff_sub_norm_g[l], 8).reshape(1, 512)
    pe = nsa_cmp_pe[l]
    cw = nsa_cmp_w[l]
    ckg = nsa_ck_norm_g[l].reshape(1, HEAD_DIM)

    n_c = bp + bs
    n_cp = -(-n_c // 8) * 8
    c_all = jnp.pad(jnp.concatenate([c_prompt, c_sample], axis=0), ((0, n_cp - n_c), (0, 0)))
    mod = _modulation(c_all, ada_w[l], ada_b[l])
    mod_p = mod[:bp].reshape(bp, 1, 6, d)
    mod_s = mod[bp:n_c].reshape(1, bs, 6, d)
    sh1p, sc1p, gt1p, sh2p, sc2p, gt2p = [mod_p[:, :, i] for i in range(6)]
    sh1s, sc1s, gt1s, sh2s, sc2s, gt2s = [mod_s[:, :, i] for i in range(6)]

    ts = min(512, sp)
    pos_p = jnp.arange(sp, dtype=I32)
    (qa, dk, dkb, dv, dvb, qn, nk, nv, win, kb, vb, gate) = _in_projection(
        x_prompt, sh1p, sc1p, pos_p, prm, ts)
    tq = min(512, sp)
    o_a = _diff_attention_prompt(qa, dkb, dvb, lam_p, sub_g2, lam_init, tq, min(512, sp))
    tqn = min(512, sp)
    o_b = _nsa_prompt(qn, gate, nk, nv, kb, vb, pe, cw, ckg, tqn, min(512, sp))
    x1p, h2p, idsp, gatesp = _out_projection(o_a, o_b, x_prompt, gt1p, sh2p, sc2p, prm, ts)

    xs_ = x_sample.reshape(1, bs, d)
    pos_s = jnp.full((bs,), past_len, I32)
    (qa_s, dk_s, _, dv_s, _, qn_s, nk_s, nv_s, win_s, _, _, gate_s) = _in_projection(
        xs_, sh1s, sc1s, pos_s, prm, bs, sample=True)
    ckt = jnp.transpose(cache_diff_k[l], (0, 2, 3, 1)).reshape(n_pool, 512, PAGE)
    cvt = jnp.transpose(cache_diff_v[l], (0, 2, 3, 1)).reshape(n_pool, 512, PAGE)
    nkt = jnp.transpose(cache_nsa_k[l], (0, 2, 3, 4, 1)).reshape(n_pool, 256, PAGE)
    nvt = jnp.transpose(cache_nsa_v[l], (0, 2, 3, 4, 1)).reshape(n_pool, 256, PAGE)
    state = state_win_kv[l].reshape(bs, w_buf, 256)
    state_t = jnp.transpose(state_win_kv[l], (0, 2, 3, 4, 1)).reshape(bs, 256, w_buf)
    pps = 8 if n_pages % 8 == 0 else 1
    as3 = lambda a: a.reshape(bs, 1, a.shape[-1])
    col = lambda a: a.reshape(bs, a.shape[-1], 1)
    o_a_s = _diff_attention_sample(page_table, lam_p, col(qa_s[0]), col(dk_s[0]), col(dv_s[0]),
                                   diff_sub_norm_g[l].reshape(HEAD_DIM, 1), ckt, cvt, lam_init, pps)
    ocmp_s, sel_s = _nsa_decode_cmp(page_table, as3(qn_s[0]), col(nk_s[0][:, :LANES]),
                                    col(nv_s[0][:, :LANES]), jnp.transpose(pe, (0, 2, 1)),
                                    jnp.transpose(cw, (0, 2, 1)), ckg.reshape(HEAD_DIM, 1),
                                    nkt, nvt, past_len, pps)
    n_sel = min(N_SELECT, past_len // CMP_BLOCK + 1)
    sel2 = jnp.concatenate([sel_s[:, 0, :n_sel], sel_s[:, 1, :n_sel]], axis=1)
    o_b_s = _nsa_decode_mix(page_table, sel2, as3(qn_s[0]), as3(gate_s[0]),
                            ocmp_s.reshape(bs, B_KV_HEADS, B_GROUP, HEAD_DIM), as3(nk_s[0]), as3(nv_s[0]),
                            as3(win_s[0]), state_t, nkt, nvt, past_len, n_sel)
    x1s, h2s, idss, gatess = _out_projection(o_a_s.astype(BF16).reshape(1, bs, 512),
                                             o_b_s.astype(BF16).reshape(1, bs, 512),
                                             xs_, gt1s, sh2s, sc2s, prm, bs)

    y_p, y_s = _moe([h2p, h2s], [_flatten_rows(idsp), _flatten_rows(idss)],
                    [_flatten_rows(gatesp), _flatten_rows(gatess)], [x1p, x1s], [gt2p, gt2s],
                    prm, 256, 256)

    p_win = win[:, sp - w_buf:] if sp >= w_buf else jnp.pad(win, ((0, 0), (w_buf - sp, 0), (0, 0)))
    s_win = jnp.concatenate([state, win_s[0][:, None, :]], axis=1)[:, 1:]
    return (y_p, y_s.reshape(bs, 1, d),
            dk.reshape(1, bp, sp, A_HEADS, 2 * A_DQK), dv.reshape(1, bp, sp, A_HEADS, HEAD_DIM),
            nk.reshape(1, bp, sp, 2, B_KV_HEADS, HEAD_DIM), nv.reshape(1, bp, sp, 2, B_KV_HEADS, HEAD_DIM),
            p_win.reshape(1, bp, w_buf, 2, B_KV_HEADS, HEAD_DIM),
            dk_s.reshape(1, bs, 1, A_HEADS, 2 * A_DQK), dv_s.reshape(1, bs, 1, A_HEADS, HEAD_DIM),
            nk_s.reshape(1, bs, 1, 2, B_KV_HEADS, HEAD_DIM), nv_s.reshape(1, bs, 1, 2, B_KV_HEADS, HEAD_DIM),
            s_win.reshape(1, bs, w_buf, 2, B_KV_HEADS, HEAD_DIM))
```

```python
import functools
import math

import jax
import jax.numpy as jnp
from jax import lax
from jax.experimental import pallas as pl
from jax.experimental.pallas import tpu as pltpu

F32 = jnp.float32
BF16 = jnp.bfloat16
I32 = jnp.int32
HI = lax.Precision.HIGHEST

HEAD_DIM = 64
A_HEADS = 8
A_DQK = 32
B_HEADS = 8
B_KV_HEADS = 2
B_GROUP = 4
CMP_BLOCK = 64
N_SELECT = 16
WINDOW = 512
ROPE_THETA = 500000.0
ROPE_FRACTION = 4
N_EXPERTS = 32
TOP_K = 4
SWIGLU_LIMIT = 7.0
SWIGLU_ALPHA = 1.702
EPS = 1e-6
NEG_BIG = -1e30
FORCE_SCORE = 1e4
PAGE = 128

A_Q = 512
IN_PAD = 2944
LANES = 128
STRIP = 32
LOG2E = 1.4426950408889634
VMEM_LIMIT = 56 * 1024 * 1024
VMEM_LIMIT_LARGE = 60 * 1024 * 1024


def _cparams(sem, vmem=VMEM_LIMIT):
    return pltpu.CompilerParams(dimension_semantics=sem, vmem_limit_bytes=vmem)


def _nt(a, b, precision=None):
    return lax.dot_general(a, b, (((1,), (1,)), ((), ())),
                           preferred_element_type=F32, precision=precision)


def _mm(a, b, precision=None):
    return jnp.dot(a, b, preferred_element_type=F32, precision=precision)


def _mod_kernel(c_ref, w_ref, b_ref, o_ref):
    c = c_ref[...]
    s = c / (1.0 + jnp.exp(-c))
    o_ref[...] = _mm(s.astype(BF16), w_ref[...].astype(BF16)) + b_ref[...]


def _modulation(c_all, ada_w, ada_b):
    n, d = c_all.shape
    width = ada_w.shape[1]
    tn = 1024
    return pl.pallas_call(
        _mod_kernel,
        out_shape=jax.ShapeDtypeStruct((n, width), F32),
        grid=(width // tn,),
        in_specs=[pl.BlockSpec((n, d), lambda j: (0, 0)),
                  pl.BlockSpec((d, tn), lambda j: (0, j)),
                  pl.BlockSpec((1, tn), lambda j: (0, j))],
        out_specs=pl.BlockSpec((n, tn), lambda j: (0, j)),
        compiler_params=_cparams(("arbitrary",)),
        name="adaln_mod",
    )(c_all, ada_w, ada_b.reshape(1, width))


def _group_norm(seg, gmat, gvec):
    out = []
    for j in range(seg.shape[1] // 256):
        c = seg[:, 256 * j:256 * (j + 1)]
        if gmat.dtype == F32:
            ms = _mm(c * c, gmat, HI)
        else:
            ms = _mm((c * c).astype(BF16), gmat)
        out.append(c * lax.rsqrt(ms + EPS) * gvec[:, 256 * j:256 * (j + 1)])
    return out[0] if len(out) == 1 else jnp.concatenate(out, axis=1)


def _rope_lanes(seg, tab_ref, half):
    w = seg.shape[1]
    rep = w // LANES
    cos = jnp.concatenate([tab_ref[0]] * rep, axis=1)
    s_up = jnp.concatenate([tab_ref[1]] * rep, axis=1)
    s_dn = jnp.concatenate([tab_ref[2]] * rep, axis=1)
    return (seg * cos + pltpu.roll(seg, w - half, 1) * s_up
            + pltpu.roll(seg, half, 1) * s_dn)


def _inproj_kernel(x_ref, sh_ref, sc_ref, g_ref, w_ref, gd_ref, gn_ref, gk_ref,
                   m32_ref, m64_ref, td_ref, tn_ref,
                   qa_ref, dk_ref, dkb_ref, dv_ref, dvb_ref, qn_ref, nk_ref, nv_ref,
                   win_ref, kb_ref, vb_ref, gate_ref):
    x = x_ref[0]
    ms = jnp.mean(x * x, axis=-1, keepdims=True)
    h = x * lax.rsqrt(ms + EPS) * g_ref[...]
    h = h * (1.0 + sc_ref[0]) + sh_ref[0]
    proj = _mm(h.astype(BF16), w_ref[...])
    m32 = m32_ref[...]
    m64 = m64_ref[...]
    gd = gd_ref[...]

    qa = _rope_lanes(_group_norm(proj[:, 0:512], m32, gd[0:1]), td_ref, 4)
    q_scale = A_DQK ** -0.5 * LOG2E if qa_ref.dtype == BF16 else 1.0
    qa_ref[0] = (qa * q_scale).astype(qa_ref.dtype)
    ka = _rope_lanes(_group_norm(proj[:, 512:1024], m32, gd[1:2]), td_ref, 4)
    dk_ref[0] = ka
    dkb_ref[0] = ka.astype(BF16)
    va = proj[:, 1024:1536]
    dv_ref[0] = va
    dvb_ref[0] = va.astype(BF16)
    qn = _rope_lanes(_group_norm(proj[:, 1536:2048], m64, gn_ref[...]), tn_ref, 8)
    qn_ref[0] = qn * (HEAD_DIM ** -0.5)
    k_cmp = _rope_lanes(proj[:, 2048:2176], tn_ref, 8)
    k_sw = _rope_lanes(_group_norm(proj[:, 2176:2432], m64, gk_ref[...]), tn_ref, 8)
    vb = proj[:, 2432:2816]
    nk_ref[0] = jnp.concatenate([k_cmp, k_sw[:, 0:128]], axis=1)
    nv_ref[0] = vb[:, 0:256]
    win_ref[0] = jnp.concatenate([k_sw[:, 128:256], vb[:, 256:384]], axis=1)
    kb_ref[0] = jnp.concatenate([k_cmp, k_sw], axis=1).astype(BF16)
    vb_ref[0] = vb.astype(BF16)
    gl = proj[:, 2816:2944]
    gate_ref[0] = 1.0 / (1.0 + jnp.exp(-gl))


def _rope_tables(pos, group, half):
    inv = ROPE_THETA ** (-jnp.arange(half, dtype=F32) / half)
    ang = pos.astype(F32)[:, None] * inv[None, :]
    cos, sin = jnp.cos(ang), jnp.sin(ang)
    n = pos.shape[0]
    pad = group - 2 * half
    c = jnp.concatenate([cos, cos, jnp.ones((n, pad), F32)], axis=1)
    up = jnp.concatenate([-sin, jnp.zeros((n, half + pad), F32)], axis=1)
    dn = jnp.concatenate([jnp.zeros((n, half), F32), sin, jnp.zeros((n, pad), F32)], axis=1)
    rep = LANES // group
    return jnp.stack([jnp.tile(c, (1, rep)), jnp.tile(up, (1, rep)), jnp.tile(dn, (1, rep))])


def _group_mean_matrix(group, dtype):
    i = jnp.arange(256)
    return jnp.where((i[:, None] // group) == (i[None, :] // group), 1.0 / group, 0.0).astype(dtype)


def _in_projection(x, sh, sc, pos, prm, ts, sample=False):
    bx, sx, d = x.shape
    r = sh.shape[1]
    rb = 1 if r == 1 else ts
    td = _rope_tables(pos, A_DQK, A_DQK // ROPE_FRACTION // 2)
    tn = _rope_tables(pos, HEAD_DIM, HEAD_DIM // ROPE_FRACTION // 2)

    def tok(width, dtype):
        return (jax.ShapeDtypeStruct((bx, sx, width), dtype),
                pl.BlockSpec((1, ts, width), lambda s, b: (b, s, 0)))

    sfx = "_f32" if sample else ""
    outs = [tok(512, F32 if sample else BF16), tok(512, F32), tok(512, BF16), tok(512, F32), tok(512, BF16),
            tok(512, F32), tok(256, F32), tok(256, F32), tok(256, F32), tok(384, BF16),
            tok(384, BF16), tok(128, F32)]
    const = lambda shape: pl.BlockSpec(shape, lambda s, b: (0,) * len(shape))
    mod_spec = pl.BlockSpec((1, rb, d), (lambda s, b: (b, 0, 0)) if r == 1 else (lambda s, b: (b, s, 0)))
    return pl.pallas_call(
        _inproj_kernel,
        out_shape=[o[0] for o in outs],
        grid=(sx // ts, bx),
        in_specs=[pl.BlockSpec((1, ts, d), lambda s, b: (b, s, 0)), mod_spec, mod_spec,
                  const((1, d)), const((d, IN_PAD)), const((2, 512)), const((1, 512)),
                  const((1, 256)), const((256, 256)), const((256, 256)),
                  pl.BlockSpec((3, ts, LANES), lambda s, b: (0, s, 0)),
                  pl.BlockSpec((3, ts, LANES), lambda s, b: (0, s, 0))],
        out_specs=[o[1] for o in outs],
        compiler_params=_cparams(("arbitrary", "arbitrary")),
        name="in_projection",
    )(x, sh, sc, prm["attn_g"], prm["w_in"], prm["gd"], prm["gn"], prm["gk"],
      prm["m32" + sfx], prm["m64" + sfx], td, tn)


def _diff_lambda(lam_ref, lam_init):
    lp = lam_ref[...]
    a = jnp.sum(lp[0:1] * lp[1:2], axis=1, keepdims=True)
    b = jnp.sum(lp[2:3] * lp[3:4], axis=1, keepdims=True)
    return jnp.exp(a) - jnp.exp(b) + lam_init


def _head_rms(o, sg):
    lane = lax.broadcasted_iota(I32, (1, LANES), 1)
    sq = o * o
    s0 = jnp.sum(jnp.where(lane < HEAD_DIM, sq, 0.0), axis=1, keepdims=True)
    s1 = jnp.sum(jnp.where(lane >= HEAD_DIM, sq, 0.0), axis=1, keepdims=True)
    ms = jnp.where(lane < HEAD_DIM, s0, s1) * (1.0 / HEAD_DIM)
    return o * lax.rsqrt(ms + EPS) * sg


def _diff_attn_kernel(lam_ref, q_ref, k_ref, v_ref, sg_ref, o_ref, *, tq, tk, lam_init):
    qi = pl.program_id(2)
    q = q_ref[0]
    lane = lax.broadcasted_iota(I32, (1, LANES), 1)
    lam = _diff_lambda(lam_ref, lam_init)
    zero = jnp.zeros_like(q)
    qm = [jnp.where((lane >= A_DQK * i) & (lane < A_DQK * (i + 1)), q, zero) for i in range(4)]
    first_head = lane < HEAD_DIM
    one = jnp.ones((tk, LANES), BF16)
    q0 = qi * tq

    def chunk(j, carry, masked):
        ms, accs = carry
        start = pl.multiple_of(j * tk, tk)
        kc = k_ref[0, pl.ds(start, tk), :]
        vc = v_ref[0, pl.ds(start, tk), :]
        vaug = (jnp.where(first_head, vc, one), jnp.where(first_head, one, vc))
        ss = [_nt(qm[i], kc) for i in range(4)]
        new_ms = [[] for _ in range(4)]
        alphas = [[] for _ in range(4)]
        ps = [[] for _ in range(4)]
        for r in range(tq // STRIP):
            rows = slice(r * STRIP, (r + 1) * STRIP)
            if masked:
                row = q0 + r * STRIP + lax.broadcasted_iota(I32, (STRIP, tk), 0)
                causal = start + lax.broadcasted_iota(I32, (STRIP, tk), 1) <= row
            for i in range(4):
                s = ss[i][rows]
                if masked:
                    s = jnp.where(causal, s, NEG_BIG)
                m_old = ms[i][rows]
                m_new = jnp.maximum(m_old, jnp.max(s, axis=1, keepdims=True))
                ps[i].append(jnp.exp2(s - jnp.concatenate([m_new] * (tk // LANES), axis=1)).astype(BF16))
                alphas[i].append(jnp.exp2(m_old - m_new))
                new_ms[i].append(m_new)
        new_accs = []
        for i in range(4):
            pv = _mm(jnp.concatenate(ps[i], axis=0), vaug[i // 2])
            new_accs.append(jnp.concatenate(alphas[i], axis=0) * accs[i] + pv)
        return tuple(jnp.concatenate(m, axis=0) for m in new_ms), tuple(new_accs)

    init = (tuple(jnp.full((tq, LANES), NEG_BIG, F32) for _ in range(4)),
            tuple(jnp.zeros((tq, LANES), F32) for _ in range(4)))
    n_full = q0 // tk
    carry = lax.fori_loop(0, n_full, lambda j, c: chunk(j, c, False), init)
    _, accs = chunk(n_full, carry, True)
    outs = [a / pltpu.roll(a, HEAD_DIM, 1) for a in accs]
    o0 = jnp.where(first_head, outs[0], outs[2])
    o1 = jnp.where(first_head, outs[1], outs[3])
    o = o0 - lam * o1
    o_ref[0] = (_head_rms(o, sg_ref[...]) * (1.0 - lam_init)).astype(BF16)


def _diff_attention_prompt(qa, dkb, dvb, lam_p, sub_g2, lam_init, tq, tk):
    b, s, _ = qa.shape
    assert tk % tq == 0 and s % tk == 0
    kern = functools.partial(_diff_attn_kernel, tq=tq, tk=tk, lam_init=lam_init)
    return pl.pallas_call(
        kern,
        out_shape=jax.ShapeDtypeStruct((b, s, 512), BF16),
        grid=(b, A_HEADS // 2, s // tq),
        in_specs=[pl.BlockSpec((4, A_DQK), lambda b_, p, i: (0, 0)),
                  pl.BlockSpec((1, tq, LANES), lambda b_, p, i: (b_, i, p)),
                  pl.BlockSpec((1, s, LANES), lambda b_, p, i: (b_, 0, p)),
                  pl.BlockSpec((1, s, LANES), lambda b_, p, i: (b_, 0, p)),
                  pl.BlockSpec((1, LANES), lambda b_, p, i: (0, 0))],
        out_specs=pl.BlockSpec((1, tq, LANES), lambda b_, p, i: (b_, i, p)),
        compiler_params=_cparams(("arbitrary", "arbitrary", "arbitrary")),
        name="diff_attention_prompt",
    )(lam_p, qa, dkb, dvb, sub_g2)


def _top_select_t(score_t, n_sel):
    nb = score_t.shape[0]
    blk = lax.broadcasted_iota(I32, (nb, 1), 0)
    cnt = jnp.zeros(score_t.shape, F32)
    for i in range(nb):
        row = score_t[i:i + 1, :]
        gt = jnp.where(row > score_t, 1.0, 0.0)
        eq = jnp.where(row == score_t, 1.0, 0.0)
        cnt = cnt + gt + jnp.where(blk > i, eq, 0.0)
    return jnp.where(cnt < n_sel, 1.0, 0.0)


def _strip_softmax(s, m_old, mask_fn):
    m_new, alpha, ps = [], [], []
    for r in range(s.shape[0] // STRIP):
        rows = slice(r * STRIP, (r + 1) * STRIP)
        sr = s[rows] if mask_fn is None else mask_fn(s[rows], rows)
        mn = jnp.broadcast_to(jnp.max(sr, axis=1, keepdims=True), (STRIP, LANES))
        if m_old is not None:
            mn = jnp.maximum(m_old[rows], mn)
            alpha.append(jnp.exp2(m_old[rows] - mn))
        ps.append(jnp.exp2(sr - jnp.concatenate([mn] * (s.shape[1] // LANES), axis=1)).astype(BF16))
        m_new.append(mn)
    cat = lambda xs: jnp.concatenate(xs, axis=0)
    return cat(m_new), (cat(alpha) if alpha else None), cat(ps)


def _compress(kc, pe_sum, w, nb):
    mean = (jnp.sum(kc.reshape(nb, CMP_BLOCK, HEAD_DIM), axis=1) + pe_sum) * (1.0 / CMP_BLOCK)
    return _mm(mean, w, HI)


def _nsa_prompt_kernel(q_ref, gate_ref, kcmp_ref, vcmp_ref, kslc_ref, vslc_ref, kwin_ref,
                       vwin_ref, pe_ref, cw_ref, ckg_ref, o_ref,
                       kaug, vs, kw, vw, ck, cv, *, tq, tk, seq):
    qi = pl.program_id(1)
    nb = seq // CMP_BLOCK
    n_sel = min(N_SELECT, nb)
    wk = WINDOW + tq

    @pl.when(qi == 0)
    def _():
        rowblk = lax.broadcasted_iota(I32, (seq, HEAD_DIM), 0) // CMP_BLOCK
        colblk = lax.broadcasted_iota(I32, (seq, HEAD_DIM), 1)
        onehot = jnp.where(rowblk == colblk, 1.0, 0.0).astype(BF16)
        pe_k = jnp.sum(pe_ref[0], axis=0, keepdims=True)
        pe_v = jnp.sum(pe_ref[1], axis=0, keepdims=True)
        ones = jnp.ones((seq, HEAD_DIM), BF16)
        for g in range(B_KV_HEADS):
            lo, hi = HEAD_DIM * g, HEAD_DIM * (g + 1)
            kaug[g] = jnp.concatenate([kslc_ref[0][:, lo:hi], onehot], axis=1)
            vs[g] = jnp.concatenate([vslc_ref[0][:, lo:hi], ones], axis=1)
            kw[g] = kwin_ref[0][:, lo:hi]
            vw[g] = jnp.concatenate([vwin_ref[0][:, lo:hi], ones], axis=1)
            c = _compress(kcmp_ref[0][:, lo:hi], pe_k, cw_ref[0], nb)
            ms = jnp.mean(c * c, axis=-1, keepdims=True)
            ck[g] = c * lax.rsqrt(ms + EPS) * ckg_ref[...]
            cv[g] = _compress(vcmp_ref[0][:, lo:hi], pe_v, cw_ref[1], nb)

    q = q_ref[0]
    gates = gate_ref[0]
    q0 = qi * tq
    qpos = q0 + lax.broadcasted_iota(I32, (tq, 1), 0)
    blk = lax.broadcasted_iota(I32, (1, nb), 1)
    complete = (blk + 1) * CMP_BLOCK - 1 <= qpos
    cur = qpos // CMP_BLOCK
    forced = blk * (blk - cur) == 0
    qpos4 = jnp.concatenate([qpos] * B_GROUP, axis=0)
    n_full = q0 // tk
    wstart = pl.multiple_of(jnp.maximum(q0 - WINDOW, 0), tq)
    kpos_w = wstart + lax.broadcasted_iota(I32, (1, wk), 1)
    qpos_t = q0 + lax.broadcasted_iota(I32, (1, tq), 1)
    blk_t = lax.broadcasted_iota(I32, (nb, 1), 0)
    complete_t = (blk_t + 1) * CMP_BLOCK - 1 <= qpos_t
    forced_t = blk_t * (blk_t - qpos_t // CMP_BLOCK) == 0
    eye = jnp.where(lax.broadcasted_iota(I32, (tq, tq), 0) == lax.broadcasted_iota(I32, (tq, tq), 1),
                    1.0, 0.0).astype(BF16)
    o_cmp_all, q4_all, qaug_all = [], [], []

    for g in range(B_KV_HEADS):
        qh = [q[:, HEAD_DIM * (B_GROUP * g + r):HEAD_DIM * (B_GROUP * g + r + 1)]
              for r in range(B_GROUP)]
        ckg = ck[g]
        cvg = cv[g]
        o_cmp = []
        imp_t = jnp.zeros((nb, tq), F32)
        for r in range(B_GROUP):
            s = jnp.where(complete, _nt(qh[r], ckg, HI), NEG_BIG)
            e = jnp.where(complete, jnp.exp(s - jnp.max(s, axis=1, keepdims=True)), 0.0)
            p = e / jnp.maximum(jnp.sum(e, axis=1, keepdims=True), 1e-30)
            o_cmp.append(_mm(p, cvg, HI))
            st = jnp.where(complete_t, _nt(ckg, qh[r], HI), NEG_BIG)
            et = jnp.where(complete_t, jnp.exp(st - jnp.max(st, axis=0, keepdims=True)), 0.0)
            imp_t = imp_t + et / jnp.maximum(jnp.sum(et, axis=0, keepdims=True), 1e-30)
        score_t = jnp.where(forced_t, FORCE_SCORE, jnp.where(complete_t, imp_t, -1.0))
        sel_t = _top_select_t(score_t, n_sel)
        sel = _nt(eye, sel_t.astype(BF16))
        bias = jnp.where(blk <= cur, jnp.where(sel > 0.5, 0.0, NEG_BIG), NEG_BIG).astype(BF16)
        if nb < HEAD_DIM:
            bias = jnp.concatenate([bias, jnp.zeros((tq, HEAD_DIM - nb), BF16)], axis=1)

        q4 = jnp.concatenate([(qh[r] * LOG2E).astype(BF16) for r in range(B_GROUP)], axis=0)
        o_cmp_all.append(o_cmp)
        q4_all.append(q4)
        qaug_all.append(jnp.concatenate([q4, jnp.concatenate([bias] * B_GROUP, axis=0)], axis=1))

    def chunk(j, carry, masked):
        start = pl.multiple_of(j * tk, tk)
        mask_fn = None
        if masked:
            kpos = start + lax.broadcasted_iota(I32, (1, tk), 1)
            mask_fn = lambda sr, rows: jnp.where(kpos <= qpos4[rows], sr, NEG_BIG)
        out = []
        for g in range(B_KV_HEADS):
            m, acc = carry[g]
            s = _nt(qaug_all[g], kaug[g, pl.ds(start, tk), :])
            m_new, alpha, p = _strip_softmax(s, m, mask_fn)
            out.append((m_new, alpha * acc + _mm(p, vs[g, pl.ds(start, tk), :])))
        return tuple(out)

    init = tuple((jnp.full((B_GROUP * tq, LANES), NEG_BIG, F32), jnp.zeros((B_GROUP * tq, LANES), F32))
                 for _ in range(B_KV_HEADS))
    carry = lax.fori_loop(0, n_full, lambda j, c: chunk(j, c, False), init)
    slc = chunk(n_full, carry, True)

    def in_window(sr, rows):
        dlt = qpos4[rows] - kpos_w
        return jnp.where(dlt >= 0, jnp.where(dlt < WINDOW, sr, NEG_BIG), NEG_BIG)

    heads_out = []
    for g in range(B_KV_HEADS):
        acc = slc[g][1]
        o_slc = acc[:, 0:HEAD_DIM] / acc[:, HEAD_DIM:LANES]
        sw = _nt(q4_all[g], kw[g, pl.ds(wstart, wk), :])
        _, _, pw = _strip_softmax(sw, None, in_window)
        accw = _mm(pw, vw[g, pl.ds(wstart, wk), :])
        o_win = accw[:, 0:HEAD_DIM] / accw[:, HEAD_DIM:LANES]
        for r in range(B_GROUP):
            h = B_GROUP * g + r
            rows = slice(r * tq, (r + 1) * tq)
            heads_out.append(gates[:, h:h + 1] * o_cmp_all[g][r]
                             + gates[:, 8 + h:9 + h] * o_slc[rows]
                             + gates[:, 16 + h:17 + h] * o_win[rows])
    o_ref[0] = jnp.concatenate(heads_out, axis=1).astype(BF16)


def _nsa_prompt(qn, gates, nk, nv, kb, vb, pe, cw, ckg, tq, tk):
    b, s, _ = qn.shape
    nb = s // CMP_BLOCK
    kern = functools.partial(_nsa_prompt_kernel, tq=tq, tk=tk, seq=s)
    full = lambda lane_blk: pl.BlockSpec((1, s, LANES), lambda b_, i: (b_, 0, lane_blk))
    const = lambda shape: pl.BlockSpec(shape, lambda b_, i: (0,) * len(shape))
    return pl.pallas_call(
        kern,
        out_shape=jax.ShapeDtypeStruct((b, s, 512), BF16),
        grid=(b, s // tq),
        in_specs=[pl.BlockSpec((1, tq, 512), lambda b_, i: (b_, i, 0)),
                  pl.BlockSpec((1, tq, LANES), lambda b_, i: (b_, i, 0)),
                  full(0), full(0), full(1), full(1), full(2), full(2),
                  const((2, CMP_BLOCK, HEAD_DIM)), const((2, HEAD_DIM, HEAD_DIM)),
                  const((1, HEAD_DIM))],
        out_specs=pl.BlockSpec((1, tq, 512), lambda b_, i: (b_, i, 0)),
        scratch_shapes=[pltpu.VMEM((2, s, LANES), BF16), pltpu.VMEM((2, s, LANES), BF16),
                        pltpu.VMEM((2, s, HEAD_DIM), BF16), pltpu.VMEM((2, s, LANES), BF16),
                        pltpu.VMEM((2, nb, HEAD_DIM), F32), pltpu.VMEM((2, nb, HEAD_DIM), F32)],
        compiler_params=_cparams(("arbitrary", "arbitrary"), VMEM_LIMIT_LARGE),
        name="nsa_prompt",
    )(qn, gates, nk, nv, kb, vb, kb, vb, pe, cw, ckg)


def _outproj_kernel(oa_ref, ob_ref, x_ref, gt_ref, sh_ref, sc_ref, g_ref, w_ref, rw_ref, rb_ref,
                    x1_ref, h2_ref, ids_ref, gates_ref):
    y = _mm(oa_ref[0], w_ref[0:512, :]) + _mm(ob_ref[0], w_ref[512:1024, :])
    x1 = x_ref[0] + gt_ref[0] * y
    x1_ref[0] = x1
    ms = jnp.mean(x1 * x1, axis=-1, keepdims=True)
    h2 = x1 * lax.rsqrt(ms + EPS) * g_ref[...]
    h2 = h2 * (1.0 + sc_ref[0]) + sh_ref[0]
    h2_ref[0] = h2
    logits = _nt(rw_ref[...].astype(BF16), h2.astype(BF16)) + rb_ref[...]
    eidx = lax.broadcasted_iota(I32, logits.shape, 0)
    work = logits
    vals, ids = [], []
    for _ in range(TOP_K):
        m = jnp.max(work, axis=0, keepdims=True)
        idx = jnp.min(jnp.where(work == m, eidx, N_EXPERTS), axis=0, keepdims=True)
        vals.append(m)
        ids.append(idx)
        work = jnp.where(eidx == idx, -3e38, work)
    es = [jnp.exp(v - vals[0]) for v in vals]
    tot = es[0] + es[1] + es[2] + es[3]
    ids_ref[0] = jnp.concatenate(ids + ids, axis=0)
    gates_ref[0] = jnp.concatenate([e / tot for e in es] * 2, axis=0)


def _out_projection(oa, ob, x, gt, sh, sc, prm, ts):
    bx, sx, d = x.shape
    r = gt.shape[1]
    rb = 1 if r == 1 else ts
    mod_spec = pl.BlockSpec((1, rb, d), (lambda b, s: (b, 0, 0)) if r == 1 else (lambda b, s: (b, s, 0)))
    const = lambda shape: pl.BlockSpec(shape, lambda b, s: (0,) * len(shape))
    tokspec = lambda w: pl.BlockSpec((1, ts, w), lambda b, s: (b, s, 0))
    nt = sx // ts
    return pl.pallas_call(
        _outproj_kernel,
        out_shape=[jax.ShapeDtypeStruct((bx, sx, d), F32), jax.ShapeDtypeStruct((bx, sx, d), F32),
                   jax.ShapeDtypeStruct((bx * nt, 8, ts), I32),
                   jax.ShapeDtypeStruct((bx * nt, 8, ts), F32)],
        grid=(bx, nt),
        in_specs=[tokspec(512), tokspec(512), tokspec(d), mod_spec, mod_spec, mod_spec,
                  const((1, d)), const((d, d)), const((N_EXPERTS, d)), const((N_EXPERTS, 1))],
        out_specs=[tokspec(d), tokspec(d),
                   pl.BlockSpec((1, 8, ts), lambda b, s: (b * nt + s, 0, 0)),
                   pl.BlockSpec((1, 8, ts), lambda b, s: (b * nt + s, 0, 0))],
        compiler_params=_cparams(("arbitrary", "arbitrary")),
        name="out_projection",
    )(oa, ob, x, gt, sh, sc, prm["ffn_g"], prm["w_out"], prm["router_wt"], prm["router_b"])


def _route_kernel(ids_ref, pos_ref, cnt_ref, carry, *, tt):
    @pl.when(pl.program_id(0) == 0)
    def _():
        carry[...] = jnp.zeros_like(carry)

    ids = ids_ref[0]
    e_iota = lax.broadcasted_iota(I32, (N_EXPERTS, tt), 0)
    hits = [ids[k:k + 1, :] == e_iota for k in range(TOP_K)]
    oh = jnp.zeros((N_EXPERTS, tt), F32)
    for k in range(TOP_K):
        oh = oh + jnp.where(hits[k], 1.0, 0.0)
    r = lax.broadcasted_iota(I32, (tt, tt), 0)
    c = lax.broadcasted_iota(I32, (tt, tt), 1)
    upper = jnp.where(r < c, 1.0, 0.0).astype(BF16)
    before = _mm(oh.astype(BF16), upper) + carry[:, 0:1]
    rows = [jnp.sum(jnp.where(hits[k], before, 0.0), axis=0, keepdims=True) for k in range(TOP_K)]
    pos_ref[0] = jnp.concatenate(rows + rows, axis=0).astype(I32)
    carry[...] = carry[...] + jnp.sum(oh, axis=1, keepdims=True)
    cnt_ref[...] = carry[...]


def _route_positions(ids3, tt):
    nt = ids3.shape[0]
    return pl.pallas_call(
        functools.partial(_route_kernel, tt=tt),
        out_shape=[jax.ShapeDtypeStruct((nt, 8, tt), I32),
                   jax.ShapeDtypeStruct((N_EXPERTS, LANES), F32)],
        grid=(nt,),
        in_specs=[pl.BlockSpec((1, 8, tt), lambda i: (i, 0, 0))],
        out_specs=[pl.BlockSpec((1, 8, tt), lambda i: (i, 0, 0)),
                   pl.BlockSpec((N_EXPERTS, LANES), lambda i: (0, 0))],
        scratch_shapes=[pltpu.VMEM((N_EXPERTS, LANES), F32)],
        compiler_params=_cparams(("arbitrary",)),
        name="moe_route_positions",
    )(ids3)


def _scatter_kernel(zs_ref, nu_ref, dest_hbm, h_ref, xs_hbm, idx, zbuf, isem, sem, zsem, *, tt, tm, nblk):
    i = pl.program_id(0)

    @pl.when(i == 0)
    def _():
        zbuf[...] = jnp.zeros_like(zbuf)

        def zero_block(row0):
            return pltpu.make_async_copy(zbuf, xs_hbm.at[pl.ds(pl.multiple_of(row0, tm), tm)], zsem)

        for e in range(N_EXPERTS):
            zero_block(zs_ref[e]).start()

        def start_tail(b, _):
            zero_block(b * tm).start()
            return 0

        def wait_tail(b, _):
            zero_block(b * tm).wait()
            return 0

        lax.fori_loop(nu_ref[0], nblk, start_tail, 0)
        for e in range(N_EXPERTS):
            zero_block(zs_ref[e]).wait()
        lax.fori_loop(nu_ref[0], nblk, wait_tail, 0)

    cp = pltpu.make_async_copy(dest_hbm.at[i], idx, isem)
    cp.start()
    cp.wait()

    def issue(t, _):
        for k in range(TOP_K):
            pltpu.make_async_copy(h_ref.at[pl.ds(t, 1)], xs_hbm.at[pl.ds(idx[k * tt + t], 1)],
                                  sem).start()
        return 0

    lax.fori_loop(0, tt, issue, 0, unroll=4)
    for k in range(TOP_K):
        pltpu.make_async_copy(h_ref, xs_hbm.at[pl.ds(0, tt)], sem).wait()


def _scatter_rows(zero_starts, n_used, dest2, h_all, n_rows, nblk, tt, tm):
    nt = dest2.shape[0]
    d = h_all.shape[1]
    return pl.pallas_call(
        functools.partial(_scatter_kernel, tt=tt, tm=tm, nblk=nblk),
        out_shape=jax.ShapeDtypeStruct((n_rows, d), F32),
        grid_spec=pltpu.PrefetchScalarGridSpec(
            num_scalar_prefetch=2,
            grid=(nt,),
            in_specs=[pl.BlockSpec(memory_space=pl.ANY),
                      pl.BlockSpec((tt, d), lambda i, zs, nu: (i, 0))],
            out_specs=pl.BlockSpec(memory_space=pl.ANY),
            scratch_shapes=[pltpu.SMEM((TOP_K * tt,), I32), pltpu.VMEM((tm, d), F32),
                            pltpu.SemaphoreType.DMA, pltpu.SemaphoreType.DMA,
                            pltpu.SemaphoreType.DMA]),
        compiler_params=_cparams(("arbitrary",)),
        name="moe_scatter_rows",
    )(zero_starts, n_used, dest2, h_all)


def _expert_kernel(be_ref, nu_ref, x_ref, wgu_ref, bgu_ref, wdn_ref, bdn_ref, y_ref, wgu_bf, wdn_bf,
                   *, d_ff):
    i = pl.program_id(0)
    last = jnp.minimum(i, nu_ref[0] - 1)

    @pl.when((i == 0) | (be_ref[last] != be_ref[jnp.maximum(last - 1, 0)]))
    def _():
        wgu_bf[...] = wgu_ref[0].astype(BF16)
        wdn_bf[...] = wdn_ref[0].astype(BF16)

    @pl.when(i < nu_ref[0])
    def _():
        gu = _mm(x_ref[...].astype(BF16), wgu_bf[...]) + bgu_ref[0]
        g = jnp.minimum(gu[:, :d_ff], SWIGLU_LIMIT)
        u = jnp.clip(gu[:, d_ff:], -SWIGLU_LIMIT, SWIGLU_LIMIT)
        a = g * (1.0 / (1.0 + jnp.exp(-SWIGLU_ALPHA * g))) * (u + 1.0)
        y_ref[...] = _mm(a.astype(BF16), wdn_bf[...]) + bdn_ref[0]

    @pl.when(pl.program_id(0) >= nu_ref[0])
    def _():
        y_ref[...] = jnp.zeros_like(y_ref)


def _expert_matmul(blk_e, n_used, xs, wgu, bgu, wdn, bdn, n_rows, tm):
    d = xs.shape[1]
    d_ff = wdn.shape[1]
    nblk = n_rows // tm
    row = lambda i, be, nu: (jnp.minimum(i, nu[0] - 1), 0)
    wsel = lambda i, be, nu: (be[jnp.minimum(i, nu[0] - 1)], 0, 0)
    return pl.pallas_call(
        functools.partial(_expert_kernel, d_ff=d_ff),
        out_shape=jax.ShapeDtypeStruct((n_rows, d), F32),
        grid_spec=pltpu.PrefetchScalarGridSpec(
            num_scalar_prefetch=2,
            grid=(nblk,),
            in_specs=[pl.BlockSpec((tm, d), row),
                      pl.BlockSpec((1, d, 2 * d_ff), wsel),
                      pl.BlockSpec((1, 1, 2 * d_ff), wsel),
                      pl.BlockSpec((1, d_ff, d), wsel),
                      pl.BlockSpec((1, 1, d), wsel)],
            out_specs=pl.BlockSpec((tm, d), lambda i, be, nu: (i, 0)),
            scratch_shapes=[pltpu.VMEM((d, 2 * d_ff), BF16), pltpu.VMEM((d_ff, d), BF16)]),
        compiler_params=_cparams(("arbitrary",)),
        name="moe_expert_matmul",
    )(blk_e, n_used, xs, wgu, bgu, wdn, bdn)


def _combine_kernel(dest_hbm, y_hbm, x1_ref, gt_ref, gate_ref, o_ref, idx, buf, isem, sem, *, tt, tile0):
    b = pl.program_id(0)
    s = pl.program_id(1)
    i = tile0 + b * pl.num_programs(1) + s
    cp = pltpu.make_async_copy(dest_hbm.at[i], idx, isem)
    cp.start()
    cp.wait()

    def issue(t, _):
        for k in range(TOP_K):
            pltpu.make_async_copy(y_hbm.at[pl.ds(idx[k * tt + t], 1)], buf.at[k, pl.ds(t, 1)],
                                  sem).start()
        return 0

    lax.fori_loop(0, tt, issue, 0, unroll=4)
    for k in range(TOP_K):
        pltpu.make_async_copy(y_hbm.at[pl.ds(0, tt)], buf.at[k], sem).wait()
    gate = gate_ref[0]
    moe = gate[:, 0:1] * buf[0]
    for k in range(1, TOP_K):
        moe = moe + gate[:, k:k + 1] * buf[k]
    o_ref[0] = x1_ref[0] + gt_ref[0] * moe


def _combine(dest2, y_rows, x1, gt, gate_t, tile0, tt):
    bx, sx, d = x1.shape
    r = gt.shape[1]
    rb = 1 if r == 1 else tt
    nt = sx // tt
    mod_spec = pl.BlockSpec((1, rb, d), (lambda b, s: (b, 0, 0)) if r == 1 else (lambda b, s: (b, s, 0)))
    return pl.pallas_call(
        functools.partial(_combine_kernel, tt=tt, tile0=tile0),
        out_shape=jax.ShapeDtypeStruct((bx, sx, d), F32),
        grid=(bx, nt),
        in_specs=[pl.BlockSpec(memory_space=pl.ANY), pl.BlockSpec(memory_space=pl.ANY),
                  pl.BlockSpec((1, tt, d), lambda b, s: (b, s, 0)), mod_spec,
                  pl.BlockSpec((1, tt, TOP_K), lambda b, s: (b * nt + s, 0, 0))],
        out_specs=pl.BlockSpec((1, tt, d), lambda b, s: (b, s, 0)),
        scratch_shapes=[pltpu.SMEM((TOP_K * tt,), I32), pltpu.VMEM((TOP_K, tt, d), F32),
                        pltpu.SemaphoreType.DMA, pltpu.SemaphoreType.DMA],
        compiler_params=_cparams(("arbitrary", "arbitrary")),
        name="moe_combine",
    )(dest2, y_rows, x1, gt, gate_t)


def _diff_decode_kernel(pt_ref, lam_ref, q_ref, kn_ref, vn_ref, sg_ref, *rest, pps, lam_init):
    k_refs = rest[:pps]
    v_refs = rest[pps:2 * pps]
    o_ref = rest[2 * pps]
    m_s, l_s, a0_s, a1_s = rest[2 * pps + 1:]
    j = pl.program_id(1)
    rnd = lambda x: x.astype(BF16).astype(F32)
    q = rnd(q_ref[0])
    scale = A_DQK ** -0.5
    n_hc = 2 * A_HEADS

    @pl.when(j == 0)
    def _():
        m_s[...] = jnp.full_like(m_s, NEG_BIG)
        l_s[...] = jnp.zeros_like(l_s)
        a0_s[...] = jnp.zeros_like(a0_s)
        a1_s[...] = jnp.zeros_like(a1_s)

    for p in range(pps):
        s = jnp.sum((rnd(k_refs[p][0]) * q).reshape(n_hc, A_DQK, PAGE), axis=1) * scale
        m_old = m_s[...]
        m_new = jnp.maximum(m_old, s)
        pr = jnp.exp(s - m_new)
        alpha = jnp.exp(m_old - m_new)
        m_s[...] = m_new
        l_s[...] = alpha * l_s[...] + pr
        for h in range(A_HEADS):
            rows = pl.ds(HEAD_DIM * h, HEAD_DIM)
            vt = v_refs[p][0, rows, :]
            for c, acc in ((0, a0_s), (1, a1_s)):
                i = 2 * h + c
                acc[rows, :] = alpha[i:i + 1] * acc[rows, :] + pr[i:i + 1] * vt

    @pl.when(j == pl.num_programs(1) - 1)
    def _():
        lam = _diff_lambda(lam_ref, lam_init)
        m = m_s[...]
        s_new = jnp.sum((rnd(kn_ref[0]) * q).reshape(n_hc, A_DQK, 1), axis=1) * scale
        big = jnp.maximum(jnp.max(m, axis=1, keepdims=True), s_new)
        w = jnp.exp(m - big)
        wn = jnp.exp(s_new - big)
        inv = 1.0 / (jnp.sum(l_s[...] * w, axis=1, keepdims=True) + wn)
        sg = sg_ref[...]
        for h in range(A_HEADS):
            rows = pl.ds(HEAD_DIM * h, HEAD_DIM)
            vn = vn_ref[0, rows, :]
            oc = []
            for c, acc in ((0, a0_s), (1, a1_s)):
                i = 2 * h + c
                num = jnp.sum(acc[rows, :] * w[i:i + 1], axis=1, keepdims=True) + wn[i:i + 1] * vn
                oc.append(num * inv[i:i + 1])
            o = oc[0] - lam * oc[1]
            ms = jnp.mean(o * o, axis=0, keepdims=True)
            o_ref[0, rows, :] = o * lax.rsqrt(ms + EPS) * sg * (1.0 - lam_init)


def _diff_attention_sample(page_table, lam_p, q_col, k_new_col, v_new_col, sub_g_col, cache_kt, cache_vt,
                           lam_init, pps):
    nb, n_pages = page_table.shape
    steps = n_pages // pps
    const = lambda shape: pl.BlockSpec(shape, lambda b, j, pt: (0,) * len(shape))
    per_b = pl.BlockSpec((1, 512, 1), lambda b, j, pt: (b, 0, 0))

    def page_spec(p):
        return pl.BlockSpec((1, 512, PAGE), lambda b, j, pt: (pt[b, j * pps + p], 0, 0))

    kern = functools.partial(_diff_decode_kernel, pps=pps, lam_init=lam_init)
    out = pl.pallas_call(
        kern,
        out_shape=jax.ShapeDtypeStruct((nb, 512, 1), F32),
        grid_spec=pltpu.PrefetchScalarGridSpec(
            num_scalar_prefetch=1,
            grid=(nb, steps),
            in_specs=[const((4, A_DQK)), per_b, per_b, per_b, const((HEAD_DIM, 1))]
                     + [page_spec(p) for p in range(pps)] * 2,
            out_specs=pl.BlockSpec((1, 512, 1), lambda b, j, pt: (b, 0, 0)),
            scratch_shapes=[pltpu.VMEM((2 * A_HEADS, PAGE), F32), pltpu.VMEM((2 * A_HEADS, PAGE), F32),
                            pltpu.VMEM((512, PAGE), F32), pltpu.VMEM((512, PAGE), F32)]),
        compiler_params=_cparams(("arbitrary", "arbitrary")),
        name="diff_attention_sample",
    )(page_table, lam_p, q_col, k_new_col, v_new_col, sub_g_col,
      *([cache_kt] * pps), *([cache_vt] * pps))
    return out[:, :, 0]


def _nsa_decode_cmp_kernel(pt_ref, q_ref, kn_ref, vn_ref, pet_ref, cwt_ref, ckg_ref, *rest,
                           pps, n_chunks, past_len):
    k_refs = rest[:pps]
    v_refs = rest[pps:2 * pps]
    ocmp_ref, sel_ref = rest[2 * pps:2 * pps + 2]
    ksum, vsum = rest[2 * pps + 2:]
    j = pl.program_id(1)
    nb_past = past_len // CMP_BLOCK
    per_page = PAGE // CMP_BLOCK
    nbp = n_chunks * LANES
    lane = lax.broadcasted_iota(I32, (1, LANES), 1)

    @pl.when(j == 0)
    def _():
        ksum[...] = jnp.zeros_like(ksum)
        vsum[...] = jnp.zeros_like(vsum)

    base = j * (pps * per_page)
    chunk = base // LANES
    lane0 = base % LANES
    for refs, acc in ((k_refs, ksum), (v_refs, vsum)):
        cur = acc[chunk]
        for p in range(pps):
            pg = refs[p][0]
            for t in range(per_page):
                in_blk = lane // CMP_BLOCK == t
                col = jnp.sum(jnp.where(in_blk, pg, 0.0), axis=1, keepdims=True)
                cur = jnp.where(lane == lane0 + per_page * p + t, col, cur)
        acc[chunk] = cur

    @pl.when(j == pl.num_programs(1) - 1)
    def _():
        q = q_ref[0]
        blk = lax.broadcasted_iota(I32, (1, nbp), 1)
        qpos = past_len
        complete = (blk + 1) * CMP_BLOCK - 1 <= qpos
        cur_blk = qpos // CMP_BLOCK
        forced = blk * (blk - cur_blk) == 0
        is_new = blk == nb_past
        kall = jnp.where(is_new, kn_ref[0], jnp.concatenate([ksum[c] for c in range(n_chunks)], axis=1))
        vall = jnp.where(is_new, vn_ref[0], jnp.concatenate([vsum[c] for c in range(n_chunks)], axis=1))
        pe_k = jnp.sum(pet_ref[0], axis=1, keepdims=True)
        pe_v = jnp.sum(pet_ref[1], axis=1, keepdims=True)
        outs = []
        sel_rows = []
        for g in range(B_KV_HEADS):
            rows = slice(HEAD_DIM * g, HEAD_DIM * (g + 1))
            c = _mm(cwt_ref[0], (kall[rows] + pe_k) * (1.0 / CMP_BLOCK), HI)
            ck = c * lax.rsqrt(jnp.mean(c * c, axis=0, keepdims=True) + EPS) * ckg_ref[...]
            cv = _mm(cwt_ref[1], (vall[rows] + pe_v) * (1.0 / CMP_BLOCK), HI)
            qg = jnp.concatenate(
                [q[:, HEAD_DIM * (B_GROUP * g + r):HEAD_DIM * (B_GROUP * g + r + 1)]
                 for r in range(B_GROUP)] * 2, axis=0)
            s = jnp.where(complete, _mm(qg, ck, HI), NEG_BIG)
            e = jnp.where(complete, jnp.exp(s - jnp.max(s, axis=1, keepdims=True)), 0.0)
            p = e / jnp.maximum(jnp.sum(e, axis=1, keepdims=True), 1e-30)
            outs.append(_nt(p, cv, HI)[0:B_GROUP])
            imp = jnp.sum(p[0:B_GROUP], axis=0, keepdims=True)
            work = jnp.where(forced, FORCE_SCORE, jnp.where(complete, imp, -1.0))
            work = jnp.where(blk <= cur_blk, work, -2.0)
            picked = jnp.zeros((1, LANES), I32)
            for t in range(N_SELECT):
                mx = jnp.max(work, axis=1, keepdims=True)
                first = jnp.min(jnp.where(work == mx, blk, nbp), axis=1, keepdims=True)
                picked = jnp.where(lane == t, first, picked)
                work = jnp.where(blk == first, -3.0, work)
            sel_rows.append(picked)
        ocmp_ref[0] = jnp.concatenate(outs, axis=0)
        sel_ref[0] = jnp.concatenate(sel_rows * 4, axis=0)


def _nsa_decode_cmp(page_table, q, k_new_col, v_new_col, pet, cwt, ckg_col, cache_kt, cache_vt,
                    past_len, pps):
    nb, n_pages = page_table.shape
    steps = n_pages // pps
    n_blocks = past_len // CMP_BLOCK + 1
    n_chunks = -(-n_blocks // LANES)
    assert LANES % (pps * (PAGE // CMP_BLOCK)) == 0 and n_blocks >= N_SELECT
    const = lambda shape: pl.BlockSpec(shape, lambda b, j, pt: (0,) * len(shape))

    def page_spec(p):
        return pl.BlockSpec((1, LANES, PAGE), lambda b, j, pt: (pt[b, j * pps + p], 0, 0))

    kern = functools.partial(_nsa_decode_cmp_kernel, pps=pps, n_chunks=n_chunks, past_len=past_len)
    return pl.pallas_call(
        kern,
        out_shape=[jax.ShapeDtypeStruct((nb, 8, HEAD_DIM), F32),
                   jax.ShapeDtypeStruct((nb, 8, LANES), I32)],
        grid_spec=pltpu.PrefetchScalarGridSpec(
            num_scalar_prefetch=1,
            grid=(nb, steps),
            in_specs=[pl.BlockSpec((1, 1, 512), lambda b, j, pt: (b, 0, 0)),
                      pl.BlockSpec((1, LANES, 1), lambda b, j, pt: (b, 0, 0)),
                      pl.BlockSpec((1, LANES, 1), lambda b, j, pt: (b, 0, 0)),
                      const((2, HEAD_DIM, CMP_BLOCK)), const((2, HEAD_DIM, HEAD_DIM)),
                      const((HEAD_DIM, 1))]
                     + [page_spec(p) for p in range(pps)] * 2,
            out_specs=[pl.BlockSpec((1, 8, HEAD_DIM), lambda b, j, pt: (b, 0, 0)),
                       pl.BlockSpec((1, 8, LANES), lambda b, j, pt: (b, 0, 0))],
            scratch_shapes=[pltpu.VMEM((n_chunks, LANES, LANES), F32),
                            pltpu.VMEM((n_chunks, LANES, LANES), F32)]),
        compiler_params=_cparams(("arbitrary", "arbitrary")),
        name="nsa_sample_compressed",
    )(page_table, q, k_new_col, v_new_col, pet, cwt, ckg_col,
      *([cache_kt] * pps), *([cache_vt] * pps))


def _pick_head(x, g):
    return jnp.where(g == 0, x[:, 0:HEAD_DIM], x[:, HEAD_DIM:LANES])


def _nsa_decode_mix_kernel(pt_ref, sel_ref, q_ref, gate_ref, ocmp_ref, kn_ref, vn_ref, wn_ref,
                           kwin_ref, vwin_ref, *rest, past_len, n_sel):
    k_refs = rest[:n_sel]
    v_refs = rest[n_sel:2 * n_sel]
    o_ref = rest[2 * n_sel]
    b = pl.program_id(0)
    g = pl.program_id(1)
    q = q_ref[0]
    gates = gate_ref[0]
    ocmp = ocmp_ref[0, 0]
    nb_past = past_len // CMP_BLOCK
    kwin = kwin_ref[0]
    vwin = vwin_ref[0]
    w_buf = kwin.shape[1]
    wrow = lax.broadcasted_iota(I32, (1, w_buf), 1)
    wlo = max(w_buf - WINDOW + 1, w_buf - past_len, 0)
    wmask = wrow >= wlo
    qg = jnp.concatenate([q[:, HEAD_DIM * r:HEAD_DIM * (r + 1)] for r in range(B_GROUP)] * 2,
                         axis=0)
    kn = _pick_head(kn_ref[0][:, LANES:2 * LANES], g)
    vn = _pick_head(vn_ref[0][:, LANES:2 * LANES], g)
    s_new = jnp.sum(qg * kn, axis=1, keepdims=True)
    ss = []
    halves = []
    for t in range(n_sel):
        blk = sel_ref[b, g * n_sel + t]
        valid = blk < nb_past
        halves.append(blk % (PAGE // CMP_BLOCK))
        s = _mm(qg, _pick_head(k_refs[t][0], halves[t]), HI)
        ss.append(jnp.where(valid, s, NEG_BIG))
    m = s_new
    for s in ss:
        m = jnp.maximum(m, jnp.max(s, axis=1, keepdims=True))
    l = jnp.exp(s_new - m)
    acc = l * vn
    for t in range(n_sel):
        p = jnp.exp(ss[t] - m)
        l = l + jnp.sum(p, axis=1, keepdims=True)
        acc = acc + _nt(p, _pick_head(v_refs[t][0], halves[t]), HI)
    o_slc = acc / l
    kwn = _pick_head(wn_ref[0][:, 0:LANES], g)
    vwn = _pick_head(wn_ref[0][:, LANES:2 * LANES], g)
    sw_new = jnp.sum(qg * kwn, axis=1, keepdims=True)
    sw = jnp.where(wmask, _mm(qg, kwin, HI), NEG_BIG)
    mw = jnp.maximum(sw_new, jnp.max(sw, axis=1, keepdims=True))
    pw = jnp.exp(sw - mw)
    pn = jnp.exp(sw_new - mw)
    lw = pn + jnp.sum(pw, axis=1, keepdims=True)
    o_win = (pn * vwn + _nt(pw, vwin, HI)) / lw
    heads = []
    for r in range(B_GROUP):
        def gate(c, r=r):
            return jnp.where(g == 0, gates[:, 8 * c + r:8 * c + r + 1],
                             gates[:, 8 * c + B_GROUP + r:8 * c + B_GROUP + r + 1])
        heads.append(gate(0) * ocmp[r:r + 1] + gate(1) * o_slc[r:r + 1] + gate(2) * o_win[r:r + 1])
    o_ref[0] = jnp.broadcast_to(jnp.concatenate(heads, axis=1), (8, 256))


def _nsa_decode_mix(page_table, sel2, q, gates, ocmp, nk_new, nv_new, win_new, state_t,
                    cache_kt, cache_vt, past_len, n_sel):
    nb = page_table.shape[0]
    w_buf = state_t.shape[2]
    nb_past = past_len // CMP_BLOCK
    per_page = PAGE // CMP_BLOCK
    per_b = lambda r, w: pl.BlockSpec((1, r, w), lambda b, g, pt, sl: (b, 0, 0))

    def blk_spec(t):
        def imap(b, g, pt, sl):
            blk = jnp.minimum(sl[b, g * n_sel + t], nb_past - 1)
            return (pt[b, blk // per_page], B_KV_HEADS + g, 0)
        return pl.BlockSpec((1, HEAD_DIM, PAGE), imap)

    kern = functools.partial(_nsa_decode_mix_kernel, past_len=past_len, n_sel=n_sel)
    out = pl.pallas_call(
        kern,
        out_shape=jax.ShapeDtypeStruct((nb, 8, 512), F32),
        grid_spec=pltpu.PrefetchScalarGridSpec(
            num_scalar_prefetch=2,
            grid=(nb, B_KV_HEADS),
            in_specs=[pl.BlockSpec((1, 1, 256), lambda b, g, pt, sl: (b, 0, g)),
                      per_b(1, LANES),
                      pl.BlockSpec((1, 1, B_GROUP, HEAD_DIM), lambda b, g, pt, sl: (b, g, 0, 0)),
                      per_b(1, 256), per_b(1, 256), per_b(1, 256),
                      pl.BlockSpec((1, HEAD_DIM, w_buf), lambda b, g, pt, sl: (b, g, 0)),
                      pl.BlockSpec((1, HEAD_DIM, w_buf), lambda b, g, pt, sl: (b, B_KV_HEADS + g, 0))]
                     + [blk_spec(t) for t in range(n_sel)] * 2,
            out_specs=pl.BlockSpec((1, 8, 256), lambda b, g, pt, sl: (b, 0, g))),
        compiler_params=_cparams(("arbitrary", "arbitrary")),
        name="nsa_sample_mix",
    )(page_table, sel2, q, gates, ocmp, nk_new, nv_new, win_new, state_t, state_t,
      *([cache_kt] * n_sel), *([cache_vt] * n_sel))
    return out[:, 0, :]


def _moe(h2_list, ids_list, gates_list, x1_list, gt_list, prm, tt, tm):
    d = h2_list[0].shape[-1]
    n_group = [h.shape[0] * h.shape[1] for h in h2_list]
    n_tok = sum(n_group)
    nt = -(-n_tok // tt)
    ntp = nt * tt
    ids = jnp.concatenate(ids_list, axis=1)[:TOP_K]
    ids = jnp.pad(ids, ((0, 8 - TOP_K), (0, ntp - n_tok)), constant_values=N_EXPERTS)
    ids3 = ids.reshape(8, nt, tt).transpose(1, 0, 2)
    pos3, cnt = _route_positions(ids3, tt)
    counts = cnt[:, 0].astype(I32)
    padded = (counts + tm - 1) // tm * tm
    ends_p = jnp.cumsum(padded)
    starts_p = ends_p - padded
    na = n_tok * TOP_K
    nblk = -(-(na + N_EXPERTS * (tm - 1)) // tm)
    n_rows = nblk * tm
    ids4 = ids3[:, :TOP_K, :]
    valid = ids4 < N_EXPERTS
    e_ar = jnp.arange(N_EXPERTS, dtype=I32)
    start_of = jnp.sum(jnp.where(ids4[..., None] == e_ar, starts_p, 0), axis=-1)
    dest = start_of + pos3[:, :TOP_K, :]
    tok_id = (jnp.arange(nt, dtype=I32)[:, None, None] * tt + jnp.arange(tt, dtype=I32)[None, None, :])
    pad_rank = (tok_id - n_tok) * TOP_K + jnp.arange(TOP_K, dtype=I32)[None, :, None]
    dest_scatter = jnp.where(valid, dest, n_rows + pad_rank).reshape(nt, TOP_K * tt)
    dest_gather = jnp.where(valid, dest, 0).reshape(nt, TOP_K * tt)
    n_trash = (ntp - n_tok) * TOP_K
    blk_start = jnp.arange(nblk, dtype=I32) * tm
    blk_e = jnp.minimum(jnp.sum(jnp.where(ends_p[None, :] <= blk_start[:, None], 1, 0), axis=1),
                        N_EXPERTS - 1).astype(I32)
    n_used = (ends_p[-1:] // tm).astype(I32)

    h_all = jnp.concatenate([h.reshape(-1, d) for h in h2_list], axis=0)
    h_all = jnp.pad(h_all, ((0, ntp - n_tok), (0, 0)))
    zero_starts = jnp.where(padded > 0, ends_p - tm, n_rows - tm).astype(I32)
    xs = _scatter_rows(zero_starts, n_used, dest_scatter, h_all, n_rows + max(n_trash, 8), nblk, tt, tm)
    y_rows = _expert_matmul(blk_e, n_used, xs, prm["w_gu"], prm["b_gu"], prm["w_dn"], prm["b_dn"],
                            n_rows, tm)

    gates = jnp.concatenate(gates_list, axis=1)[:TOP_K]
    gates = jnp.pad(gates, ((0, 0), (0, ntp - n_tok)))
    gate_t = gates.reshape(TOP_K, nt, tt).transpose(1, 2, 0)
    outs = []
    tok0 = 0
    for x1, gt, n in zip(x1_list, gt_list, n_group):
        bx, sx, _ = x1.shape
        tile0 = tok0 // tt
        if sx % tt:
            padn = tt - sx
            x1p = jnp.pad(x1, ((0, 0), (0, padn), (0, 0)))
            gtp = jnp.pad(gt, ((0, 0), (0, padn), (0, 0)))
            o = _combine(dest_gather, y_rows, x1p, gtp, gate_t[tile0:tile0 + 1], tile0, tt)[:, :sx]
        else:
            o = _combine(dest_gather, y_rows, x1, gt, gate_t[tile0:tile0 + bx * (sx // tt)], tile0, tt)
        outs.append(o)
        tok0 += n
    return outs


def _flatten_rows(a):
    return a.transpose(1, 0, 2).reshape(8, -1)


def kernel(x_prompt, x_sample, c_prompt, c_sample, cache_diff_k, cache_diff_v, cache_nsa_k, cache_nsa_v, state_win_kv, page_table, attn_norm_g, ffn_norm_g, ada_w, ada_b, w_in, w_out, diff_q_norm_g, diff_k_norm_g, diff_lambda, diff_sub_norm_g, nsa_q_norm_g, nsa_k_norm_g, nsa_ck_norm_g, nsa_cmp_pe, nsa_cmp_w, router_w, router_b, expert_w_gu, expert_b_gu, expert_w_down, expert_b_down):
    depth = w_in.shape[0]
    assert depth == 1, "single-layer trunk"
    bp, sp, d = x_prompt.shape
    bs, ss, _ = x_sample.shape
    assert ss == 1
    n_pages = page_table.shape[1]
    past_len = n_pages * PAGE
    w_buf = state_win_kv.shape[2]
    n_pool = cache_diff_k.shape[1]
    l = 0
    lam_init = 0.8 - 0.6 * math.exp(-0.3 * l)
    d_ff = expert_w_down.shape[2]

    prm = {
        "attn_g": attn_norm_g[l].reshape(1, d),
        "ffn_g": ffn_norm_g[l].reshape(1, d),
        "w_in": jnp.pad(w_in[l], ((0, 0), (0, IN_PAD - w_in.shape[2]))).astype(BF16),
        "gd": jnp.stack([jnp.tile(diff_q_norm_g[l], 16), jnp.tile(diff_k_norm_g[l], 16)]),
        "gn": jnp.tile(nsa_q_norm_g[l], 8).reshape(1, 512),
        "gk": jnp.concatenate([jnp.tile(nsa_k_norm_g[l, 0], 2), jnp.tile(nsa_k_norm_g[l, 1], 2)]).reshape(1, 256),
        "m32": _group_mean_matrix(A_DQK, BF16),
        "m64": _group_mean_matrix(HEAD_DIM, BF16),
        "m32_f32": _group_mean_matrix(A_DQK, F32),
        "m64_f32": _group_mean_matrix(HEAD_DIM, F32),
        "w_out": w_out[l].astype(BF16),
        "router_wt": router_w[l].T,
        "router_b": router_b[l].reshape(N_EXPERTS, 1),
        "w_gu": expert_w_gu[l],
        "b_gu": expert_b_gu[l].reshape(N_EXPERTS, 1, 2 * d_ff),
        "w_dn": expert_w_down[l],
        "b_dn": expert_b_down[l].reshape(N_EXPERTS, 1, d),
    }
    lam_p = diff_lambda[l]
    sub_g2 = jnp.tile(diff_sub_norm_g[l], 2).reshape(1, LANES)
    sub_g8 = jnp.tile(diff_sub_norm_g[l], 8).reshape(1, 512)
    pe = nsa_cmp_pe[l]
    cw = nsa_cmp_w[l]
    ckg = nsa_ck_norm_g[l].reshape(1, HEAD_DIM)

    n_c = bp + bs
    n_cp = -(-n_c // 8) * 8
    c_all = jnp.pad(jnp.concatenate([c_prompt, c_sample], axis=0), ((0, n_cp - n_c), (0, 0)))
    mod = _modulation(c_all, ada_w[l], ada_b[l])
    mod_p = mod[:bp].reshape(bp, 1, 6, d)
    mod_s = mod[bp:n_c].reshape(1, bs, 6, d)
    sh1p, sc1p, gt1p, sh2p, sc2p, gt2p = [mod_p[:, :, i] for i in range(6)]
    sh1s, sc1s, gt1s, sh2s, sc2s, gt2s = [mod_s[:, :, i] for i in range(6)]

    ts = min(512, sp)
    pos_p = jnp.arange(sp, dtype=I32)
    (qa, dk, dkb, dv, dvb, qn, nk, nv, win, kb, vb, gate) = _in_projection(
        x_prompt, sh1p, sc1p, pos_p, prm, ts)
    tq = min(512, sp)
    o_a = _diff_attention_prompt(qa, dkb, dvb, lam_p, sub_g2, lam_init, tq, min(512, sp))
    tqn = min(512, sp)
    o_b = _nsa_prompt(qn, gate, nk, nv, kb, vb, pe, cw, ckg, tqn, min(512, sp))
    x1p, h2p, idsp, gatesp = _out_projection(o_a, o_b, x_prompt, gt1p, sh2p, sc2p, prm, ts)

    xs_ = x_sample.reshape(1, bs, d)
    pos_s = jnp.full((bs,), past_len, I32)
    (qa_s, dk_s, _, dv_s, _, qn_s, nk_s, nv_s, win_s, _, _, gate_s) = _in_projection(
        xs_, sh1s, sc1s, pos_s, prm, bs, sample=True)
    ckt = jnp.transpose(cache_diff_k[l], (0, 2, 3, 1)).reshape(n_pool, 512, PAGE)
    cvt = jnp.transpose(cache_diff_v[l], (0, 2, 3, 1)).reshape(n_pool, 512, PAGE)
    nkt = jnp.transpose(cache_nsa_k[l], (0, 2, 3, 4, 1)).reshape(n_pool, 256, PAGE)
    nvt = jnp.transpose(cache_nsa_v[l], (0, 2, 3, 4, 1)).reshape(n_pool, 256, PAGE)
    state = state_win_kv[l].reshape(bs, w_buf, 256)
    state_t = jnp.transpose(state_win_kv[l], (0, 2, 3, 4, 1)).reshape(bs, 256, w_buf)
    pps = 8 if n_pages % 8 == 0 else 1
    as3 = lambda a: a.reshape(bs, 1, a.shape[-1])
    col = lambda a: a.reshape(bs, a.shape[-1], 1)
    o_a_s = _diff_attention_sample(page_table, lam_p, col(qa_s[0]), col(dk_s[0]), col(dv_s[0]),
                                   diff_sub_norm_g[l].reshape(HEAD_DIM, 1), ckt, cvt, lam_init, pps)
    ocmp_s, sel_s = _nsa_decode_cmp(page_table, as3(qn_s[0]), col(nk_s[0][:, :LANES]),
                                    col(nv_s[0][:, :LANES]), jnp.transpose(pe, (0, 2, 1)),
                                    jnp.transpose(cw, (0, 2, 1)), ckg.reshape(HEAD_DIM, 1),
                                    nkt, nvt, past_len, pps)
    n_sel = min(N_SELECT, past_len // CMP_BLOCK + 1)
    sel2 = jnp.concatenate([sel_s[:, 0, :n_sel], sel_s[:, 1, :n_sel]], axis=1)
    o_b_s = _nsa_decode_mix(page_table, sel2, as3(qn_s[0]), as3(gate_s[0]),
                            ocmp_s.reshape(bs, B_KV_HEADS, B_GROUP, HEAD_DIM), as3(nk_s[0]), as3(nv_s[0]),
                            as3(win_s[0]), state_t, nkt, nvt, past_len, n_sel)
    x1s, h2s, idss, gatess = _out_projection(o_a_s.astype(BF16).reshape(1, bs, 512),
                                             o_b_s.astype(BF16).reshape(1, bs, 512),
                                             xs_, gt1s, sh2s, sc2s, prm, bs)

    y_p, y_s = _moe([h2p, h2s], [_flatten_rows(idsp), _flatten_rows(idss)],
                    [_flatten_rows(gatesp), _flatten_rows(gatess)], [x1p, x1s], [gt2p, gt2s],
                    prm, 256, 512)

    p_win = win[:, sp - w_buf:] if sp >= w_buf else jnp.pad(win, ((0, 0), (w_buf - sp, 0), (0, 0)))
    s_win = jnp.concatenate([state, win_s[0][:, None, :]], axis=1)[:, 1:]
    return (y_p, y_s.reshape(bs, 1, d),
            dk.reshape(1, bp, sp, A_HEADS, 2 * A_DQK), dv.reshape(1, bp, sp, A_HEADS, HEAD_DIM),
            nk.reshape(1, bp, sp, 2, B_KV_HEADS, HEAD_DIM), nv.reshape(1, bp, sp, 2, B_KV_HEADS, HEAD_DIM),
            p_win.reshape(1, bp, w_buf, 2, B_KV_HEADS, HEAD_DIM),
            dk_s.reshape(1, bs, 1, A_HEADS, 2 * A_DQK), dv_s.reshape(1, bs, 1, A_HEADS, HEAD_DIM),
            nk_s.reshape(1, bs, 1, 2, B_KV_HEADS, HEAD_DIM), nv_s.reshape(1, bs, 1, 2, B_KV_HEADS, HEAD_DIM),
            s_win.reshape(1, bs, w_buf, 2, B_KV_HEADS, HEAD_DIM))
```
